```python
import math
import jax, jax.numpy as jnp
from jax import lax
import numpy as np

D_MODEL = 1024
BATCH = 8
SEQ = 2048
DEPTH = 1
DEC_BATCH = 8
DEC_SEQ = 32
PAST_LEN = 2048

CHUNK = 64
EPS = 1e-6
D_CONV = D_MODEL
CONV_W = 3
N_HEADS = 8
HEAD_DIM = 64
V_DIM = 2 * HEAD_DIM
QK_WIDTH = N_HEADS * 2 * HEAD_DIM
ATTN_WIDTH = N_HEADS * V_DIM
ROPE_THETA = 10000.0
Q_BLOCK = 128
IN_SPLITS = [D_CONV, 2 * D_CONV, 3 * D_CONV,
             3 * D_CONV + QK_WIDTH,
             3 * D_CONV + 2 * QK_WIDTH,
             3 * D_CONV + 2 * QK_WIDTH + ATTN_WIDTH]
IN_WIDTH = 3 * D_CONV + 2 * QK_WIDTH + ATTN_WIDTH + 2 * D_MODEL
N_KEYS = 128
N_EXPERTS = N_KEYS * N_KEYS
PEER_HEADS = 8
PEER_TOPK = 16
D_KEY = 256
PEER_BLOCK = 128

kernel_name = 'hybrid_conv_diffattn_peer_stream_step'


def lambda_init(layer_idx):
    return 0.8 - 0.6 * math.exp(-0.3 * layer_idx)


def rmsnorm(x, g):
    xf = x.astype(jnp.float32)
    r = lax.rsqrt(jnp.mean(xf * xf, axis=-1, keepdims=True) + EPS)
    return (xf * r).astype(x.dtype) * g


def rope(x, pos):
    half = HEAD_DIM // 2
    inv = ROPE_THETA ** (-jnp.arange(half, dtype=jnp.float32) / half)
    ang = pos.astype(jnp.float32)[:, None] * inv[None, :]
    cos = jnp.cos(ang)[:, None, None, :].astype(x.dtype)
    sin = jnp.sin(ang)[:, None, None, :].astype(x.dtype)
    x1, x2 = x[..., :half], x[..., half:]
    return jnp.concatenate([x1 * cos - x2 * sin, x2 * cos + x1 * sin], axis=-1)


def conv_branch(bg, cg, hh, conv_w, conv_state):
    u = cg * hh
    up = jnp.concatenate([conv_state, u], axis=1)
    T = u.shape[1]
    y = sum(conv_w[j] * up[:, j:j + T] for j in range(CONV_W))
    return bg * y, up[:, -(CONV_W - 1):]


def diff_attn_block(q, k, v, q_pos, k_pos, lam):
    s = jnp.einsum('bqhcd,bkhcd->bhcqk', q, k).astype(jnp.float32) * (HEAD_DIM ** -0.5)
    visible = k_pos[None, :] < (q_pos[:, None] // CHUNK + 1) * CHUNK
    s = jnp.where(visible, s, -jnp.inf)
    p = jax.nn.softmax(s, axis=-1)
    a = p[:, :, 0] - lam * p[:, :, 1]
    return jnp.einsum('bhqk,bkhe->bqhe', a.astype(v.dtype), v)


def peer_ffn(xt, wq, q_g, subkeys, u, v):
    n = xt.shape[0]
    q = rmsnorm((xt @ wq).reshape(n, PEER_HEADS, D_KEY), q_g)
    q = q.reshape(n, PEER_HEADS, 2, D_KEY // 2)
    s = jnp.einsum('nhcd,ckd->nhck', q, subkeys).astype(jnp.float32)
    s_top, i_top = lax.top_k(s, PEER_TOPK)
    cand = s_top[:, :, 0, :, None] + s_top[:, :, 1, None, :]
    cand_idx = i_top[:, :, 0, :, None] * N_KEYS + i_top[:, :, 1, None, :]
    best, sel = lax.top_k(cand.reshape(n, PEER_HEADS, PEER_TOPK * PEER_TOPK), PEER_TOPK)
    experts = jnp.take_along_axis(cand_idx.reshape(n, PEER_HEADS, -1), sel, axis=-1)
    g = jax.nn.softmax(best, axis=-1).reshape(n, -1)
    e = experts.reshape(n, -1)
    ue = jnp.take(u, e, axis=0)
    h = jax.nn.gelu(jnp.einsum('nkd,nd->nk', ue, xt), approximate=False)
    ve = jnp.take(v, e, axis=0)
    return jnp.einsum('nk,nkd->nd', (g * h).astype(xt.dtype), ve)


def trunk_layer(x, pos, conv_state, k_past, v_past, w, lam_init):
    (norm1_g, w_in, b_gate, conv_w, q_norm_g, k_norm_g, lam_q1, lam_k1, lam_q2, lam_k2,
     subln_g, w_conv_out, w_attn_out, w_o, norm2_g, peer_wq, peer_q_g, peer_subkeys,
     peer_u, peer_v) = w
    B, T, _ = x.shape
    xn = rmsnorm(x, norm1_g)
    z = xn @ w_in
    hb, hc, hh, q, k, v, gz = jnp.split(z, IN_SPLITS, axis=-1)
    conv_y, new_conv = conv_branch(hb, hc, hh, conv_w, conv_state)
    q = rope(rmsnorm(q.reshape(B, T, N_HEADS, 2, HEAD_DIM), q_norm_g), pos)
    k = rope(rmsnorm(k.reshape(B, T, N_HEADS, 2, HEAD_DIM), k_norm_g), pos)
    v = v.reshape(B, T, N_HEADS, V_DIM)
    lam = (jnp.exp(jnp.sum(lam_q1.astype(jnp.float32) * lam_k1.astype(jnp.float32)))
           - jnp.exp(jnp.sum(lam_q2.astype(jnp.float32) * lam_k2.astype(jnp.float32)))
           + lam_init)
    if k_past is None:
        k_pos = pos
        nb = T // Q_BLOCK
        qb = jnp.moveaxis(q.reshape(B, nb, Q_BLOCK, N_HEADS, 2, HEAD_DIM), 1, 0)
        pb = pos.reshape(nb, Q_BLOCK)
        o = lax.map(lambda a: diff_attn_block(a[0], k, v, a[1], k_pos, lam), (qb, pb))
        o = jnp.moveaxis(o, 0, 1).reshape(B, T, N_HEADS, V_DIM)
    else:
        k_all = jnp.concatenate([k_past, k], axis=1)
        v_all = jnp.concatenate([v_past, v], axis=1)
        k_pos = jnp.arange(k_all.shape[1], dtype=jnp.int32)
        o = diff_attn_block(q, k_all, v_all, pos, k_pos, lam)
    o = (rmsnorm(o, subln_g) * (1.0 - lam_init)).reshape(B, T, ATTN_WIDTH)
    gates = jax.nn.sigmoid(gz.reshape(B, T, 2, D_MODEL) + b_gate)
    mix = gates[:, :, 0] * (conv_y @ w_conv_out) + gates[:, :, 1] * (o @ w_attn_out)
    x = x + mix @ w_o
    xn2 = rmsnorm(x, norm2_g)
    if k_past is None:
        blocks = xn2.reshape(-1, PEER_BLOCK, D_MODEL)
        f = lax.map(lambda t: peer_ffn(t, peer_wq, peer_q_g, peer_subkeys, peer_u, peer_v), blocks)
    else:
        f = peer_ffn(xn2.reshape(-1, D_MODEL), peer_wq, peer_q_g, peer_subkeys, peer_u, peer_v)
    x = x + f.reshape(B, T, D_MODEL)
    return x, k, v, new_conv


def setup_inputs(seed: int = 0) -> dict:
    key = jax.random.key(seed)
    ks = jax.random.split(key, 24)
    L = DEPTH
    nrm = lambda k, shape, scale: jax.random.normal(k, shape, jnp.float32) * scale
    return {
        'x_prompt': nrm(ks[0], (BATCH, SEQ, D_MODEL), 1.0),
        'x_sample': nrm(ks[1], (DEC_BATCH, DEC_SEQ, D_MODEL), 1.0),
        'cache_k': nrm(ks[2], (L, DEC_BATCH, PAST_LEN, N_HEADS, 2, HEAD_DIM), 1.0),
        'cache_v': nrm(ks[3], (L, DEC_BATCH, PAST_LEN, N_HEADS, V_DIM), 1.0),
        'state_conv': nrm(ks[4], (L, DEC_BATCH, CONV_W - 1, D_CONV), 1.0),
        'norm1_g': 1.0 + nrm(ks[5], (L, D_MODEL), 0.02),
        'w_in': nrm(ks[6], (L, D_MODEL, IN_WIDTH), D_MODEL ** -0.5),
        'b_gate': nrm(ks[7], (L, 2, D_MODEL), 0.02),
        'conv_w': nrm(ks[8], (L, CONV_W, D_CONV), CONV_W ** -0.5),
        'q_norm_g': 1.0 + nrm(ks[9], (L, HEAD_DIM), 0.02),
        'k_norm_g': 1.0 + nrm(ks[10], (L, HEAD_DIM), 0.02),
        'lam_q1': nrm(ks[11], (L, HEAD_DIM), 0.1),
        'lam_k1': nrm(ks[12], (L, HEAD_DIM), 0.1),
        'lam_q2': nrm(ks[13], (L, HEAD_DIM), 0.1),
        'lam_k2': nrm(ks[14], (L, HEAD_DIM), 0.1),
        'subln_g': 1.0 + nrm(ks[15], (L, V_DIM), 0.02),
        'w_conv_out': nrm(ks[16], (L, D_CONV, D_MODEL), D_CONV ** -0.5),
        'w_attn_out': nrm(ks[17], (L, ATTN_WIDTH, D_MODEL), ATTN_WIDTH ** -0.5),
        'w_o': nrm(ks[18], (L, D_MODEL, D_MODEL), D_MODEL ** -0.5),
        'norm2_g': 1.0 + nrm(ks[19], (L, D_MODEL), 0.02),
        'peer_wq': nrm(ks[20], (L, D_MODEL, PEER_HEADS * D_KEY), D_MODEL ** -0.5),
        'peer_q_g': 1.0 + nrm(ks[21], (L, D_KEY), 0.02),
        'peer_subkeys': nrm(ks[22], (L, 2, N_KEYS, D_KEY // 2), (D_KEY // 2) ** -0.5),
        'peer_u': nrm(jax.random.fold_in(ks[23], 0), (L, N_EXPERTS, D_MODEL), D_MODEL ** -0.5),
        'peer_v': nrm(jax.random.fold_in(ks[23], 1), (L, N_EXPERTS, D_MODEL), PEER_HEADS ** -0.5),
    }


def reference(x_prompt, x_sample, cache_k, cache_v, state_conv, norm1_g, w_in, b_gate, conv_w,
              q_norm_g, k_norm_g, lam_q1, lam_k1, lam_q2, lam_k2, subln_g, w_conv_out,
              w_attn_out, w_o, norm2_g, peer_wq, peer_q_g, peer_subkeys, peer_u, peer_v):
    y_p, y_s = x_prompt, x_sample
    past_len = cache_k.shape[2]
    pos_p = jnp.arange(x_prompt.shape[1], dtype=jnp.int32)
    pos_s = past_len + jnp.arange(x_sample.shape[1], dtype=jnp.int32)
    kp, vp, cp, ksm, vsm, csm = [], [], [], [], [], []
    for l in range(DEPTH):
        w = (norm1_g[l], w_in[l], b_gate[l], conv_w[l], q_norm_g[l], k_norm_g[l], lam_q1[l],
             lam_k1[l], lam_q2[l], lam_k2[l], subln_g[l], w_conv_out[l], w_attn_out[l], w_o[l],
             norm2_g[l], peer_wq[l], peer_q_g[l], peer_subkeys[l], peer_u[l], peer_v[l])
        lam0 = lambda_init(l)
        zero_conv = jnp.zeros((x_prompt.shape[0], CONV_W - 1, D_CONV), x_prompt.dtype)
        y_p, k_new, v_new, c_new = trunk_layer(y_p, pos_p, zero_conv, None, None, w, lam0)
        kp.append(k_new); vp.append(v_new); cp.append(c_new)
        y_s, k_new, v_new, c_new = trunk_layer(y_s, pos_s, state_conv[l], cache_k[l], cache_v[l], w, lam0)
        ksm.append(k_new); vsm.append(v_new); csm.append(c_new)
    return (y_p, y_s, jnp.stack(kp), jnp.stack(vp), jnp.stack(cp),
            jnp.stack(ksm), jnp.stack(vsm), jnp.stack(csm))
```

```python
import functools
import math

import jax
import jax.numpy as jnp
import numpy as np
from jax import lax
from jax.experimental import pallas as pl
from jax.experimental.pallas import tpu as pltpu

EPS = 1e-6
CHUNK = 64
CHUNK_SHIFT = 6
ROPE_THETA = 10000.0
PEER_TOPK = 16
LAYER_IDX = 0

V7X_VMEM_LIMIT = 56 * 1024 * 1024
LANES = 128

F32 = jnp.float32
BF16 = jnp.bfloat16


def _lambda_init(layer_idx):
    return 0.8 - 0.6 * math.exp(-0.3 * layer_idx)


def _dot(a, b):
    return jnp.dot(a, b, preferred_element_type=F32)


def _dot_nt(a, b):
    return lax.dot_general(a, b, (((1,), (1,)), ((), ())), preferred_element_type=F32)


def _rmsnorm_rows(x, g):
    r = lax.rsqrt(jnp.mean(x * x, axis=-1, keepdims=True) + EPS)
    return (x * r) * g


def _cparams(sem):
    return pltpu.CompilerParams(dimension_semantics=sem, vmem_limit_bytes=V7X_VMEM_LIMIT)


def _branch_a_kernel(x_ref, g1_ref, wb_ref, wc_ref, wh_ref, wga_ref, wgb_ref, bg_ref, cw_ref, cs_ref,
                     wco_ref, gaa_ref, gb_ref, nc_ref, carry_ref):
    t = pl.program_id(1)

    @pl.when(t == 0)
    def _():
        carry_ref[...] = cs_ref[...]

    x = x_ref[...]
    xn = _rmsnorm_rows(x, g1_ref[...]).astype(BF16)
    hb = _dot(xn, wb_ref[...])
    hc = _dot(xn, wc_ref[...])
    hh = _dot(xn, wh_ref[...])
    u = hc * hh
    tm = u.shape[0]
    row = lax.broadcasted_iota(jnp.int32, u.shape, 0)
    c0 = carry_ref[0:1, :]
    c1 = carry_ref[1:2, :]
    u1 = jnp.where(row == 0, c1, pltpu.roll(u, 1, 0))
    u2 = jnp.where(row == 0, c0, jnp.where(row == 1, c1, pltpu.roll(u, 2, 0)))
    y = cw_ref[0:1, :] * u2 + cw_ref[1:2, :] * u1 + cw_ref[2:3, :] * u
    conv_y = hb * y
    new_c = u[tm - 2:tm, :]
    carry_ref[...] = new_c
    nc_ref[...] = new_c
    a = _dot(conv_y.astype(BF16), wco_ref[...])
    ga = jax.nn.sigmoid(_dot(xn, wga_ref[...]) + bg_ref[0:1, :])
    gb = jax.nn.sigmoid(_dot(xn, wgb_ref[...]) + bg_ref[1:2, :])
    gaa_ref[...] = ga * a
    gb_ref[...] = gb


def _branch_a(x, g1, w_in_bf, b_gate, conv_w, conv_state, wco_bf, tm):
    B, T, D = x.shape
    nt = T // tm
    wspec = lambda j: pl.BlockSpec((D, D), lambda b, t, j=j: (0, j))
    tile = pl.BlockSpec((None, tm, D), lambda b, t: (b, t, 0))
    full2 = lambda r: pl.BlockSpec((r, D), lambda b, t: (0, 0))
    return pl.pallas_call(
        _branch_a_kernel,
        grid=(B, nt),
        in_specs=[tile, full2(1), wspec(0), wspec(1), wspec(2), wspec(6), wspec(7), full2(2), full2(3),
                  pl.BlockSpec((None, 2, D), lambda b, t: (b, 0, 0)), pl.BlockSpec((D, D), lambda b, t: (0, 0))],
        out_specs=[tile, tile, pl.BlockSpec((None, 2, D), lambda b, t: (b, 0, 0))],
        out_shape=[jax.ShapeDtypeStruct((B, T, D), F32), jax.ShapeDtypeStruct((B, T, D), F32),
                   jax.ShapeDtypeStruct((B, 2, D), F32)],
        scratch_shapes=[pltpu.VMEM((2, D), F32)],
        compiler_params=_cparams(("arbitrary", "arbitrary")),
        name="branch_a",
    )(x, g1, w_in_bf, w_in_bf, w_in_bf, w_in_bf, w_in_bf, b_gate, conv_w, conv_state, wco_bf)


def _qkv_kernel(x_ref, g1_ref, wq_ref, wk_ref, wv_ref, qg_ref, kg_ref, cos_ref, sin_ref, grp_ref,
                k_ref, v_ref, qb_ref, kb_ref, vb_ref, *, head_dim):
    x = x_ref[...]
    xn = _rmsnorm_rows(x, g1_ref[...]).astype(BF16)
    q = _dot(xn, wq_ref[...])
    k = _dot(xn, wk_ref[...])
    v = _dot(xn, wv_ref[...])
    D = q.shape[1]
    half = head_dim // 2
    reps = D // cos_ref.shape[1]
    cos = jnp.tile(cos_ref[...], (1, reps))
    sin = jnp.tile(sin_ref[...], (1, reps))
    lane = lax.broadcasted_iota(jnp.int32, q.shape, 1)
    first_half = (lane & half) == 0
    grp = grp_ref[...]
    gw = grp.shape[0]

    def headnorm_rope(z, g):
        z2 = z * z
        hi = z2.astype(BF16)
        lo = (z2 - hi.astype(F32)).astype(BF16)
        parts = []
        for s in range(D // gw):
            sl = slice(s * gw, (s + 1) * gw)
            parts.append(_dot(hi[:, sl], grp) + _dot(lo[:, sl], grp))
        ss = jnp.concatenate(parts, axis=1)
        r = lax.rsqrt(ss * (1.0 / head_dim) + EPS)
        zn = (z * r) * g
        sw = jnp.where(first_half, pltpu.roll(zn, D - half, 1), pltpu.roll(zn, half, 1))
        return zn * cos + sw * sin

    qr = headnorm_rope(q, qg_ref[...])
    kr = headnorm_rope(k, kg_ref[...])
    k_ref[...] = kr
    v_ref[...] = v
    qb_ref[...] = (qr * (head_dim ** -0.5)).astype(BF16)
    kb_ref[...] = kr.astype(BF16)
    vb_ref[...] = v.astype(BF16)


def _qkv(x, g1, w_in_bf, qg_row, kg_row, cos_t, sin_t, grp, tm, head_dim):
    B, T, D = x.shape
    nt = T // tm
    wspec = lambda j: pl.BlockSpec((D, D), lambda b, t, j=j: (0, j))
    tile = pl.BlockSpec((None, tm, D), lambda b, t: (b, t, 0))
    row = pl.BlockSpec((1, D), lambda b, t: (0, 0))
    tab = pl.BlockSpec((tm, cos_t.shape[1]), lambda b, t: (t, 0))
    return pl.pallas_call(
        functools.partial(_qkv_kernel, head_dim=head_dim),
        grid=(B, nt),
        in_specs=[tile, row, wspec(3), wspec(4), wspec(5), row, row, tab, tab,
                  pl.BlockSpec(grp.shape, lambda b, t: (0, 0))],
        out_specs=[tile] * 5,
        out_shape=[jax.ShapeDtypeStruct((B, T, D), F32)] * 2 + [jax.ShapeDtypeStruct((B, T, D), BF16)] * 3,
        compiler_params=_cparams(("arbitrary", "arbitrary")),
        name="qkv",
    )(x, g1, w_in_bf, w_in_bf, w_in_bf, qg_row, kg_row, cos_t, sin_t, grp)


def _stack_halves(q, head_dim):
    lane = lax.broadcasted_iota(jnp.int32, q.shape, 1)
    zero = jnp.zeros_like(q)
    return jnp.concatenate([jnp.where(lane < head_dim, q, zero), jnp.where(lane >= head_dim, q, zero)], axis=0)


def _lambda_value(lam_ref, lam0):
    l = lam_ref[...]
    s1 = jnp.sum(l[0:1, :] * l[1:2, :], axis=-1, keepdims=True)
    s2 = jnp.sum(l[2:3, :] * l[3:4, :], axis=-1, keepdims=True)
    return jnp.exp(s1) - jnp.exp(s2) + lam0


def _attn_finish(acc, l, tq, lam, sg, lam0):
    o = acc[:tq] / l[:tq] - lam * (acc[tq:] / l[tq:])
    return _rmsnorm_rows(o, sg) * (1.0 - lam0)


def _attn_prompt_kernel(q_ref, k_ref, v_ref, lam_ref, sg_ref, o_ref, *, tq, head_dim, lam0):
    qi = pl.program_id(2)
    qs = _stack_halves(q_ref[...], head_dim)

    def block(j, carry, masked):
        m, l, acc = carry
        start = pl.multiple_of(j * tq, tq)
        kb = k_ref[pl.ds(start, tq), :]
        vb = v_ref[pl.ds(start, tq), :]
        s = _dot_nt(qs, kb)
        if masked:
            r = lax.broadcasted_iota(jnp.int32, s.shape, 0)
            c = lax.broadcasted_iota(jnp.int32, s.shape, 1)
            rr = jnp.where(r >= tq, r - tq, r)
            vis = (c >> CHUNK_SHIFT) <= (rr >> CHUNK_SHIFT)
            s = jnp.where(vis, s, -1e30)
        m_new = jnp.maximum(m, jnp.max(s, axis=-1, keepdims=True))
        alpha = jnp.exp(m - m_new)
        p = jnp.exp(s - m_new)
        l = alpha * l + jnp.sum(p, axis=-1, keepdims=True)
        acc = alpha * acc + _dot(p.astype(BF16), vb)
        return m_new, l, acc

    init = (jnp.full((2 * tq, 1), -1e30, F32), jnp.zeros((2 * tq, 1), F32),
            jnp.zeros((2 * tq, v_ref.shape[1]), F32))
    carry = lax.fori_loop(0, qi, lambda j, c: block(j, c, False), init)
    _, l, acc = block(qi, carry, True)
    o_ref[...] = _attn_finish(acc, l, tq, _lambda_value(lam_ref, lam0), sg_ref[...], lam0).astype(o_ref.dtype)


def _attn_prompt(qb, kb, vb, lam4, sg_row, n_heads, tq, lam0):
    B, T, D = qb.shape
    hw = D // n_heads
    head_dim = hw // 2
    assert tq % CHUNK == 0 and T % tq == 0
    return pl.pallas_call(
        functools.partial(_attn_prompt_kernel, tq=tq, head_dim=head_dim, lam0=lam0),
        grid=(B, n_heads, T // tq),
        in_specs=[pl.BlockSpec((None, tq, hw), lambda b, h, i: (b, i, h)),
                  pl.BlockSpec((None, T, hw), lambda b, h, i: (b, 0, h)),
                  pl.BlockSpec((None, T, hw), lambda b, h, i: (b, 0, h)),
                  pl.BlockSpec(lam4.shape, lambda b, h, i: (0, 0)),
                  pl.BlockSpec((1, hw), lambda b, h, i: (0, 0))],
        out_specs=pl.BlockSpec((None, tq, hw), lambda b, h, i: (b, i, h)),
        out_shape=jax.ShapeDtypeStruct((B, T, D), BF16),
        compiler_params=_cparams(("arbitrary", "arbitrary", "arbitrary")),
        name="attn_prompt",
    )(qb, kb, vb, lam4, sg_row)


def _attn_sample_kernel(q_ref, kc_ref, vc_ref, kn_ref, vn_ref, lam_ref, sg_ref, o_ref, *, head_dim, lam0):
    ts = q_ref.shape[0]
    past = kc_ref.shape[0]
    qs = _stack_halves(q_ref[...], head_dim)
    sc = _dot_nt(qs, kc_ref[...].astype(BF16))
    sn = _dot_nt(qs, kn_ref[...])

    def visible(shape, k_off):
        r = lax.broadcasted_iota(jnp.int32, shape, 0)
        c = lax.broadcasted_iota(jnp.int32, shape, 1)
        q_pos = past + jnp.where(r >= ts, r - ts, r)
        return ((c + k_off) >> CHUNK_SHIFT) <= (q_pos >> CHUNK_SHIFT)

    sc = jnp.where(visible(sc.shape, 0), sc, -1e30)
    sn = jnp.where(visible(sn.shape, past), sn, -1e30)
    m = jnp.maximum(jnp.max(sc, axis=-1, keepdims=True), jnp.max(sn, axis=-1, keepdims=True))
    pc = jnp.exp(sc - m)
    pn = jnp.exp(sn - m)
    l = jnp.sum(pc, axis=-1, keepdims=True) + jnp.sum(pn, axis=-1, keepdims=True)
    acc = _dot(pc.astype(BF16), vc_ref[...].astype(BF16)) + _dot(pn.astype(BF16), vn_ref[...])
    o_ref[...] = _attn_finish(acc, l, ts, _lambda_value(lam_ref, lam0), sg_ref[...], lam0).astype(o_ref.dtype)


def _attn_sample(qb, kb, vb, cache_k2, cache_v2, lam4, sg_row, n_heads, lam0):
    B, Ts, D = qb.shape
    P = cache_k2.shape[1]
    hw = D // n_heads
    new = pl.BlockSpec((None, Ts, hw), lambda b, h: (b, 0, h))
    old = pl.BlockSpec((None, P, hw), lambda b, h: (b, 0, h))
    return pl.pallas_call(
        functools.partial(_attn_sample_kernel, head_dim=hw // 2, lam0=lam0),
        grid=(B, n_heads),
        in_specs=[new, old, old, new, new, pl.BlockSpec(lam4.shape, lambda b, h: (0, 0)),
                  pl.BlockSpec((1, hw), lambda b, h: (0, 0))],
        out_specs=new,
        out_shape=jax.ShapeDtypeStruct((B, Ts, D), BF16),
        compiler_params=_cparams(("arbitrary", "arbitrary")),
        name="attn_sample",
    )(qb, cache_k2, cache_v2, kb, vb, lam4, sg_row)


def _merge_kernel(x_ref, gaa_ref, gb_ref, on_ref, wao_ref, wo_ref, g2_ref, x1_ref, xn2_ref):
    b = _dot(on_ref[...], wao_ref[...])
    mix = gaa_ref[...] + gb_ref[...] * b
    x1 = x_ref[...] + _dot(mix.astype(BF16), wo_ref[...])
    x1_ref[...] = x1
    xn2_ref[...] = _rmsnorm_rows(x1, g2_ref[...]).astype(BF16)


def _merge(x2, gaa2, gb2, on2, wao_bf, wo_bf, g2, tm):
    N, D = x2.shape
    tile = pl.BlockSpec((tm, D), lambda t: (t, 0))
    wfull = pl.BlockSpec((D, D), lambda t: (0, 0))
    return pl.pallas_call(
        _merge_kernel,
        grid=(N // tm,),
        in_specs=[tile, tile, tile, tile, wfull, wfull, pl.BlockSpec((1, D), lambda t: (0, 0))],
        out_specs=[tile, tile],
        out_shape=[jax.ShapeDtypeStruct((N, D), F32), jax.ShapeDtypeStruct((N, D), BF16)],
        compiler_params=_cparams(("arbitrary",)),
        name="merge",
    )(x2, gaa2, gb2, on2, wao_bf, wo_bf, g2)


def _staircase_pairs(topk):
    return [(a, b) for a in range(topk) for b in range(topk) if (a + 1) * (b + 1) <= topk]


def _peer_select_kernel(xn2_ref, wqt_ref, qg_ref, bd_ref, r1_ref, bs_ref, c0_ref, as_ref,
                        s_ref, work_ref, rank_ref, top_ref, *, nh, nk, topk):
    tm = xn2_ref.shape[0]
    dk = nk
    qp = _dot_nt(wqt_ref[...], xn2_ref[...])
    qg = qg_ref[...]

    for h in range(nh):
        lo = qp[h * dk:(h + 1) * dk, :]
        hi = qp[(nh + h) * dk:(nh + h + 1) * dk, :]
        ss = jnp.sum(lo * lo + hi * hi, axis=0, keepdims=True)
        r = lax.rsqrt(ss * (1.0 / (2 * dk)) + EPS)
        for c, part in ((0, lo), (1, hi)):
            rows = slice((c * nh + h) * dk, (c * nh + h + 1) * dk)
            s_ref[c, pl.ds(h * dk, dk), :] = (part * r) * qg[rows, :]

    for c in range(2):
        qn = s_ref[c].astype(BF16)
        s_ref[c] = _dot(bd_ref[c], qn)

    kio = lax.broadcasted_iota(jnp.int32, (nk, nh, tm), 0)
    neg_inf = jnp.float32(-jnp.inf)

    for c in range(2):
        work_ref[...] = s_ref[c].reshape(nk, nh, tm)
        rank_ref[c] = jnp.full((nk, nh, tm), float(topk), F32)

        def round_body(r, _):
            w = work_ref[...]
            m = jnp.max(w, axis=0)
            idx = jnp.min(jnp.where(w == m[None], kio, nk), axis=0)
            oh = kio == idx[None]
            work_ref[...] = jnp.where(oh, neg_inf, w)
            rank_ref[c] = jnp.where(oh, r.astype(F32), rank_ref[c])
            top_ref[c, pl.ds(r, 1)] = m[None]
            return 0

        lax.fori_loop(0, topk, round_body, 0)

    A = [top_ref[0, a] for a in range(topk)]
    Bv = [top_ref[1, b] for b in range(topk)]

    pairs = _staircase_pairs(topk)
    cand = [A[a] + Bv[b] for (a, b) in pairs]
    n = len(pairs)
    beaten = [jnp.zeros((nh, tm), F32) for _ in range(n)]
    for p in range(n):
        ap, bp = pairs[p]
        for q in range(p + 1, n):
            aq, bq = pairs[q]
            if ap <= aq and bp <= bq:
                beaten[q] = beaten[q] + 1.0
            else:
                t = jnp.where(cand[p] >= cand[q], 1.0, 0.0)
                beaten[q] = beaten[q] + t
                beaten[p] = beaten[p] + (1.0 - t)
    sel = [jnp.where(beaten[p] < float(topk), 1.0, 0.0) for p in range(n)]

    ea = [jnp.exp(A[a] - A[0]) for a in range(topk)]
    eb = [jnp.exp(Bv[b] - Bv[0]) for b in range(topk)]
    z = jnp.zeros((nh, tm), F32)
    cnt = [jnp.zeros((nh, tm), F32) for _ in range(topk)]
    for p, (a, b) in enumerate(pairs):
        z = z + sel[p] * (ea[a] * eb[b])
        cnt[a] = cnt[a] + sel[p]
    inv_z = 1.0 / z

    def emit(out_ref, val):
        val = val.reshape(nk * nh, tm)
        for lb in range(tm // LANES):
            out_ref[lb] = val[:, lb * LANES:(lb + 1) * LANES]

    s0 = s_ref[0].reshape(nk, nh, tm)
    s1 = s_ref[1].reshape(nk, nh, tm)
    emit(as_ref, jnp.exp(s0 - A[0][None]) * inv_z[None])
    emit(bs_ref, jnp.exp(s1 - Bv[0][None]))
    rank0 = rank_ref[0]
    c0 = jnp.zeros((nk, nh, tm), F32)
    for a in range(topk):
        c0 = c0 + jnp.where(rank0 == float(a), cnt[a][None], 0.0)
    emit(c0_ref, c0)
    emit(r1_ref, rank_ref[1])


def _peer_select(xn2, wqt_bf, qg_rows, bd_bf, nh, nk, tm):
    N, D = xn2.shape
    R = nk * nh
    assert tm % LANES == 0
    out = pl.BlockSpec((tm // LANES, R, LANES), lambda t: (t, 0, 0))
    return pl.pallas_call(
        functools.partial(_peer_select_kernel, nh=nh, nk=nk, topk=PEER_TOPK),
        grid=(N // tm,),
        in_specs=[pl.BlockSpec((tm, D), lambda t: (t, 0)),
                  pl.BlockSpec(wqt_bf.shape, lambda t: (0, 0)),
                  pl.BlockSpec(qg_rows.shape, lambda t: (0, 0)),
                  pl.BlockSpec(bd_bf.shape, lambda t: (0, 0, 0))],
        out_specs=[out] * 4,
        out_shape=[jax.ShapeDtypeStruct((N // LANES, R, LANES), F32)] * 4,
        scratch_shapes=[pltpu.VMEM((2, R, tm), F32), pltpu.VMEM((nk, nh, tm), F32),
                        pltpu.VMEM((2, nk, nh, tm), F32), pltpu.VMEM((2, PEER_TOPK, nh, tm), F32)],
        compiler_params=_cparams(("arbitrary",)),
        name="peer_select",
    )(xn2, wqt_bf, qg_rows, bd_bf)


def _gelu_exact(x):
    return 0.5 * x * (1.0 + lax.erf(x * (2.0 ** -0.5)))


def _peer_main_kernel(xn2_ref, x1_ref, u_ref, vt_ref, r1_ref, bs_ref, c0_ref, as_ref, y_ref,
                      xt_ref, acc_ref, *, nh, nk):
    e = pl.program_id(1)
    te = u_ref.shape[0]
    rows_per = te // nk

    @pl.when(e == 0)
    def _():
        xt_ref[...] = xn2_ref[...].astype(F32).T.astype(BF16)
        acc_ref[...] = jnp.zeros_like(acc_ref)

    ht = _dot(u_ref[...], xt_ref[...])
    parts = []
    for ii in range(rows_per):
        base = pl.multiple_of((e * rows_per + ii) * nh, nh)
        cols = []
        for lb in range(ht.shape[1] // LANES):
            crow = c0_ref[lb, pl.ds(base, nh), :]
            arow = as_ref[lb, pl.ds(base, nh), :]
            w = jnp.zeros((nk, LANES), F32)
            for h in range(nh):
                r1 = r1_ref[lb, pl.ds(h, nk, stride=nh), :]
                bs = bs_ref[lb, pl.ds(h, nk, stride=nh), :]
                w = w + jnp.where(r1 < crow[h:h + 1, :], bs, 0.0) * arow[h:h + 1, :]
            hblk = ht[ii * nk:(ii + 1) * nk, lb * LANES:(lb + 1) * LANES]
            cols.append((_gelu_exact(hblk) * w).astype(BF16))
        parts.append(jnp.concatenate(cols, axis=1))
    gt = jnp.concatenate(parts, axis=0)
    acc_ref[...] += _dot(vt_ref[...], gt)

    @pl.when(e == pl.num_programs(1) - 1)
    def _():
        y_ref[...] = x1_ref[...] + acc_ref[...].T


def _peer_main(xn2, x1, u_bf, vt_bf, r1, bs, c0, a_s, nh, nk, tm, te):
    N, D = xn2.shape
    E = u_bf.shape[0]
    R = nk * nh
    tok = pl.BlockSpec((tm, D), lambda t, e: (t, 0))
    tab = pl.BlockSpec((tm // LANES, R, LANES), lambda t, e: (t, 0, 0))
    return pl.pallas_call(
        functools.partial(_peer_main_kernel, nh=nh, nk=nk),
        grid=(N // tm, E // te),
        in_specs=[tok, tok, pl.BlockSpec((te, D), lambda t, e: (e, 0)),
                  pl.BlockSpec((D, te), lambda t, e: (0, e)), tab, tab, tab, tab],
        out_specs=tok,
        out_shape=jax.ShapeDtypeStruct((N, D), F32),
        scratch_shapes=[pltpu.VMEM((D, tm), BF16), pltpu.VMEM((D, tm), F32)],
        compiler_params=_cparams(("arbitrary", "arbitrary")),
        name="peer_main",
    )(xn2, x1, u_bf, vt_bf, r1, bs, c0, a_s)


def _rope_tables(pos, head_dim, width):
    half = head_dim // 2
    inv = ROPE_THETA ** (-jnp.arange(half, dtype=F32) / half)
    ang = pos.astype(F32)[:, None] * inv[None, :]
    cos = jnp.cos(ang)
    sin = jnp.sin(ang)
    reps = width // head_dim
    cos_t = jnp.tile(jnp.concatenate([cos, cos], axis=1), (1, reps))
    sin_t = jnp.tile(jnp.concatenate([-sin, sin], axis=1), (1, reps))
    return cos_t, sin_t


def _token_tile(n, pref):
    t = min(pref, n)
    while n % t:
        t //= 2
    return t


def _trunk(x, pos, conv_state, cache_k2, cache_v2, p):
    B, T, D = x.shape
    nh, hd = p["n_heads"], p["head_dim"]
    tm = _token_tile(T, 256)
    cos_t, sin_t = _rope_tables(pos, hd, 128)
    gaa, gb, new_conv = _branch_a(x, p["g1"], p["w_in"], p["b_gate"], p["conv_w"], conv_state, p["wco"], tm)
    k, v, qb, kb, vb = _qkv(x, p["g1"], p["w_in"], p["qg"], p["kg"], cos_t, sin_t, p["grp"], tm, hd)
    if cache_k2 is None:
        on = _attn_prompt(qb, kb, vb, p["lam4"], p["sg"], nh, tm, p["lam0"])
    else:
        on = _attn_sample(qb, kb, vb, cache_k2, cache_v2, p["lam4"], p["sg"], nh, p["lam0"])
    N = B * T
    tn = _token_tile(N, 256)
    x1, xn2 = _merge(x.reshape(N, D), gaa.reshape(N, D), gb.reshape(N, D), on.reshape(N, D),
                     p["wao"], p["wo"], p["g2"], tn)
    pnh, nk = p["peer_heads"], p["n_keys"]
    qg_rows = jnp.broadcast_to(p["peer_qg_col"], (p["peer_qg_col"].shape[0], tn))
    r1, bs, c0, a_s = _peer_select(xn2, p["wqt"], qg_rows, p["bd"], pnh, nk, tn)
    y = _peer_main(xn2, x1, p["u"], p["vt"], r1, bs, c0, a_s, pnh, nk, tn, 512)
    return y.reshape(B, T, D), k, v, new_conv


def kernel(x_prompt, x_sample, cache_k, cache_v, state_conv, norm1_g, w_in, b_gate, conv_w, q_norm_g, k_norm_g,
           lam_q1, lam_k1, lam_q2, lam_k2, subln_g, w_conv_out, w_attn_out, w_o, norm2_g, peer_wq, peer_q_g,
           peer_subkeys, peer_u, peer_v):
    depth = w_in.shape[0]
    assert depth == 1
    l = LAYER_IDX
    D = x_prompt.shape[-1]
    n_heads, head_dim, v_dim = cache_k.shape[3], cache_k.shape[5], cache_v.shape[4]
    assert v_dim == 2 * head_dim and n_heads * v_dim == D and w_in.shape[2] == 8 * D
    n_keys, dk = peer_subkeys.shape[2], peer_subkeys.shape[3]
    assert n_keys == dk
    peer_heads = peer_wq.shape[2] // (2 * dk)
    past = cache_k.shape[2]
    assert past % CHUNK == 0 and (1 << CHUNK_SHIFT) == CHUNK

    eye_h = jnp.eye(peer_heads, dtype=F32)
    grp = np.kron(np.eye(256 // head_dim), np.ones((head_dim, head_dim))).astype(np.float32)
    p = dict(
        n_heads=n_heads, head_dim=head_dim, peer_heads=peer_heads, n_keys=n_keys, lam0=_lambda_init(l),
        g1=norm1_g[l][None, :], w_in=w_in[l].astype(BF16), b_gate=b_gate[l], conv_w=conv_w[l],
        wco=w_conv_out[l].astype(BF16), wao=w_attn_out[l].astype(BF16), wo=w_o[l].astype(BF16),
        qg=jnp.tile(q_norm_g[l], D // head_dim)[None, :], kg=jnp.tile(k_norm_g[l], D // head_dim)[None, :],
        grp=jnp.asarray(grp, BF16),
        lam4=jnp.stack([lam_q1[l], lam_k1[l], lam_q2[l], lam_k2[l]]),
        sg=subln_g[l][None, :], g2=norm2_g[l][None, :],
        wqt=peer_wq[l].reshape(D, peer_heads, 2, dk).transpose(2, 1, 3, 0).reshape(2 * peer_heads * dk, D).astype(BF16),
        peer_qg_col=jnp.broadcast_to(peer_q_g[l].reshape(2, 1, dk), (2, peer_heads, dk)).reshape(-1, 1),
        bd=jnp.einsum("ckd,hg->ckhgd", peer_subkeys[l], eye_h).reshape(2, n_keys * peer_heads, peer_heads * dk).astype(BF16),
        u=peer_u[l].astype(BF16), vt=peer_v[l].T.astype(BF16),
    )

    B, T, _ = x_prompt.shape
    Bs, Ts, _ = x_sample.shape
    pos_p = jnp.arange(T, dtype=jnp.int32)
    pos_s = past + jnp.arange(Ts, dtype=jnp.int32)
    zero_conv = jnp.zeros((B, 2, D), x_prompt.dtype)
    y_p, k_p, v_p, c_p = _trunk(x_prompt, pos_p, zero_conv, None, None, p)
    y_s, k_s, v_s, c_s = _trunk(x_sample, pos_s, state_conv[l], cache_k[l].reshape(Bs, past, D),
                                cache_v[l].reshape(Bs, past, D), p)
    return (y_p, y_s,
            k_p.reshape(1, B, T, n_heads, 2, head_dim), v_p.reshape(1, B, T, n_heads, v_dim), c_p[None],
            k_s.reshape(1, Bs, Ts, n_heads, 2, head_dim), v_s.reshape(1, Bs, Ts, n_heads, v_dim), c_s[None])
```

```python
import functools
import math

import jax
import jax.numpy as jnp
import numpy as np
from jax import lax
from jax.experimental import pallas as pl
from jax.experimental.pallas import tpu as pltpu

EPS = 1e-6
CHUNK = 64
CHUNK_SHIFT = 6
ROPE_THETA = 10000.0
PEER_TOPK = 16
LAYER_IDX = 0

V7X_VMEM_LIMIT = 56 * 1024 * 1024
LANES = 128

F32 = jnp.float32
BF16 = jnp.bfloat16


def _lambda_init(layer_idx):
    return 0.8 - 0.6 * math.exp(-0.3 * layer_idx)


def _dot(a, b):
    return jnp.dot(a, b, preferred_element_type=F32)


def _dot_nt(a, b):
    return lax.dot_general(a, b, (((1,), (1,)), ((), ())), preferred_element_type=F32)


def _rmsnorm_rows(x, g):
    r = lax.rsqrt(jnp.mean(x * x, axis=-1, keepdims=True) + EPS)
    return (x * r) * g


def _cparams(sem):
    return pltpu.CompilerParams(dimension_semantics=sem, vmem_limit_bytes=V7X_VMEM_LIMIT)


def _branch_a_kernel(x_ref, g1_ref, wb_ref, wc_ref, wh_ref, wga_ref, wgb_ref, bg_ref, cw_ref, cs_ref,
                     wco_ref, gaa_ref, gb_ref, nc_ref, carry_ref):
    t = pl.program_id(1)

    @pl.when(t == 0)
    def _():
        carry_ref[...] = cs_ref[...]

    x = x_ref[...]
    xn = _rmsnorm_rows(x, g1_ref[...]).astype(BF16)
    hb = _dot(xn, wb_ref[...])
    hc = _dot(xn, wc_ref[...])
    hh = _dot(xn, wh_ref[...])
    u = hc * hh
    tm = u.shape[0]
    row = lax.broadcasted_iota(jnp.int32, u.shape, 0)
    c0 = carry_ref[0:1, :]
    c1 = carry_ref[1:2, :]
    u1 = jnp.where(row == 0, c1, pltpu.roll(u, 1, 0))
    u2 = jnp.where(row == 0, c0, jnp.where(row == 1, c1, pltpu.roll(u, 2, 0)))
    y = cw_ref[0:1, :] * u2 + cw_ref[1:2, :] * u1 + cw_ref[2:3, :] * u
    conv_y = hb * y
    new_c = u[tm - 2:tm, :]
    carry_ref[...] = new_c
    nc_ref[...] = new_c
    a = _dot(conv_y.astype(BF16), wco_ref[...])
    ga = jax.nn.sigmoid(_dot(xn, wga_ref[...]) + bg_ref[0:1, :])
    gb = jax.nn.sigmoid(_dot(xn, wgb_ref[...]) + bg_ref[1:2, :])
    gaa_ref[...] = ga * a
    gb_ref[...] = gb


def _branch_a(x, g1, w_in_bf, b_gate, conv_w, conv_state, wco_bf, tm):
    B, T, D = x.shape
    nt = T // tm
    wspec = lambda j: pl.BlockSpec((D, D), lambda b, t, j=j: (0, j))
    tile = pl.BlockSpec((None, tm, D), lambda b, t: (b, t, 0))
    full2 = lambda r: pl.BlockSpec((r, D), lambda b, t: (0, 0))
    return pl.pallas_call(
        _branch_a_kernel,
        grid=(B, nt),
        in_specs=[tile, full2(1), wspec(0), wspec(1), wspec(2), wspec(6), wspec(7), full2(2), full2(3),
                  pl.BlockSpec((None, 2, D), lambda b, t: (b, 0, 0)), pl.BlockSpec((D, D), lambda b, t: (0, 0))],
        out_specs=[tile, tile, pl.BlockSpec((None, 2, D), lambda b, t: (b, 0, 0))],
        out_shape=[jax.ShapeDtypeStruct((B, T, D), F32), jax.ShapeDtypeStruct((B, T, D), F32),
                   jax.ShapeDtypeStruct((B, 2, D), F32)],
        scratch_shapes=[pltpu.VMEM((2, D), F32)],
        compiler_params=_cparams(("arbitrary", "arbitrary")),
        name="branch_a",
    )(x, g1, w_in_bf, w_in_bf, w_in_bf, w_in_bf, w_in_bf, b_gate, conv_w, conv_state, wco_bf)


def _qkv_kernel(x_ref, g1_ref, wq_ref, wk_ref, wv_ref, qg_ref, kg_ref, cos_ref, sin_ref, grp_ref,
                k_ref, v_ref, qb_ref, kb_ref, vb_ref, *, head_dim):
    x = x_ref[...]
    xn = _rmsnorm_rows(x, g1_ref[...]).astype(BF16)
    q = _dot(xn, wq_ref[...])
    k = _dot(xn, wk_ref[...])
    v = _dot(xn, wv_ref[...])
    D = q.shape[1]
    half = head_dim // 2
    reps = D // cos_ref.shape[1]
    cos = jnp.tile(cos_ref[...], (1, reps))
    sin = jnp.tile(sin_ref[...], (1, reps))
    lane = lax.broadcasted_iota(jnp.int32, q.shape, 1)
    first_half = (lane & half) == 0
    grp = grp_ref[...]
    gw = grp.shape[0]

    def headnorm_rope(z, g):
        z2 = z * z
        hi = z2.astype(BF16)
        lo = (z2 - hi.astype(F32)).astype(BF16)
        parts = []
        for s in range(D // gw):
            sl = slice(s * gw, (s + 1) * gw)
            parts.append(_dot(hi[:, sl], grp) + _dot(lo[:, sl], grp))
        ss = jnp.concatenate(parts, axis=1)
        r = lax.rsqrt(ss * (1.0 / head_dim) + EPS)
        zn = (z * r) * g
        sw = jnp.where(first_half, pltpu.roll(zn, D - half, 1), pltpu.roll(zn, half, 1))
        return zn * cos + sw * sin

    qr = headnorm_rope(q, qg_ref[...])
    kr = headnorm_rope(k, kg_ref[...])
    k_ref[...] = kr
    v_ref[...] = v
    qb_ref[...] = (qr * (head_dim ** -0.5)).astype(BF16)
    kb_ref[...] = kr.astype(BF16)
    vb_ref[...] = v.astype(BF16)


def _qkv(x, g1, w_in_bf, qg_row, kg_row, cos_t, sin_t, grp, tm, head_dim):
    B, T, D = x.shape
    nt = T // tm
    wspec = lambda j: pl.BlockSpec((D, D), lambda b, t, j=j: (0, j))
    tile = pl.BlockSpec((None, tm, D), lambda b, t: (b, t, 0))
    row = pl.BlockSpec((1, D), lambda b, t: (0, 0))
    tab = pl.BlockSpec((tm, cos_t.shape[1]), lambda b, t: (t, 0))
    return pl.pallas_call(
        functools.partial(_qkv_kernel, head_dim=head_dim),
        grid=(B, nt),
        in_specs=[tile, row, wspec(3), wspec(4), wspec(5), row, row, tab, tab,
                  pl.BlockSpec(grp.shape, lambda b, t: (0, 0))],
        out_specs=[tile] * 5,
        out_shape=[jax.ShapeDtypeStruct((B, T, D), F32)] * 2 + [jax.ShapeDtypeStruct((B, T, D), BF16)] * 3,
        compiler_params=_cparams(("arbitrary", "arbitrary")),
        name="qkv",
    )(x, g1, w_in_bf, w_in_bf, w_in_bf, qg_row, kg_row, cos_t, sin_t, grp)


def _stack_halves(q, head_dim):
    lane = lax.broadcasted_iota(jnp.int32, q.shape, 1)
    zero = jnp.zeros_like(q)
    return jnp.concatenate([jnp.where(lane < head_dim, q, zero), jnp.where(lane >= head_dim, q, zero)], axis=0)


def _lambda_value(lam_ref, lam0):
    l = lam_ref[...]
    s1 = jnp.sum(l[0:1, :] * l[1:2, :], axis=-1, keepdims=True)
    s2 = jnp.sum(l[2:3, :] * l[3:4, :], axis=-1, keepdims=True)
    return jnp.exp(s1) - jnp.exp(s2) + lam0


def _attn_finish(acc, l, tq, lam, sg, lam0):
    o = acc[:tq] / l[:tq] - lam * (acc[tq:] / l[tq:])
    return _rmsnorm_rows(o, sg) * (1.0 - lam0)


def _attn_prompt_kernel(q_ref, k_ref, v_ref, lam_ref, sg_ref, o_ref, *, tq, head_dim, lam0):
    qi = pl.program_id(2)
    qs = _stack_halves(q_ref[...], head_dim)

    def block(j, carry, masked):
        m, l, acc = carry
        start = pl.multiple_of(j * tq, tq)
        kb = k_ref[pl.ds(start, tq), :]
        vb = v_ref[pl.ds(start, tq), :]
        s = _dot_nt(qs, kb)
        if masked:
            r = lax.broadcasted_iota(jnp.int32, s.shape, 0)
            c = lax.broadcasted_iota(jnp.int32, s.shape, 1)
            rr = jnp.where(r >= tq, r - tq, r)
            vis = (c >> CHUNK_SHIFT) <= (rr >> CHUNK_SHIFT)
            s = jnp.where(vis, s, -1e30)
        m_new = jnp.maximum(m, jnp.max(s, axis=-1, keepdims=True))
        alpha = jnp.exp(m - m_new)
        p = jnp.exp(s - m_new)
        l = alpha * l + jnp.sum(p, axis=-1, keepdims=True)
        acc = alpha * acc + _dot(p.astype(BF16), vb)
        return m_new, l, acc

    init = (jnp.full((2 * tq, 1), -1e30, F32), jnp.zeros((2 * tq, 1), F32),
            jnp.zeros((2 * tq, v_ref.shape[1]), F32))
    carry = lax.fori_loop(0, qi, lambda j, c: block(j, c, False), init)
    _, l, acc = block(qi, carry, True)
    o_ref[...] = _attn_finish(acc, l, tq, _lambda_value(lam_ref, lam0), sg_ref[...], lam0).astype(o_ref.dtype)


def _attn_prompt(qb, kb, vb, lam4, sg_row, n_heads, tq, lam0):
    B, T, D = qb.shape
    hw = D // n_heads
    head_dim = hw // 2
    assert tq % CHUNK == 0 and T % tq == 0
    return pl.pallas_call(
        functools.partial(_attn_prompt_kernel, tq=tq, head_dim=head_dim, lam0=lam0),
        grid=(B, n_heads, T // tq),
        in_specs=[pl.BlockSpec((None, tq, hw), lambda b, h, i: (b, i, h)),
                  pl.BlockSpec((None, T, hw), lambda b, h, i: (b, 0, h)),
                  pl.BlockSpec((None, T, hw), lambda b, h, i: (b, 0, h)),
                  pl.BlockSpec(lam4.shape, lambda b, h, i: (0, 0)),
                  pl.BlockSpec((1, hw), lambda b, h, i: (0, 0))],
        out_specs=pl.BlockSpec((None, tq, hw), lambda b, h, i: (b, i, h)),
        out_shape=jax.ShapeDtypeStruct((B, T, D), BF16),
        compiler_params=_cparams(("arbitrary", "arbitrary", "arbitrary")),
        name="attn_prompt",
    )(qb, kb, vb, lam4, sg_row)


def _attn_sample_kernel(q_ref, kc_ref, vc_ref, kn_ref, vn_ref, lam_ref, sg_ref, o_ref, *, head_dim, lam0):
    ts = q_ref.shape[0]
    past = kc_ref.shape[0]
    qs = _stack_halves(q_ref[...], head_dim)
    sc = _dot_nt(qs, kc_ref[...].astype(BF16))
    sn = _dot_nt(qs, kn_ref[...])

    def visible(shape, k_off):
        r = lax.broadcasted_iota(jnp.int32, shape, 0)
        c = lax.broadcasted_iota(jnp.int32, shape, 1)
        q_pos = past + jnp.where(r >= ts, r - ts, r)
        return ((c + k_off) >> CHUNK_SHIFT) <= (q_pos >> CHUNK_SHIFT)

    sc = jnp.where(visible(sc.shape, 0), sc, -1e30)
    sn = jnp.where(visible(sn.shape, past), sn, -1e30)
    m = jnp.maximum(jnp.max(sc, axis=-1, keepdims=True), jnp.max(sn, axis=-1, keepdims=True))
    pc = jnp.exp(sc - m)
    pn = jnp.exp(sn - m)
    l = jnp.sum(pc, axis=-1, keepdims=True) + jnp.sum(pn, axis=-1, keepdims=True)
    acc = _dot(pc.astype(BF16), vc_ref[...].astype(BF16)) + _dot(pn.astype(BF16), vn_ref[...])
    o_ref[...] = _attn_finish(acc, l, ts, _lambda_value(lam_ref, lam0), sg_ref[...], lam0).astype(o_ref.dtype)


def _attn_sample(qb, kb, vb, cache_k2, cache_v2, lam4, sg_row, n_heads, lam0):
    B, Ts, D = qb.shape
    P = cache_k2.shape[1]
    hw = D // n_heads
    new = pl.BlockSpec((None, Ts, hw), lambda b, h: (b, 0, h))
    old = pl.BlockSpec((None, P, hw), lambda b, h: (b, 0, h))
    return pl.pallas_call(
        functools.partial(_attn_sample_kernel, head_dim=hw // 2, lam0=lam0),
        grid=(B, n_heads),
        in_specs=[new, old, old, new, new, pl.BlockSpec(lam4.shape, lambda b, h: (0, 0)),
                  pl.BlockSpec((1, hw), lambda b, h: (0, 0))],
        out_specs=new,
        out_shape=jax.ShapeDtypeStruct((B, Ts, D), BF16),
        compiler_params=_cparams(("arbitrary", "arbitrary")),
        name="attn_sample",
    )(qb, cache_k2, cache_v2, kb, vb, lam4, sg_row)


def _merge_kernel(x_ref, gaa_ref, gb_ref, on_ref, wao_ref, wo_ref, g2_ref, x1_ref, xn2_ref):
    b = _dot(on_ref[...], wao_ref[...])
    mix = gaa_ref[...] + gb_ref[...] * b
    x1 = x_ref[...] + _dot(mix.astype(BF16), wo_ref[...])
    x1_ref[...] = x1
    xn2_ref[...] = _rmsnorm_rows(x1, g2_ref[...]).astype(BF16)


def _merge(x2, gaa2, gb2, on2, wao_bf, wo_bf, g2, tm):
    N, D = x2.shape
    tile = pl.BlockSpec((tm, D), lambda t: (t, 0))
    wfull = pl.BlockSpec((D, D), lambda t: (0, 0))
    return pl.pallas_call(
        _merge_kernel,
        grid=(N // tm,),
        in_specs=[tile, tile, tile, tile, wfull, wfull, pl.BlockSpec((1, D), lambda t: (0, 0))],
        out_specs=[tile, tile],
        out_shape=[jax.ShapeDtypeStruct((N, D), F32), jax.ShapeDtypeStruct((N, D), BF16)],
        compiler_params=_cparams(("arbitrary",)),
        name="merge",
    )(x2, gaa2, gb2, on2, wao_bf, wo_bf, g2)


def _staircase_pairs(topk):
    return [(a, b) for a in range(topk) for b in range(topk) if (a + 1) * (b + 1) <= topk]


def _peer_select_kernel(xn2_ref, wqt_ref, qg_ref, bd_ref, r1_ref, bs_ref, c0_ref, as_ref,
                        s_ref, work_ref, rank_ref, top_ref, idx_ref, tr_ref, *, nh, nk, topk):
    tm = xn2_ref.shape[0]
    dk = nk
    qp = _dot_nt(wqt_ref[...], xn2_ref[...])
    qg = qg_ref[...]

    for h in range(nh):
        lo = qp[h * dk:(h + 1) * dk, :]
        hi = qp[(nh + h) * dk:(nh + h + 1) * dk, :]
        ss = jnp.sum(lo * lo + hi * hi, axis=0, keepdims=True)
        r = lax.rsqrt(ss * (1.0 / (2 * dk)) + EPS)
        for c, part in ((0, lo), (1, hi)):
            rows = slice((c * nh + h) * dk, (c * nh + h + 1) * dk)
            s_ref[c, pl.ds(h * dk, dk), :] = (part * r) * qg[rows, :]

    for c in range(2):
        qn = s_ref[c].astype(BF16)
        s_ref[c] = _dot(bd_ref[c], qn)

    neg_inf = jnp.float32(-jnp.inf)
    n_chains = 8
    per_chain = nk // n_chains

    for c in range(2):
        work_ref[...] = s_ref[c].reshape(nk, nh, tm)
        if c == 1:
            rank_ref[...] = jnp.full((nk, nh, tm), float(topk), F32)

        def round_body(r, _, c=c):
            chains = []
            for g in range(n_chains):
                k0 = g * per_chain
                m = work_ref[k0]
                idx = jnp.full((nh, tm), k0, jnp.int32)
                for k in range(k0 + 1, k0 + per_chain):
                    w = work_ref[k]
                    gt = w > m
                    m = jnp.where(gt, w, m)
                    idx = jnp.where(gt, k, idx)
                chains.append((m, idx))
            while len(chains) > 1:
                merged = []
                for (ma, ia), (mb, ib) in zip(chains[0::2], chains[1::2]):
                    gt = mb > ma
                    merged.append((jnp.where(gt, mb, ma), jnp.where(gt, ib, ia)))
                chains = merged
            m, idx = chains[0]
            top_ref[c, pl.ds(r, 1)] = m[None]
            if c == 0:
                idx_ref[pl.ds(r, 1)] = idx[None]
            rf = r.astype(F32)
            for k in range(nk):
                oh = idx == k
                work_ref[k] = jnp.where(oh, neg_inf, work_ref[k])
                if c == 1:
                    rank_ref[k] = jnp.where(oh, rf, rank_ref[k])
            return 0

        lax.fori_loop(0, topk, round_body, 0)

    A = [top_ref[0, a] for a in range(topk)]
    Bv = [top_ref[1, b] for b in range(topk)]

    pairs = _staircase_pairs(topk)
    cand = [A[a] + Bv[b] for (a, b) in pairs]
    n = len(pairs)
    beaten = [jnp.zeros((nh, tm), F32) for _ in range(n)]
    for p in range(n):
        ap, bp = pairs[p]
        for q in range(p + 1, n):
            aq, bq = pairs[q]
            if ap <= aq and bp <= bq:
                beaten[q] = beaten[q] + 1.0
            else:
                t = jnp.where(cand[p] >= cand[q], 1.0, 0.0)
                beaten[q] = beaten[q] + t
                beaten[p] = beaten[p] + (1.0 - t)
    sel = [jnp.where(beaten[p] < float(topk), 1.0, 0.0) for p in range(n)]

    ea = [jnp.exp(A[a] - A[0]) for a in range(topk)]
    eb = [jnp.exp(Bv[b] - Bv[0]) for b in range(topk)]
    z = jnp.zeros((nh, tm), F32)
    cnt = [jnp.zeros((nh, tm), F32) for _ in range(topk)]
    for p, (a, b) in enumerate(pairs):
        z = z + sel[p] * (ea[a] * eb[b])
        cnt[a] = cnt[a] + sel[p]
    inv_z = 1.0 / z

    def emit(out_ref, val):
        val = val.reshape(nk * nh, tm)
        for lb in range(tm // LANES):
            out_ref[lb] = val[:, lb * LANES:(lb + 1) * LANES]

    def emit_by_head(out_ref, val):
        emit(tr_ref, val)
        for lb in range(tm // LANES):
            for h in range(nh):
                out_ref[lb, h] = tr_ref[lb, pl.ds(h, nk, stride=nh), :].astype(BF16)

    s0 = s_ref[0].reshape(nk, nh, tm)
    s1 = s_ref[1].reshape(nk, nh, tm)
    emit(as_ref, jnp.exp(s0 - A[0][None]) * (0.5 * inv_z)[None])
    emit_by_head(bs_ref, jnp.exp(s1 - Bv[0][None]))
    kio = lax.broadcasted_iota(jnp.int32, (nk, nh, tm), 0)
    c0 = jnp.zeros((nk, nh, tm), F32)
    for a in range(topk):
        c0 = jnp.where(kio == idx_ref[a][None], cnt[a][None], c0)
    emit(c0_ref, c0)
    emit_by_head(r1_ref, rank_ref[...])


def _peer_select(xn2, wqt_bf, qg_rows, bd_bf, nh, nk, tm):
    N, D = xn2.shape
    R = nk * nh
    assert tm % LANES == 0
    nlb = tm // LANES
    by_row = pl.BlockSpec((nlb, R, LANES), lambda t: (t, 0, 0))
    by_head = pl.BlockSpec((nlb, nh, nk, LANES), lambda t: (t, 0, 0, 0))
    return pl.pallas_call(
        functools.partial(_peer_select_kernel, nh=nh, nk=nk, topk=PEER_TOPK),
        grid=(N // tm,),
        in_specs=[pl.BlockSpec((tm, D), lambda t: (t, 0)),
                  pl.BlockSpec(wqt_bf.shape, lambda t: (0, 0)),
                  pl.BlockSpec(qg_rows.shape, lambda t: (0, 0)),
                  pl.BlockSpec(bd_bf.shape, lambda t: (0, 0, 0))],
        out_specs=[by_head, by_head, by_row, by_row],
        out_shape=[jax.ShapeDtypeStruct((N // LANES, nh, nk, LANES), BF16)] * 2
        + [jax.ShapeDtypeStruct((N // LANES, R, LANES), F32)] * 2,
        scratch_shapes=[pltpu.VMEM((2, R, tm), F32), pltpu.VMEM((nk, nh, tm), F32),
                        pltpu.VMEM((nk, nh, tm), F32), pltpu.VMEM((2, PEER_TOPK, nh, tm), F32),
                        pltpu.VMEM((PEER_TOPK, nh, tm), jnp.int32), pltpu.VMEM((nlb, R, LANES), F32)],
        compiler_params=_cparams(("arbitrary",)),
        name="peer_select",
    )(xn2, wqt_bf, qg_rows, bd_bf)


def _gelu_twice(x):
    return x + x * lax.erf(x * (2.0 ** -0.5))


def _peer_main_kernel(xn2_ref, x1_ref, u_ref, vt_ref, r1_ref, bs_ref, c0_ref, as_ref, y_ref,
                      xt_ref, acc_ref, *, nh, nk, slab):
    e = pl.program_id(1)
    te = u_ref.shape[0]
    tm = xn2_ref.shape[0]
    rows_per_slab = slab // nk

    @pl.when(e == 0)
    def _():
        xt_ref[...] = xn2_ref[...].astype(F32).T.astype(BF16)
        acc_ref[...] = jnp.zeros_like(acc_ref)

    zero = jnp.zeros((nk, LANES), BF16)
    total = None
    for s in range(te // slab):
        ht = _dot(u_ref[s * slab:(s + 1) * slab, :], xt_ref[...])
        gl = _gelu_twice(ht).astype(BF16)
        first_key = e * (te // nk) + s * rows_per_slab
        cols = []
        for lb in range(tm // LANES):
            rows = [(c0_ref[lb, pl.ds(pl.multiple_of((first_key + ii) * nh, nh), nh), :],
                     as_ref[lb, pl.ds(pl.multiple_of((first_key + ii) * nh, nh), nh), :])
                    for ii in range(rows_per_slab)]
            w = [zero] * rows_per_slab
            for h in range(nh):
                r1 = r1_ref[lb, h]
                bs = bs_ref[lb, h]
                for ii, (crow, arow) in enumerate(rows):
                    cb = jnp.broadcast_to(crow[h:h + 1, :], (nk, LANES)).astype(BF16)
                    ab = jnp.broadcast_to(arow[h:h + 1, :], (nk, LANES)).astype(BF16)
                    w[ii] = w[ii] + jnp.where(r1 < cb, bs, zero) * ab
            cols.append(jnp.concatenate(w, axis=0))
        gt = gl * jnp.concatenate(cols, axis=1)
        part = _dot(vt_ref[:, s * slab:(s + 1) * slab], gt)
        total = part if total is None else total + part
    acc_ref[...] += total

    @pl.when(e == pl.num_programs(1) - 1)
    def _():
        y_ref[...] = x1_ref[...] + acc_ref[...].T


def _peer_main(xn2, x1, u_bf, vt_bf, r1, bs, c0, a_s, nh, nk, tm, te):
    N, D = xn2.shape
    E = u_bf.shape[0]
    R = nk * nh
    tok = pl.BlockSpec((tm, D), lambda t, e: (t, 0))
    nlb = tm // LANES
    by_row = pl.BlockSpec((nlb, R, LANES), lambda t, e: (t, 0, 0))
    by_head = pl.BlockSpec((nlb, nh, nk, LANES), lambda t, e: (t, 0, 0, 0))
    return pl.pallas_call(
        functools.partial(_peer_main_kernel, nh=nh, nk=nk, slab=2 * nk),
        grid=(N // tm, E // te),
        in_specs=[tok, tok, pl.BlockSpec((te, D), lambda t, e: (e, 0)),
                  pl.BlockSpec((D, te), lambda t, e: (0, e)), by_head, by_head, by_row, by_row],
        out_specs=tok,
        out_shape=jax.ShapeDtypeStruct((N, D), F32),
        scratch_shapes=[pltpu.VMEM((D, tm), BF16), pltpu.VMEM((D, tm), F32)],
        compiler_params=_cparams(("arbitrary", "arbitrary")),
        name="peer_main",
    )(xn2, x1, u_bf, vt_bf, r1, bs, c0, a_s)


def _rope_tables(pos, head_dim, width):
    half = head_dim // 2
    inv = ROPE_THETA ** (-jnp.arange(half, dtype=F32) / half)
    ang = pos.astype(F32)[:, None] * inv[None, :]
    cos = jnp.cos(ang)
    sin = jnp.sin(ang)
    reps = width // head_dim
    cos_t = jnp.tile(jnp.concatenate([cos, cos], axis=1), (1, reps))
    sin_t = jnp.tile(jnp.concatenate([-sin, sin], axis=1), (1, reps))
    return cos_t, sin_t


def _token_tile(n, pref):
    t = min(pref, n)
    while n % t:
        t //= 2
    return t


def _trunk(x, pos, conv_state, cache_k2, cache_v2, p):
    B, T, D = x.shape
    nh, hd = p["n_heads"], p["head_dim"]
    tm = _token_tile(T, 256)
    cos_t, sin_t = _rope_tables(pos, hd, 128)
    gaa, gb, new_conv = _branch_a(x, p["g1"], p["w_in"], p["b_gate"], p["conv_w"], conv_state, p["wco"], tm)
    k, v, qb, kb, vb = _qkv(x, p["g1"], p["w_in"], p["qg"], p["kg"], cos_t, sin_t, p["grp"], tm, hd)
    if cache_k2 is None:
        on = _attn_prompt(qb, kb, vb, p["lam4"], p["sg"], nh, _token_tile(T, 512), p["lam0"])
    else:
        on = _attn_sample(qb, kb, vb, cache_k2, cache_v2, p["lam4"], p["sg"], nh, p["lam0"])
    N = B * T
    tn = _token_tile(N, 256)
    x1, xn2 = _merge(x.reshape(N, D), gaa.reshape(N, D), gb.reshape(N, D), on.reshape(N, D),
                     p["wao"], p["wo"], p["g2"], tn)
    pnh, nk = p["peer_heads"], p["n_keys"]
    qg_rows = jnp.broadcast_to(p["peer_qg_col"], (p["peer_qg_col"].shape[0], tn))
    r1, bs, c0, a_s = _peer_select(xn2, p["wqt"], qg_rows, p["bd"], pnh, nk, tn)
    y = _peer_main(xn2, x1, p["u"], p["vt"], r1, bs, c0, a_s, pnh, nk, _token_tile(N, 512), 1024)
    return y.reshape(B, T, D), k, v, new_conv


def kernel(x_prompt, x_sample, cache_k, cache_v, state_conv, norm1_g, w_in, b_gate, conv_w, q_norm_g, k_norm_g,
           lam_q1, lam_k1, lam_q2, lam_k2, subln_g, w_conv_out, w_attn_out, w_o, norm2_g, peer_wq, peer_q_g,
           peer_subkeys, peer_u, peer_v):
    depth = w_in.shape[0]
    assert depth == 1
    l = LAYER_IDX
    D = x_prompt.shape[-1]
    n_heads, head_dim, v_dim = cache_k.shape[3], cache_k.shape[5], cache_v.shape[4]
    assert v_dim == 2 * head_dim and n_heads * v_dim == D and w_in.shape[2] == 8 * D
    n_keys, dk = peer_subkeys.shape[2], peer_subkeys.shape[3]
    assert n_keys == dk
    peer_heads = peer_wq.shape[2] // (2 * dk)
    past = cache_k.shape[2]
    assert past % CHUNK == 0 and (1 << CHUNK_SHIFT) == CHUNK

    eye_h = jnp.eye(peer_heads, dtype=F32)
    grp = np.kron(np.eye(256 // head_dim), np.ones((head_dim, head_dim))).astype(np.float32)
    p = dict(
        n_heads=n_heads, head_dim=head_dim, peer_heads=peer_heads, n_keys=n_keys, lam0=_lambda_init(l),
        g1=norm1_g[l][None, :], w_in=w_in[l].astype(BF16), b_gate=b_gate[l], conv_w=conv_w[l],
        wco=w_conv_out[l].astype(BF16), wao=w_attn_out[l].astype(BF16), wo=w_o[l].astype(BF16),
        qg=jnp.tile(q_norm_g[l], D // head_dim)[None, :], kg=jnp.tile(k_norm_g[l], D // head_dim)[None, :],
        grp=jnp.asarray(grp, BF16),
        lam4=jnp.stack([lam_q1[l], lam_k1[l], lam_q2[l], lam_k2[l]]),
        sg=subln_g[l][None, :], g2=norm2_g[l][None, :],
        wqt=peer_wq[l].reshape(D, peer_heads, 2, dk).transpose(2, 1, 3, 0).reshape(2 * peer_heads * dk, D).astype(BF16),
        peer_qg_col=jnp.broadcast_to(peer_q_g[l].reshape(2, 1, dk), (2, peer_heads, dk)).reshape(-1, 1),
        bd=jnp.einsum("ckd,hg->ckhgd", peer_subkeys[l], eye_h).reshape(2, n_keys * peer_heads, peer_heads * dk).astype(BF16),
        u=peer_u[l].astype(BF16), vt=peer_v[l].T.astype(BF16),
    )

    B, T, _ = x_prompt.shape
    Bs, Ts, _ = x_sample.shape
    pos_p = jnp.arange(T, dtype=jnp.int32)
    pos_s = past + jnp.arange(Ts, dtype=jnp.int32)
    zero_conv = jnp.zeros((B, 2, D), x_prompt.dtype)
    y_p, k_p, v_p, c_p = _trunk(x_prompt, pos_p, zero_conv, None, None, p)
    y_s, k_s, v_s, c_s = _trunk(x_sample, pos_s, state_conv[l], cache_k[l].reshape(Bs, past, D),
                                cache_v[l].reshape(Bs, past, D), p)
    return (y_p, y_s,
            k_p.reshape(1, B, T, n_heads, 2, head_dim), v_p.reshape(1, B, T, n_heads, v_dim), c_p[None],
            k_s.reshape(1, Bs, Ts, n_heads, 2, head_dim), v_s.reshape(1, Bs, Ts, n_heads, v_dim), c_s[None])
```

```python
import functools
import math

import jax
import jax.numpy as jnp
import numpy as np
from jax import lax
from jax.experimental import pallas as pl
from jax.experimental.pallas import tpu as pltpu

EPS = 1e-6
CHUNK = 64
CHUNK_SHIFT = 6
ROPE_THETA = 10000.0
PEER_TOPK = 16
LAYER_IDX = 0

V7X_VMEM_LIMIT = 56 * 1024 * 1024
LANES = 128

F32 = jnp.float32
BF16 = jnp.bfloat16


def _lambda_init(layer_idx):
    return 0.8 - 0.6 * math.exp(-0.3 * layer_idx)


def _dot(a, b):
    return jnp.dot(a, b, preferred_element_type=F32)


def _dot_nt(a, b):
    return lax.dot_general(a, b, (((1,), (1,)), ((), ())), preferred_element_type=F32)


def _rmsnorm_rows(x, g):
    r = lax.rsqrt(jnp.mean(x * x, axis=-1, keepdims=True) + EPS)
    return (x * r) * g


def _cparams(sem):
    return pltpu.CompilerParams(dimension_semantics=sem, vmem_limit_bytes=V7X_VMEM_LIMIT)


def _branch_a_kernel(x_ref, g1_ref, wb_ref, wc_ref, wh_ref, wga_ref, wgb_ref, bg_ref, cw_ref, cs_ref,
                     wco_ref, gaa_ref, gb_ref, nc_ref, carry_ref):
    t = pl.program_id(1)

    @pl.when(t == 0)
    def _():
        carry_ref[...] = cs_ref[...]

    x = x_ref[...]
    xn = _rmsnorm_rows(x, g1_ref[...]).astype(BF16)
    hb = _dot(xn, wb_ref[...])
    hc = _dot(xn, wc_ref[...])
    hh = _dot(xn, wh_ref[...])
    u = hc * hh
    tm = u.shape[0]
    row = lax.broadcasted_iota(jnp.int32, u.shape, 0)
    c0 = carry_ref[0:1, :]
    c1 = carry_ref[1:2, :]
    u1 = jnp.where(row == 0, c1, pltpu.roll(u, 1, 0))
    u2 = jnp.where(row == 0, c0, jnp.where(row == 1, c1, pltpu.roll(u, 2, 0)))
    y = cw_ref[0:1, :] * u2 + cw_ref[1:2, :] * u1 + cw_ref[2:3, :] * u
    conv_y = hb * y
    new_c = u[tm - 2:tm, :]
    carry_ref[...] = new_c
    nc_ref[...] = new_c
    a = _dot(conv_y.astype(BF16), wco_ref[...])
    ga = jax.nn.sigmoid(_dot(xn, wga_ref[...]) + bg_ref[0:1, :])
    gb = jax.nn.sigmoid(_dot(xn, wgb_ref[...]) + bg_ref[1:2, :])
    gaa_ref[...] = ga * a
    gb_ref[...] = gb


def _branch_a(x, g1, w_in_bf, b_gate, conv_w, conv_state, wco_bf, tm):
    B, T, D = x.shape
    nt = T // tm
    wspec = lambda j: pl.BlockSpec((D, D), lambda b, t, j=j: (0, j))
    tile = pl.BlockSpec((None, tm, D), lambda b, t: (b, t, 0))
    full2 = lambda r: pl.BlockSpec((r, D), lambda b, t: (0, 0))
    return pl.pallas_call(
        _branch_a_kernel,
        grid=(B, nt),
        in_specs=[tile, full2(1), wspec(0), wspec(1), wspec(2), wspec(6), wspec(7), full2(2), full2(3),
                  pl.BlockSpec((None, 2, D), lambda b, t: (b, 0, 0)), pl.BlockSpec((D, D), lambda b, t: (0, 0))],
        out_specs=[tile, tile, pl.BlockSpec((None, 2, D), lambda b, t: (b, 0, 0))],
        out_shape=[jax.ShapeDtypeStruct((B, T, D), F32), jax.ShapeDtypeStruct((B, T, D), F32),
                   jax.ShapeDtypeStruct((B, 2, D), F32)],
        scratch_shapes=[pltpu.VMEM((2, D), F32)],
        compiler_params=_cparams(("arbitrary", "arbitrary")),
        name="branch_a",
    )(x, g1, w_in_bf, w_in_bf, w_in_bf, w_in_bf, w_in_bf, b_gate, conv_w, conv_state, wco_bf)


def _qkv_kernel(x_ref, g1_ref, wq_ref, wk_ref, wv_ref, qg_ref, kg_ref, cos_ref, sin_ref, grp_ref,
                k_ref, v_ref, qb_ref, kb_ref, vb_ref, *, head_dim):
    x = x_ref[...]
    xn = _rmsnorm_rows(x, g1_ref[...]).astype(BF16)
    q = _dot(xn, wq_ref[...])
    k = _dot(xn, wk_ref[...])
    v = _dot(xn, wv_ref[...])
    D = q.shape[1]
    half = head_dim // 2
    reps = D // cos_ref.shape[1]
    cos = jnp.tile(cos_ref[...], (1, reps))
    sin = jnp.tile(sin_ref[...], (1, reps))
    lane = lax.broadcasted_iota(jnp.int32, q.shape, 1)
    first_half = (lane & half) == 0
    grp = grp_ref[...]
    gw = grp.shape[0]

    def headnorm_rope(z, g):
        z2 = z * z
        hi = z2.astype(BF16)
        lo = (z2 - hi.astype(F32)).astype(BF16)
        parts = []
        for s in range(D // gw):
            sl = slice(s * gw, (s + 1) * gw)
            parts.append(_dot(hi[:, sl], grp) + _dot(lo[:, sl], grp))
        ss = jnp.concatenate(parts, axis=1)
        r = lax.rsqrt(ss * (1.0 / head_dim) + EPS)
        zn = (z * r) * g
        sw = jnp.where(first_half, pltpu.roll(zn, D - half, 1), pltpu.roll(zn, half, 1))
        return zn * cos + sw * sin

    qr = headnorm_rope(q, qg_ref[...])
    kr = headnorm_rope(k, kg_ref[...])
    k_ref[...] = kr
    v_ref[...] = v
    qb_ref[...] = (qr * (head_dim ** -0.5)).astype(BF16)
    kb_ref[...] = kr.astype(BF16)
    vb_ref[...] = v.astype(BF16)


def _qkv(x, g1, w_in_bf, qg_row, kg_row, cos_t, sin_t, grp, tm, head_dim):
    B, T, D = x.shape
    nt = T // tm
    wspec = lambda j: pl.BlockSpec((D, D), lambda b, t, j=j: (0, j))
    tile = pl.BlockSpec((None, tm, D), lambda b, t: (b, t, 0))
    row = pl.BlockSpec((1, D), lambda b, t: (0, 0))
    tab = pl.BlockSpec((tm, cos_t.shape[1]), lambda b, t: (t, 0))
    return pl.pallas_call(
        functools.partial(_qkv_kernel, head_dim=head_dim),
        grid=(B, nt),
        in_specs=[tile, row, wspec(3), wspec(4), wspec(5), row, row, tab, tab,
                  pl.BlockSpec(grp.shape, lambda b, t: (0, 0))],
        out_specs=[tile] * 5,
        out_shape=[jax.ShapeDtypeStruct((B, T, D), F32)] * 2 + [jax.ShapeDtypeStruct((B, T, D), BF16)] * 3,
        compiler_params=_cparams(("arbitrary", "arbitrary")),
        name="qkv",
    )(x, g1, w_in_bf, w_in_bf, w_in_bf, qg_row, kg_row, cos_t, sin_t, grp)


def _stack_halves(q, head_dim):
    lane = lax.broadcasted_iota(jnp.int32, q.shape, 1)
    zero = jnp.zeros_like(q)
    return jnp.concatenate([jnp.where(lane < head_dim, q, zero), jnp.where(lane >= head_dim, q, zero)], axis=0)


def _lambda_value(lam_ref, lam0):
    l = lam_ref[...]
    s1 = jnp.sum(l[0:1, :] * l[1:2, :], axis=-1, keepdims=True)
    s2 = jnp.sum(l[2:3, :] * l[3:4, :], axis=-1, keepdims=True)
    return jnp.exp(s1) - jnp.exp(s2) + lam0


def _attn_finish(acc, l, tq, lam, sg, lam0):
    o = acc[:tq] / l[:tq] - lam * (acc[tq:] / l[tq:])
    return _rmsnorm_rows(o, sg) * (1.0 - lam0)


def _attn_prompt_kernel(q_ref, k_ref, v_ref, lam_ref, sg_ref, o_ref, *, tq, head_dim, lam0):
    qi = pl.program_id(2)
    qs = _stack_halves(q_ref[...], head_dim)

    def block(j, carry, masked):
        m, l, acc = carry
        start = pl.multiple_of(j * tq, tq)
        kb = k_ref[pl.ds(start, tq), :]
        vb = v_ref[pl.ds(start, tq), :]
        s = _dot_nt(qs, kb)
        if masked:
            r = lax.broadcasted_iota(jnp.int32, s.shape, 0)
            c = lax.broadcasted_iota(jnp.int32, s.shape, 1)
            rr = jnp.where(r >= tq, r - tq, r)
            vis = (c >> CHUNK_SHIFT) <= (rr >> CHUNK_SHIFT)
            s = jnp.where(vis, s, -1e30)
        m_new = jnp.maximum(m, jnp.max(s, axis=-1, keepdims=True))
        alpha = jnp.exp(m - m_new)
        p = jnp.exp(s - m_new)
        l = alpha * l + jnp.sum(p, axis=-1, keepdims=True)
        acc = alpha * acc + _dot(p.astype(BF16), vb)
        return m_new, l, acc

    init = (jnp.full((2 * tq, 1), -1e30, F32), jnp.zeros((2 * tq, 1), F32),
            jnp.zeros((2 * tq, v_ref.shape[1]), F32))
    carry = lax.fori_loop(0, qi, lambda j, c: block(j, c, False), init)
    _, l, acc = block(qi, carry, True)
    o_ref[...] = _attn_finish(acc, l, tq, _lambda_value(lam_ref, lam0), sg_ref[...], lam0).astype(o_ref.dtype)


def _attn_prompt(qb, kb, vb, lam4, sg_row, n_heads, tq, lam0):
    B, T, D = qb.shape
    hw = D // n_heads
    head_dim = hw // 2
    assert tq % CHUNK == 0 and T % tq == 0
    return pl.pallas_call(
        functools.partial(_attn_prompt_kernel, tq=tq, head_dim=head_dim, lam0=lam0),
        grid=(B, n_heads, T // tq),
        in_specs=[pl.BlockSpec((None, tq, hw), lambda b, h, i: (b, i, h)),
                  pl.BlockSpec((None, T, hw), lambda b, h, i: (b, 0, h)),
                  pl.BlockSpec((None, T, hw), lambda b, h, i: (b, 0, h)),
                  pl.BlockSpec(lam4.shape, lambda b, h, i: (0, 0)),
                  pl.BlockSpec((1, hw), lambda b, h, i: (0, 0))],
        out_specs=pl.BlockSpec((None, tq, hw), lambda b, h, i: (b, i, h)),
        out_shape=jax.ShapeDtypeStruct((B, T, D), BF16),
        compiler_params=_cparams(("arbitrary", "arbitrary", "arbitrary")),
        name="attn_prompt",
    )(qb, kb, vb, lam4, sg_row)


def _attn_sample_kernel(q_ref, kc_ref, vc_ref, kn_ref, vn_ref, lam_ref, sg_ref, o_ref, *, head_dim, lam0):
    ts = q_ref.shape[0]
    past = kc_ref.shape[0]
    qs = _stack_halves(q_ref[...], head_dim)
    sc = _dot_nt(qs, kc_ref[...].astype(BF16))
    sn = _dot_nt(qs, kn_ref[...])

    def visible(shape, k_off):
        r = lax.broadcasted_iota(jnp.int32, shape, 0)
        c = lax.broadcasted_iota(jnp.int32, shape, 1)
        q_pos = past + jnp.where(r >= ts, r - ts, r)
        return ((c + k_off) >> CHUNK_SHIFT) <= (q_pos >> CHUNK_SHIFT)

    sc = jnp.where(visible(sc.shape, 0), sc, -1e30)
    sn = jnp.where(visible(sn.shape, past), sn, -1e30)
    m = jnp.maximum(jnp.max(sc, axis=-1, keepdims=True), jnp.max(sn, axis=-1, keepdims=True))
    pc = jnp.exp(sc - m)
    pn = jnp.exp(sn - m)
    l = jnp.sum(pc, axis=-1, keepdims=True) + jnp.sum(pn, axis=-1, keepdims=True)
    acc = _dot(pc.astype(BF16), vc_ref[...].astype(BF16)) + _dot(pn.astype(BF16), vn_ref[...])
    o_ref[...] = _attn_finish(acc, l, ts, _lambda_value(lam_ref, lam0), sg_ref[...], lam0).astype(o_ref.dtype)


def _attn_sample(qb, kb, vb, cache_k2, cache_v2, lam4, sg_row, n_heads, lam0):
    B, Ts, D = qb.shape
    P = cache_k2.shape[1]
    hw = D // n_heads
    new = pl.BlockSpec((None, Ts, hw), lambda b, h: (b, 0, h))
    old = pl.BlockSpec((None, P, hw), lambda b, h: (b, 0, h))
    return pl.pallas_call(
        functools.partial(_attn_sample_kernel, head_dim=hw // 2, lam0=lam0),
        grid=(B, n_heads),
        in_specs=[new, old, old, new, new, pl.BlockSpec(lam4.shape, lambda b, h: (0, 0)),
                  pl.BlockSpec((1, hw), lambda b, h: (0, 0))],
        out_specs=new,
        out_shape=jax.ShapeDtypeStruct((B, Ts, D), BF16),
        compiler_params=_cparams(("arbitrary", "arbitrary")),
        name="attn_sample",
    )(qb, cache_k2, cache_v2, kb, vb, lam4, sg_row)


def _merge_kernel(x_ref, gaa_ref, gb_ref, on_ref, wao_ref, wo_ref, g2_ref, x1_ref, xn2_ref):
    b = _dot(on_ref[...], wao_ref[...])
    mix = gaa_ref[...] + gb_ref[...] * b
    x1 = x_ref[...] + _dot(mix.astype(BF16), wo_ref[...])
    x1_ref[...] = x1
    xn2_ref[...] = _rmsnorm_rows(x1, g2_ref[...]).astype(BF16)


def _merge(x2, gaa2, gb2, on2, wao_bf, wo_bf, g2, tm):
    N, D = x2.shape
    tile = pl.BlockSpec((tm, D), lambda t: (t, 0))
    wfull = pl.BlockSpec((D, D), lambda t: (0, 0))
    return pl.pallas_call(
        _merge_kernel,
        grid=(N // tm,),
        in_specs=[tile, tile, tile, tile, wfull, wfull, pl.BlockSpec((1, D), lambda t: (0, 0))],
        out_specs=[tile, tile],
        out_shape=[jax.ShapeDtypeStruct((N, D), F32), jax.ShapeDtypeStruct((N, D), BF16)],
        compiler_params=_cparams(("arbitrary",)),
        name="merge",
    )(x2, gaa2, gb2, on2, wao_bf, wo_bf, g2)


def _staircase_pairs(topk):
    return [(a, b) for a in range(topk) for b in range(topk) if (a + 1) * (b + 1) <= topk]


def _peer_select_kernel(xn2_ref, wqt_ref, qg_ref, bd_ref, r1_ref, bs_ref, c0_ref, as_ref,
                        s_ref, work_ref, rank_ref, top_ref, idx_ref, tr_ref, *, nh, nk, topk):
    tm = xn2_ref.shape[0]
    dk = nk
    qp = _dot_nt(wqt_ref[...], xn2_ref[...])
    qg = qg_ref[...]

    for h in range(nh):
        lo = qp[h * dk:(h + 1) * dk, :]
        hi = qp[(nh + h) * dk:(nh + h + 1) * dk, :]
        ss = jnp.sum(lo * lo + hi * hi, axis=0, keepdims=True)
        r = lax.rsqrt(ss * (1.0 / (2 * dk)) + EPS)
        for c, part in ((0, lo), (1, hi)):
            rows = slice((c * nh + h) * dk, (c * nh + h + 1) * dk)
            s_ref[c, pl.ds(h * dk, dk), :] = (part * r) * qg[rows, :]

    for c in range(2):
        qn = s_ref[c].astype(BF16)
        s_ref[c] = _dot(bd_ref[c], qn)

    neg_inf = jnp.float32(-jnp.inf)
    n_chains = 8
    per_chain = nk // n_chains

    for c in range(2):
        work_ref[...] = s_ref[c].reshape(nk, nh, tm)
        if c == 1:
            rank_ref[...] = jnp.full((nk, nh, tm), float(topk), F32)

        def round_body(r, _, c=c):
            chains = []
            for g in range(n_chains):
                k0 = g * per_chain
                m = work_ref[k0]
                idx = jnp.full((nh, tm), k0, jnp.int32)
                for k in range(k0 + 1, k0 + per_chain):
                    w = work_ref[k]
                    gt = w > m
                    m = jnp.where(gt, w, m)
                    idx = jnp.where(gt, k, idx)
                chains.append((m, idx))
            while len(chains) > 1:
                merged = []
                for (ma, ia), (mb, ib) in zip(chains[0::2], chains[1::2]):
                    gt = mb > ma
                    merged.append((jnp.where(gt, mb, ma), jnp.where(gt, ib, ia)))
                chains = merged
            m, idx = chains[0]
            top_ref[c, pl.ds(r, 1)] = m[None]
            if c == 0:
                idx_ref[pl.ds(r, 1)] = idx[None]
            rf = lax.convert_element_type(r, F32)
            for k in range(nk):
                oh = idx == k
                work_ref[k] = jnp.where(oh, neg_inf, work_ref[k])
                if c == 1:
                    rank_ref[k] = jnp.where(oh, rf, rank_ref[k])
            return 0

        lax.fori_loop(0, topk, round_body, 0)

    A = [top_ref[0, a] for a in range(topk)]
    Bv = [top_ref[1, b] for b in range(topk)]

    pairs = _staircase_pairs(topk)
    cand = [A[a] + Bv[b] for (a, b) in pairs]
    n = len(pairs)
    beaten = [jnp.zeros((nh, tm), F32) for _ in range(n)]
    for p in range(n):
        ap, bp = pairs[p]
        for q in range(p + 1, n):
            aq, bq = pairs[q]
            if ap <= aq and bp <= bq:
                beaten[q] = beaten[q] + 1.0
            else:
                t = jnp.where(cand[p] >= cand[q], 1.0, 0.0)
                beaten[q] = beaten[q] + t
                beaten[p] = beaten[p] + (1.0 - t)
    sel = [jnp.where(beaten[p] < float(topk), 1.0, 0.0) for p in range(n)]

    ea = [jnp.exp(A[a] - A[0]) for a in range(topk)]
    eb = [jnp.exp(Bv[b] - Bv[0]) for b in range(topk)]
    z = jnp.zeros((nh, tm), F32)
    cnt = [jnp.zeros((nh, tm), F32) for _ in range(topk)]
    for p, (a, b) in enumerate(pairs):
        z = z + sel[p] * (ea[a] * eb[b])
        cnt[a] = cnt[a] + sel[p]
    inv_z = 1.0 / z

    def emit(out_ref, val):
        val = val.reshape(nk * nh, tm)
        for lb in range(tm // LANES):
            out_ref[lb] = val[:, lb * LANES:(lb + 1) * LANES]

    def emit_by_head(out_ref, val):
        emit(tr_ref, val)
        for lb in range(tm // LANES):
            for h in range(nh):
                out_ref[lb, h] = tr_ref[lb, pl.ds(h, nk, stride=nh), :].astype(BF16)

    s0 = s_ref[0].reshape(nk, nh, tm)
    s1 = s_ref[1].reshape(nk, nh, tm)
    emit(as_ref, jnp.exp(s0 - A[0][None]) * (0.5 * inv_z)[None])
    emit_by_head(bs_ref, jnp.exp(s1 - Bv[0][None]))
    kio = lax.broadcasted_iota(jnp.int32, (nk, nh, tm), 0)
    c0 = jnp.zeros((nk, nh, tm), F32)
    for a in range(topk):
        c0 = jnp.where(kio == idx_ref[a][None], cnt[a][None], c0)
    emit(c0_ref, c0)
    emit_by_head(r1_ref, rank_ref[...])


def _peer_select(xn2, wqt_bf, qg_rows, bd_bf, nh, nk, tm):
    N, D = xn2.shape
    R = nk * nh
    assert tm % LANES == 0
    nlb = tm // LANES
    by_row = pl.BlockSpec((nlb, R, LANES), lambda t: (t, 0, 0))
    by_head = pl.BlockSpec((nlb, nh, nk, LANES), lambda t: (t, 0, 0, 0))
    return pl.pallas_call(
        functools.partial(_peer_select_kernel, nh=nh, nk=nk, topk=PEER_TOPK),
        grid=(N // tm,),
        in_specs=[pl.BlockSpec((tm, D), lambda t: (t, 0)),
                  pl.BlockSpec(wqt_bf.shape, lambda t: (0, 0)),
                  pl.BlockSpec(qg_rows.shape, lambda t: (0, 0)),
                  pl.BlockSpec(bd_bf.shape, lambda t: (0, 0, 0))],
        out_specs=[by_head, by_head, by_row, by_row],
        out_shape=[jax.ShapeDtypeStruct((N // LANES, nh, nk, LANES), BF16)] * 2
        + [jax.ShapeDtypeStruct((N // LANES, R, LANES), F32)] * 2,
        scratch_shapes=[pltpu.VMEM((2, R, tm), F32), pltpu.VMEM((nk, nh, tm), F32),
                        pltpu.VMEM((nk, nh, tm), F32), pltpu.VMEM((2, PEER_TOPK, nh, tm), F32),
                        pltpu.VMEM((PEER_TOPK, nh, tm), jnp.int32), pltpu.VMEM((nlb, R, LANES), F32)],
        compiler_params=_cparams(("arbitrary",)),
        name="peer_select",
    )(xn2, wqt_bf, qg_rows, bd_bf)


def _gelu_twice(x):
    return x + x * lax.erf(x * (2.0 ** -0.5))


BF16_SUBLANE_ROWS = 16


def _bcast_rows_bf16(row, n):
    one = jnp.broadcast_to(row, (BF16_SUBLANE_ROWS, row.shape[1])).astype(BF16)
    return jnp.tile(one, (n // BF16_SUBLANE_ROWS, 1))


def _peer_main_kernel(xn2_ref, x1_ref, u_ref, vt_ref, r1_ref, bs_ref, c0_ref, as_ref, y_ref,
                      xt_ref, acc_ref, gl_a, gl_b, gt_a, gt_b, *, nh, nk, ne, n_blocks):
    s = pl.program_id(0)
    te = u_ref.shape[0]
    tm = xn2_ref.shape[0]
    keys_per_block = te // nk
    e1 = s % ne
    e2 = (s + ne - 1) % ne
    e3 = (s + 2 * ne - 2) % ne

    @pl.when(s == 0)
    def _():
        gl_b[...] = jnp.zeros_like(gl_b)
        gt_a[...] = jnp.zeros_like(gt_a)
        acc_ref[...] = jnp.zeros_like(acc_ref)

    @pl.when((e1 == 0) & (s < n_blocks))
    def _():
        xt_ref[...] = xn2_ref[...].astype(F32).T.astype(BF16)

    @pl.when((e3 == 0) & (s >= 2))
    def _():
        acc_ref[...] = jnp.zeros_like(acc_ref)

    zero = jnp.zeros((nk, LANES), BF16)
    one = jnp.ones((nk, LANES), BF16)

    keys_per_piece = 1
    n_pieces = keys_per_block // keys_per_piece

    def gate_piece(e, piece, lb):
        rows = []
        for ii in range(piece * keys_per_piece, (piece + 1) * keys_per_piece):
            base = pl.multiple_of((e * keys_per_block + ii) * nh, nh)
            rows.append((c0_ref[lb, pl.ds(base, nh), :], as_ref[lb, pl.ds(base, nh), :]))
        w = [zero] * keys_per_piece
        for h in range(nh):
            slab = pl.ds((lb * nh + h) * nk, nk)
            r1 = r1_ref[slab, :]
            bs = bs_ref[slab, :]
            for k, (crow, arow) in enumerate(rows):
                cb = _bcast_rows_bf16(crow[h:h + 1, :], nk)
                ab = _bcast_rows_bf16(arow[h:h + 1, :], nk)
                below = jnp.minimum(jnp.maximum(cb - r1, zero), one)
                w[k] = w[k] + (below * bs) * ab
        return w

    def step(gl_new, gl_old, gt_new, gt_old):
        D = vt_ref.shape[0]
        out_rows = D // keys_per_block
        nlb = tm // LANES

        def stage1(ii):
            rows = slice(ii * nk, (ii + 1) * nk)
            gl_new[rows, :] = _gelu_twice(_dot(u_ref[rows, :], xt_ref[...])).astype(BF16)

        def stage2(piece, lb):
            cols = slice(lb * LANES, (lb + 1) * LANES)
            for k, w in enumerate(gate_piece(e2, piece, lb)):
                ii = piece * keys_per_piece + k
                rows = slice(ii * nk, (ii + 1) * nk)
                gt_new[rows, cols] = gl_old[rows, cols] * w

        def stage3(ii):
            orow = slice(ii * out_rows, (ii + 1) * out_rows)
            acc_ref[orow, :] += _dot(vt_ref[orow, :], gt_old[...])

        mxu_work = []
        for ii in range(keys_per_block):
            mxu_work += [functools.partial(stage1, ii), functools.partial(stage3, ii)]
        vpu_work = [functools.partial(stage2, piece, lb) for piece in range(n_pieces) for lb in range(nlb)]
        order = sorted([((i + 0.5) / len(mxu_work), 0, f) for i, f in enumerate(mxu_work)]
                       + [((i + 0.5) / len(vpu_work), 1, f) for i, f in enumerate(vpu_work)], key=lambda t: t[:2])
        for _, _, piece_fn in order:
            piece_fn()

    @pl.when(s % 2 == 0)
    def _():
        step(gl_a, gl_b, gt_b, gt_a)

    @pl.when(s % 2 == 1)
    def _():
        step(gl_b, gl_a, gt_a, gt_b)

    @pl.when((e3 == ne - 1) & (s >= 2))
    def _():
        y_ref[...] = x1_ref[...] + acc_ref[...].T


def _peer_main(xn2, x1, u_bf, vt_bf, r1, bs, c0, a_s, nh, nk, tm, te):
    N, D = xn2.shape
    E = u_bf.shape[0]
    R = nk * nh
    ne = E // te
    n_blocks = (N // tm) * ne
    last = n_blocks - 1
    nlb = tm // LANES
    tile1 = lambda s: jnp.minimum(s, last) // ne
    tile2 = lambda s: jnp.clip(s - 1, 0, last) // ne
    tile3 = lambda s: jnp.clip(s - 2, 0, last) // ne
    by_row = pl.BlockSpec((nlb, R, LANES), lambda s: (tile2(s), 0, 0))
    by_head = pl.BlockSpec((nlb * nh * nk, LANES), lambda s: (tile2(s), 0))
    r1 = r1.reshape(-1, LANES)
    bs = bs.reshape(-1, LANES)
    return pl.pallas_call(
        functools.partial(_peer_main_kernel, nh=nh, nk=nk, ne=ne, n_blocks=n_blocks),
        grid=(n_blocks + 2,),
        in_specs=[pl.BlockSpec((tm, D), lambda s: (tile1(s), 0)),
                  pl.BlockSpec((tm, D), lambda s: (tile3(s), 0)),
                  pl.BlockSpec((te, D), lambda s: (s % ne, 0)),
                  pl.BlockSpec((D, te), lambda s: (0, (s + 2 * ne - 2) % ne)),
                  by_head, by_head, by_row, by_row],
        out_specs=pl.BlockSpec((tm, D), lambda s: (tile3(s), 0)),
        out_shape=jax.ShapeDtypeStruct((N, D), F32),
        scratch_shapes=[pltpu.VMEM((D, tm), BF16), pltpu.VMEM((D, tm), F32)]
        + [pltpu.VMEM((te, tm), BF16)] * 4,
        compiler_params=_cparams(("arbitrary",)),
        name="peer_main",
    )(xn2, x1, u_bf, vt_bf, r1, bs, c0, a_s)


def _rope_tables(pos, head_dim, width):
    half = head_dim // 2
    inv = ROPE_THETA ** (-jnp.arange(half, dtype=F32) / half)
    ang = pos.astype(F32)[:, None] * inv[None, :]
    cos = jnp.cos(ang)
    sin = jnp.sin(ang)
    reps = width // head_dim
    cos_t = jnp.tile(jnp.concatenate([cos, cos], axis=1), (1, reps))
    sin_t = jnp.tile(jnp.concatenate([-sin, sin], axis=1), (1, reps))
    return cos_t, sin_t


def _token_tile(n, pref):
    t = min(pref, n)
    while n % t:
        t //= 2
    return t


def _trunk(x, pos, conv_state, cache_k2, cache_v2, p):
    B, T, D = x.shape
    nh, hd = p["n_heads"], p["head_dim"]
    tm = _token_tile(T, 256)
    cos_t, sin_t = _rope_tables(pos, hd, 128)
    gaa, gb, new_conv = _branch_a(x, p["g1"], p["w_in"], p["b_gate"], p["conv_w"], conv_state, p["wco"], tm)
    k, v, qb, kb, vb = _qkv(x, p["g1"], p["w_in"], p["qg"], p["kg"], cos_t, sin_t, p["grp"], tm, hd)
    if cache_k2 is None:
        on = _attn_prompt(qb, kb, vb, p["lam4"], p["sg"], nh, _token_tile(T, 512), p["lam0"])
    else:
        on = _attn_sample(qb, kb, vb, cache_k2, cache_v2, p["lam4"], p["sg"], nh, p["lam0"])
    N = B * T
    tn = _token_tile(N, 256)
    x1, xn2 = _merge(x.reshape(N, D), gaa.reshape(N, D), gb.reshape(N, D), on.reshape(N, D),
                     p["wao"], p["wo"], p["g2"], tn)
    pnh, nk = p["peer_heads"], p["n_keys"]
    qg_rows = jnp.broadcast_to(p["peer_qg_col"], (p["peer_qg_col"].shape[0], tn))
    r1, bs, c0, a_s = _peer_select(xn2, p["wqt"], qg_rows, p["bd"], pnh, nk, tn)
    y = _peer_main(xn2, x1, p["u"], p["vt"], r1, bs, c0, a_s, pnh, nk, _token_tile(N, 512), 1024)
    return y.reshape(B, T, D), k, v, new_conv


def kernel(x_prompt, x_sample, cache_k, cache_v, state_conv, norm1_g, w_in, b_gate, conv_w, q_norm_g, k_norm_g,
           lam_q1, lam_k1, lam_q2, lam_k2, subln_g, w_conv_out, w_attn_out, w_o, norm2_g, peer_wq, peer_q_g,
           peer_subkeys, peer_u, peer_v):
    depth = w_in.shape[0]
    assert depth == 1
    l = LAYER_IDX
    D = x_prompt.shape[-1]
    n_heads, head_dim, v_dim = cache_k.shape[3], cache_k.shape[5], cache_v.shape[4]
    assert v_dim == 2 * head_dim and n_heads * v_dim == D and w_in.shape[2] == 8 * D
    n_keys, dk = peer_subkeys.shape[2], peer_subkeys.shape[3]
    assert n_keys == dk
    peer_heads = peer_wq.shape[2] // (2 * dk)
    past = cache_k.shape[2]
    assert past % CHUNK == 0 and (1 << CHUNK_SHIFT) == CHUNK

    eye_h = jnp.eye(peer_heads, dtype=F32)
    grp = np.kron(np.eye(256 // head_dim), np.ones((head_dim, head_dim))).astype(np.float32)
    p = dict(
        n_heads=n_heads, head_dim=head_dim, peer_heads=peer_heads, n_keys=n_keys, lam0=_lambda_init(l),
        g1=norm1_g[l][None, :], w_in=w_in[l].astype(BF16), b_gate=b_gate[l], conv_w=conv_w[l],
        wco=w_conv_out[l].astype(BF16), wao=w_attn_out[l].astype(BF16), wo=w_o[l].astype(BF16),
        qg=jnp.tile(q_norm_g[l], D // head_dim)[None, :], kg=jnp.tile(k_norm_g[l], D // head_dim)[None, :],
        grp=jnp.asarray(grp, BF16),
        lam4=jnp.stack([lam_q1[l], lam_k1[l], lam_q2[l], lam_k2[l]]),
        sg=subln_g[l][None, :], g2=norm2_g[l][None, :],
        wqt=peer_wq[l].reshape(D, peer_heads, 2, dk).transpose(2, 1, 3, 0).reshape(2 * peer_heads * dk, D).astype(BF16),
        peer_qg_col=jnp.broadcast_to(peer_q_g[l].reshape(2, 1, dk), (2, peer_heads, dk)).reshape(-1, 1),
        bd=jnp.einsum("ckd,hg->ckhgd", peer_subkeys[l], eye_h).reshape(2, n_keys * peer_heads, peer_heads * dk).astype(BF16),
        u=peer_u[l].astype(BF16), vt=peer_v[l].T.astype(BF16),
    )

    B, T, _ = x_prompt.shape
    Bs, Ts, _ = x_sample.shape
    pos_p = jnp.arange(T, dtype=jnp.int32)
    pos_s = past + jnp.arange(Ts, dtype=jnp.int32)
    zero_conv = jnp.zeros((B, 2, D), x_prompt.dtype)
    y_p, k_p, v_p, c_p = _trunk(x_prompt, pos_p, zero_conv, None, None, p)
    y_s, k_s, v_s, c_s = _trunk(x_sample, pos_s, state_conv[l], cache_k[l].reshape(Bs, past, D),
                                cache_v[l].reshape(Bs, past, D), p)
    return (y_p, y_s,
            k_p.reshape(1, B, T, n_heads, 2, head_dim), v_p.reshape(1, B, T, n_heads, v_dim), c_p[None],
            k_s.reshape(1, Bs, Ts, n_heads, 2, head_dim), v_s.reshape(1, Bs, Ts, n_heads, v_dim), c_s[None])
```

```python
import functools
import math

import jax
import jax.numpy as jnp
import numpy as np
from jax import lax
from jax.experimental import pallas as pl
from jax.experimental.pallas import tpu as pltpu

EPS = 1e-6
CHUNK = 64
CHUNK_SHIFT = 6
ROPE_THETA = 10000.0
PEER_TOPK = 16
LAYER_IDX = 0

V7X_VMEM_LIMIT = 56 * 1024 * 1024
LANES = 128
PEER_EXPERT_BLOCK = 1024

F32 = jnp.float32
BF16 = jnp.bfloat16


def _lambda_init(layer_idx):
    return 0.8 - 0.6 * math.exp(-0.3 * layer_idx)


def _dot(a, b):
    return jnp.dot(a, b, preferred_element_type=F32)


def _dot_nt(a, b):
    return lax.dot_general(a, b, (((1,), (1,)), ((), ())), preferred_element_type=F32)


def _rmsnorm_rows(x, g):
    r = lax.rsqrt(jnp.mean(x * x, axis=-1, keepdims=True) + EPS)
    return (x * r) * g


def _cparams(sem):
    return pltpu.CompilerParams(dimension_semantics=sem, vmem_limit_bytes=V7X_VMEM_LIMIT)


def _branch_a_kernel(x_ref, g1_ref, wb_ref, wc_ref, wh_ref, wga_ref, wgb_ref, bg_ref, cw_ref, cs_ref,
                     wco_ref, gaa_ref, gb_ref, nc_ref, carry_ref):
    t = pl.program_id(1)

    @pl.when(t == 0)
    def _():
        carry_ref[...] = cs_ref[...]

    x = x_ref[...]
    xn = _rmsnorm_rows(x, g1_ref[...]).astype(BF16)
    hb = _dot(xn, wb_ref[...])
    hc = _dot(xn, wc_ref[...])
    hh = _dot(xn, wh_ref[...])
    u = hc * hh
    tm = u.shape[0]
    row = lax.broadcasted_iota(jnp.int32, u.shape, 0)
    c0 = carry_ref[0:1, :]
    c1 = carry_ref[1:2, :]
    u1 = jnp.where(row == 0, c1, pltpu.roll(u, 1, 0))
    u2 = jnp.where(row == 0, c0, jnp.where(row == 1, c1, pltpu.roll(u, 2, 0)))
    y = cw_ref[0:1, :] * u2 + cw_ref[1:2, :] * u1 + cw_ref[2:3, :] * u
    conv_y = hb * y
    new_c = u[tm - 2:tm, :]
    carry_ref[...] = new_c
    nc_ref[...] = new_c
    a = _dot(conv_y.astype(BF16), wco_ref[...])
    ga = jax.nn.sigmoid(_dot(xn, wga_ref[...]) + bg_ref[0:1, :])
    gb = jax.nn.sigmoid(_dot(xn, wgb_ref[...]) + bg_ref[1:2, :])
    gaa_ref[...] = ga * a
    gb_ref[...] = gb


def _branch_a(x, g1, w_in_bf, b_gate, conv_w, conv_state, wco_bf, tm):
    B, T, D = x.shape
    nt = T // tm
    wspec = lambda j: pl.BlockSpec((D, D), lambda b, t, j=j: (0, j))
    tile = pl.BlockSpec((None, tm, D), lambda b, t: (b, t, 0))
    full2 = lambda r: pl.BlockSpec((r, D), lambda b, t: (0, 0))
    return pl.pallas_call(
        _branch_a_kernel,
        grid=(B, nt),
        in_specs=[tile, full2(1), wspec(0), wspec(1), wspec(2), wspec(6), wspec(7), full2(2), full2(3),
                  pl.BlockSpec((None, 2, D), lambda b, t: (b, 0, 0)), pl.BlockSpec((D, D), lambda b, t: (0, 0))],
        out_specs=[tile, tile, pl.BlockSpec((None, 2, D), lambda b, t: (b, 0, 0))],
        out_shape=[jax.ShapeDtypeStruct((B, T, D), F32), jax.ShapeDtypeStruct((B, T, D), F32),
                   jax.ShapeDtypeStruct((B, 2, D), F32)],
        scratch_shapes=[pltpu.VMEM((2, D), F32)],
        compiler_params=_cparams(("arbitrary", "arbitrary")),
        name="branch_a",
    )(x, g1, w_in_bf, w_in_bf, w_in_bf, w_in_bf, w_in_bf, b_gate, conv_w, conv_state, wco_bf)


def _qkv_kernel(x_ref, g1_ref, wq_ref, wk_ref, wv_ref, qg_ref, kg_ref, cos_ref, sin_ref, grp_ref,
                k_ref, v_ref, qb_ref, kb_ref, vb_ref, *, head_dim):
    x = x_ref[...]
    xn = _rmsnorm_rows(x, g1_ref[...]).astype(BF16)
    q = _dot(xn, wq_ref[...])
    k = _dot(xn, wk_ref[...])
    v = _dot(xn, wv_ref[...])
    D = q.shape[1]
    half = head_dim // 2
    reps = D // cos_ref.shape[1]
    cos = jnp.tile(cos_ref[...], (1, reps))
    sin = jnp.tile(sin_ref[...], (1, reps))
    lane = lax.broadcasted_iota(jnp.int32, q.shape, 1)
    first_half = (lane & half) == 0
    grp = grp_ref[...]
    gw = grp.shape[0]

    def headnorm_rope(z, g):
        z2 = z * z
        hi = z2.astype(BF16)
        lo = (z2 - hi.astype(F32)).astype(BF16)
        parts = []
        for s in range(D // gw):
            sl = slice(s * gw, (s + 1) * gw)
            parts.append(_dot(hi[:, sl], grp) + _dot(lo[:, sl], grp))
        ss = jnp.concatenate(parts, axis=1)
        r = lax.rsqrt(ss * (1.0 / head_dim) + EPS)
        zn = (z * r) * g
        sw = jnp.where(first_half, pltpu.roll(zn, D - half, 1), pltpu.roll(zn, half, 1))
        return zn * cos + sw * sin

    qr = headnorm_rope(q, qg_ref[...])
    kr = headnorm_rope(k, kg_ref[...])
    k_ref[...] = kr
    v_ref[...] = v
    hw = 2 * head_dim
    for dst, val in ((qb_ref, qr * (head_dim ** -0.5)), (kb_ref, kr), (vb_ref, v)):
        val = val.astype(BF16)
        for h in range(D // hw):
            dst[h] = val[:, h * hw:(h + 1) * hw]


def _qkv(x, g1, w_in_bf, qg_row, kg_row, cos_t, sin_t, grp, tm, head_dim):
    B, T, D = x.shape
    nt = T // tm
    hw = 2 * head_dim
    nh = D // hw
    wspec = lambda j: pl.BlockSpec((D, D), lambda b, t, j=j: (0, j))
    tile = pl.BlockSpec((None, tm, D), lambda b, t: (b, t, 0))
    row = pl.BlockSpec((1, D), lambda b, t: (0, 0))
    tab = pl.BlockSpec((tm, cos_t.shape[1]), lambda b, t: (t, 0))
    return pl.pallas_call(
        functools.partial(_qkv_kernel, head_dim=head_dim),
        grid=(B, nt),
        in_specs=[tile, row, wspec(3), wspec(4), wspec(5), row, row, tab, tab,
                  pl.BlockSpec(grp.shape, lambda b, t: (0, 0))],
        out_specs=[tile] * 2 + [pl.BlockSpec((None, nh, tm, hw), lambda b, t: (b, 0, t, 0))] * 3,
        out_shape=[jax.ShapeDtypeStruct((B, T, D), F32)] * 2 + [jax.ShapeDtypeStruct((B, nh, T, hw), BF16)] * 3,
        compiler_params=_cparams(("arbitrary", "arbitrary")),
        name="qkv",
    )(x, g1, w_in_bf, w_in_bf, w_in_bf, qg_row, kg_row, cos_t, sin_t, grp)


def _stack_halves(q, head_dim):
    lane = lax.broadcasted_iota(jnp.int32, q.shape, 1)
    zero = jnp.zeros_like(q)
    return jnp.concatenate([jnp.where(lane < head_dim, q, zero), jnp.where(lane >= head_dim, q, zero)], axis=0)


def _lambda_value(lam_ref, lam0):
    l = lam_ref[...]
    s1 = jnp.sum(l[0:1, :] * l[1:2, :], axis=-1, keepdims=True)
    s2 = jnp.sum(l[2:3, :] * l[3:4, :], axis=-1, keepdims=True)
    return jnp.exp(s1) - jnp.exp(s2) + lam0


def _attn_finish(acc, l, tq, lam, sg, lam0):
    o = acc[:tq] / l[:tq] - lam * (acc[tq:] / l[tq:])
    return _rmsnorm_rows(o, sg) * (1.0 - lam0)


def _attn_prompt_kernel(q_ref, k_ref, v_ref, lam_ref, sg_ref, o_ref, *, tq, head_dim, lam0):
    qi = pl.program_id(2)
    qs = _stack_halves(q_ref[...], head_dim)

    def block(j, carry, masked):
        m, l, acc = carry
        start = pl.multiple_of(j * tq, tq)
        kb = k_ref[pl.ds(start, tq), :]
        vb = v_ref[pl.ds(start, tq), :]
        s = _dot_nt(qs, kb)
        if masked:
            r = lax.broadcasted_iota(jnp.int32, s.shape, 0)
            c = lax.broadcasted_iota(jnp.int32, s.shape, 1)
            rr = jnp.where(r >= tq, r - tq, r)
            vis = (c >> CHUNK_SHIFT) <= (rr >> CHUNK_SHIFT)
            s = jnp.where(vis, s, -1e30)
        m_new = jnp.maximum(m, jnp.max(s, axis=-1, keepdims=True))
        alpha = jnp.exp(m - m_new)
        p = jnp.exp(s - m_new)
        l = alpha * l + jnp.sum(p, axis=-1, keepdims=True)
        acc = alpha * acc + _dot(p.astype(BF16), vb)
        return m_new, l, acc

    init = (jnp.full((2 * tq, 1), -1e30, F32), jnp.zeros((2 * tq, 1), F32),
            jnp.zeros((2 * tq, v_ref.shape[1]), F32))
    carry = lax.fori_loop(0, qi, lambda j, c: block(j, c, False), init)
    _, l, acc = block(qi, carry, True)
    o_ref[...] = _attn_finish(acc, l, tq, _lambda_value(lam_ref, lam0), sg_ref[...], lam0).astype(o_ref.dtype)


def _attn_prompt(qb, kb, vb, lam4, sg_row, n_heads, tq, lam0):
    B, n_heads_, T, hw = qb.shape
    assert n_heads_ == n_heads
    D = n_heads * hw
    head_dim = hw // 2
    assert tq % CHUNK == 0 and T % tq == 0
    return pl.pallas_call(
        functools.partial(_attn_prompt_kernel, tq=tq, head_dim=head_dim, lam0=lam0),
        grid=(B, n_heads, T // tq),
        in_specs=[pl.BlockSpec((None, None, tq, hw), lambda b, h, i: (b, h, i, 0)),
                  pl.BlockSpec((None, None, T, hw), lambda b, h, i: (b, h, 0, 0)),
                  pl.BlockSpec((None, None, T, hw), lambda b, h, i: (b, h, 0, 0)),
                  pl.BlockSpec(lam4.shape, lambda b, h, i: (0, 0)),
                  pl.BlockSpec((1, hw), lambda b, h, i: (0, 0))],
        out_specs=pl.BlockSpec((None, tq, hw), lambda b, h, i: (b, i, h)),
        out_shape=jax.ShapeDtypeStruct((B, T, D), BF16),
        compiler_params=_cparams(("arbitrary", "arbitrary", "arbitrary")),
        name="attn_prompt",
    )(qb, kb, vb, lam4, sg_row)


def _attn_sample_kernel(q_ref, kc_ref, vc_ref, kn_ref, vn_ref, lam_ref, sg_ref, o_ref, *, head_dim, lam0):
    ts = q_ref.shape[0]
    past = kc_ref.shape[0]
    qs = _stack_halves(q_ref[...], head_dim)
    sc = _dot_nt(qs, kc_ref[...].astype(BF16))
    sn = _dot_nt(qs, kn_ref[...])

    def visible(shape, k_off):
        r = lax.broadcasted_iota(jnp.int32, shape, 0)
        c = lax.broadcasted_iota(jnp.int32, shape, 1)
        q_pos = past + jnp.where(r >= ts, r - ts, r)
        return ((c + k_off) >> CHUNK_SHIFT) <= (q_pos >> CHUNK_SHIFT)

    sc = jnp.where(visible(sc.shape, 0), sc, -1e30)
    sn = jnp.where(visible(sn.shape, past), sn, -1e30)
    m = jnp.maximum(jnp.max(sc, axis=-1, keepdims=True), jnp.max(sn, axis=-1, keepdims=True))
    pc = jnp.exp(sc - m)
    pn = jnp.exp(sn - m)
    l = jnp.sum(pc, axis=-1, keepdims=True) + jnp.sum(pn, axis=-1, keepdims=True)
    acc = _dot(pc.astype(BF16), vc_ref[...].astype(BF16)) + _dot(pn.astype(BF16), vn_ref[...])
    o_ref[...] = _attn_finish(acc, l, ts, _lambda_value(lam_ref, lam0), sg_ref[...], lam0).astype(o_ref.dtype)


def _attn_sample(qb, kb, vb, cache_k2, cache_v2, lam4, sg_row, n_heads, lam0):
    B, _, Ts, hw = qb.shape
    D = n_heads * hw
    P = cache_k2.shape[1]
    new = pl.BlockSpec((None, None, Ts, hw), lambda b, h: (b, h, 0, 0))
    old = pl.BlockSpec((None, P, hw), lambda b, h: (b, 0, h))
    return pl.pallas_call(
        functools.partial(_attn_sample_kernel, head_dim=hw // 2, lam0=lam0),
        grid=(B, n_heads),
        in_specs=[new, old, old, new, new, pl.BlockSpec(lam4.shape, lambda b, h: (0, 0)),
                  pl.BlockSpec((1, hw), lambda b, h: (0, 0))],
        out_specs=pl.BlockSpec((None, Ts, hw), lambda b, h: (b, 0, h)),
        out_shape=jax.ShapeDtypeStruct((B, Ts, D), BF16),
        compiler_params=_cparams(("arbitrary", "arbitrary")),
        name="attn_sample",
    )(qb, cache_k2, cache_v2, kb, vb, lam4, sg_row)


def _merge_kernel(x_ref, gaa_ref, gb_ref, on_ref, wao_ref, wo_ref, g2_ref, x1_ref, xn2_ref):
    b = _dot(on_ref[...], wao_ref[...])
    mix = gaa_ref[...] + gb_ref[...] * b
    x1 = x_ref[...] + _dot(mix.astype(BF16), wo_ref[...])
    x1_ref[...] = x1
    xn2_ref[...] = _rmsnorm_rows(x1, g2_ref[...]).astype(BF16)


def _merge(x2, gaa2, gb2, on2, wao_bf, wo_bf, g2, tm):
    N, D = x2.shape
    tile = pl.BlockSpec((tm, D), lambda t: (t, 0))
    wfull = pl.BlockSpec((D, D), lambda t: (0, 0))
    return pl.pallas_call(
        _merge_kernel,
        grid=(N // tm,),
        in_specs=[tile, tile, tile, tile, wfull, wfull, pl.BlockSpec((1, D), lambda t: (0, 0))],
        out_specs=[tile, tile],
        out_shape=[jax.ShapeDtypeStruct((N, D), F32), jax.ShapeDtypeStruct((N, D), BF16)],
        compiler_params=_cparams(("arbitrary",)),
        name="merge",
    )(x2, gaa2, gb2, on2, wao_bf, wo_bf, g2)


def _staircase_pairs(topk):
    return [(a, b) for a in range(topk) for b in range(topk) if (a + 1) * (b + 1) <= topk]


def _peer_select_kernel(xn2_ref, wqt_ref, qg_ref, bd_ref, r1_ref, bs_ref, c0_ref, as_ref,
                        s_ref, work_ref, rank_ref, top_ref, idx_ref, tr_ref, *, nh, nk, topk):
    tm = xn2_ref.shape[0]
    dk = nk
    qp = _dot_nt(wqt_ref[...], xn2_ref[...])
    qg = qg_ref[...]

    for h in range(nh):
        lo = qp[h * dk:(h + 1) * dk, :]
        hi = qp[(nh + h) * dk:(nh + h + 1) * dk, :]
        ss = jnp.sum(lo * lo + hi * hi, axis=0, keepdims=True)
        r = lax.rsqrt(ss * (1.0 / (2 * dk)) + EPS)
        for c, part in ((0, lo), (1, hi)):
            rows = slice((c * nh + h) * dk, (c * nh + h + 1) * dk)
            s_ref[c, pl.ds(h * dk, dk), :] = (part * r) * qg[rows, :]

    for c in range(2):
        qn = s_ref[c].astype(BF16)
        s_ref[c] = _dot(bd_ref[c], qn)

    neg_inf = jnp.float32(-jnp.inf)
    n_chains = 8
    per_chain = nk // n_chains

    for c in range(2):
        work_ref[...] = s_ref[c].reshape(nk, nh, tm)
        if c == 1:
            rank_ref[...] = jnp.full((nk, nh, tm), float(topk), F32)

        def round_body(r, _, c=c):
            chains = []
            for g in range(n_chains):
                k0 = g * per_chain
                m = work_ref[k0]
                idx = jnp.full((nh, tm), k0, jnp.int32)
                for k in range(k0 + 1, k0 + per_chain):
                    w = work_ref[k]
                    gt = w > m
                    m = jnp.where(gt, w, m)
                    idx = jnp.where(gt, k, idx)
                chains.append((m, idx))
            while len(chains) > 1:
                merged = []
                for (ma, ia), (mb, ib) in zip(chains[0::2], chains[1::2]):
                    gt = mb > ma
                    merged.append((jnp.where(gt, mb, ma), jnp.where(gt, ib, ia)))
                chains = merged
            m, idx = chains[0]
            top_ref[c, pl.ds(r, 1)] = m[None]
            if c == 0:
                idx_ref[pl.ds(r, 1)] = idx[None]
            rf = lax.convert_element_type(r, F32)
            for k in range(nk):
                oh = idx == k
                work_ref[k] = jnp.where(oh, neg_inf, work_ref[k])
                if c == 1:
                    rank_ref[k] = jnp.where(oh, rf, rank_ref[k])
            return 0

        lax.fori_loop(0, topk, round_body, 0)

    A = [top_ref[0, a] for a in range(topk)]
    Bv = [top_ref[1, b] for b in range(topk)]

    pairs = _staircase_pairs(topk)
    cand = [A[a] + Bv[b] for (a, b) in pairs]
    n = len(pairs)
    beaten = [jnp.zeros((nh, tm), F32) for _ in range(n)]
    for p in range(n):
        ap, bp = pairs[p]
        for q in range(p + 1, n):
            aq, bq = pairs[q]
            if ap <= aq and bp <= bq:
                beaten[q] = beaten[q] + 1.0
            else:
                t = jnp.where(cand[p] >= cand[q], 1.0, 0.0)
                beaten[q] = beaten[q] + t
                beaten[p] = beaten[p] + (1.0 - t)
    sel = [jnp.where(beaten[p] < float(topk), 1.0, 0.0) for p in range(n)]

    ea = [jnp.exp(A[a] - A[0]) for a in range(topk)]
    eb = [jnp.exp(Bv[b] - Bv[0]) for b in range(topk)]
    z = jnp.zeros((nh, tm), F32)
    cnt = [jnp.zeros((nh, tm), F32) for _ in range(topk)]
    for p, (a, b) in enumerate(pairs):
        z = z + sel[p] * (ea[a] * eb[b])
        cnt[a] = cnt[a] + sel[p]
    inv_z = 1.0 / z

    def emit(out_ref, val):
        val = val.reshape(nk * nh, tm)
        for lb in range(tm // LANES):
            out_ref[lb] = val[:, lb * LANES:(lb + 1) * LANES]

    def emit_by_head(out_ref, val):
        emit(tr_ref, val)
        for lb in range(tm // LANES):
            for h in range(nh):
                out_ref[lb, h] = tr_ref[lb, pl.ds(h, nk, stride=nh), :].astype(BF16)

    s0 = s_ref[0].reshape(nk, nh, tm)
    s1 = s_ref[1].reshape(nk, nh, tm)
    emit(as_ref, jnp.exp(s0 - A[0][None]) * (0.5 * inv_z)[None])
    emit_by_head(bs_ref, jnp.exp(s1 - Bv[0][None]))
    kio = lax.broadcasted_iota(jnp.int32, (nk, nh, tm), 0)
    c0 = jnp.zeros((nk, nh, tm), F32)
    for a in range(topk):
        c0 = jnp.where(kio == idx_ref[a][None], cnt[a][None], c0)
    emit(c0_ref, c0)
    emit_by_head(r1_ref, rank_ref[...])


def _peer_select(xn2, wqt_bf, qg_rows, bd_bf, nh, nk, tm):
    N, D = xn2.shape
    R = nk * nh
    assert tm % LANES == 0
    nlb = tm // LANES
    by_row = pl.BlockSpec((nlb, R, LANES), lambda t: (t, 0, 0))
    by_head = pl.BlockSpec((nlb, nh, nk, LANES), lambda t: (t, 0, 0, 0))
    return pl.pallas_call(
        functools.partial(_peer_select_kernel, nh=nh, nk=nk, topk=PEER_TOPK),
        grid=(N // tm,),
        in_specs=[pl.BlockSpec((tm, D), lambda t: (t, 0)),
                  pl.BlockSpec(wqt_bf.shape, lambda t: (0, 0)),
                  pl.BlockSpec(qg_rows.shape, lambda t: (0, 0)),
                  pl.BlockSpec(bd_bf.shape, lambda t: (0, 0, 0))],
        out_specs=[by_head, by_head, by_row, by_row],
        out_shape=[jax.ShapeDtypeStruct((N // LANES, nh, nk, LANES), BF16)] * 2
        + [jax.ShapeDtypeStruct((N // LANES, R, LANES), F32)] * 2,
        scratch_shapes=[pltpu.VMEM((2, R, tm), F32), pltpu.VMEM((nk, nh, tm), F32),
                        pltpu.VMEM((nk, nh, tm), F32), pltpu.VMEM((2, PEER_TOPK, nh, tm), F32),
                        pltpu.VMEM((PEER_TOPK, nh, tm), jnp.int32), pltpu.VMEM((nlb, R, LANES), F32)],
        compiler_params=_cparams(("arbitrary",)),
        name="peer_select",
    )(xn2, wqt_bf, qg_rows, bd_bf)


def _gelu_twice(x):
    return x + x * lax.erf(x * (2.0 ** -0.5))


BF16_SUBLANE_ROWS = 16


def _bcast_rows_bf16(row, n):
    one = jnp.broadcast_to(row, (BF16_SUBLANE_ROWS, row.shape[1])).astype(BF16)
    return jnp.tile(one, (n // BF16_SUBLANE_ROWS, 1))


def _peer_main_kernel(xn2_ref, x1_ref, u_ref, vt_ref, r1_ref, bs_ref, c0_ref, as_ref, y_ref,
                      xt_ref, acc_ref, gl_a, gl_b, gt_a, gt_b, *, nh, nk, ne, n_blocks):
    s = pl.program_id(0)
    te = u_ref.shape[0]
    tm = xn2_ref.shape[0]
    keys_per_block = te // nk
    e1 = s % ne
    e2 = (s + ne - 1) % ne
    e3 = (s + 2 * ne - 2) % ne

    @pl.when(s == 0)
    def _():
        gl_b[...] = jnp.zeros_like(gl_b)
        gt_a[...] = jnp.zeros_like(gt_a)
        acc_ref[...] = jnp.zeros_like(acc_ref)

    @pl.when((e1 == 0) & (s < n_blocks))
    def _():
        xt_ref[...] = xn2_ref[...].astype(F32).T.astype(BF16)

    @pl.when((e3 == 0) & (s >= 2))
    def _():
        acc_ref[...] = jnp.zeros_like(acc_ref)

    zero = jnp.zeros((nk, LANES), BF16)
    one = jnp.ones((nk, LANES), BF16)

    keys_per_piece = 1
    n_pieces = keys_per_block // keys_per_piece

    def gate_piece(e, piece, lb):
        rows = []
        for ii in range(piece * keys_per_piece, (piece + 1) * keys_per_piece):
            base = pl.multiple_of((e * keys_per_block + ii) * nh, nh)
            rows.append((c0_ref[lb, pl.ds(base, nh), :], as_ref[lb, pl.ds(base, nh), :]))
        w = [zero] * keys_per_piece
        for h in range(nh):
            slab = pl.ds((lb * nh + h) * nk, nk)
            r1 = r1_ref[slab, :]
            bs = bs_ref[slab, :]
            for k, (crow, arow) in enumerate(rows):
                cb = _bcast_rows_bf16(crow[h:h + 1, :], nk)
                ab = _bcast_rows_bf16(arow[h:h + 1, :], nk)
                below = jnp.minimum(jnp.maximum(cb - r1, zero), one)
                w[k] = w[k] + (below * bs) * ab
        return w

    def step(gl_new, gl_old, gt_new, gt_old):
        D = vt_ref.shape[0]
        out_rows = D // keys_per_block
        nlb = tm // LANES

        def stage1(ii):
            rows = slice(ii * nk, (ii + 1) * nk)
            gl_new[rows, :] = _gelu_twice(_dot(u_ref[rows, :], xt_ref[...])).astype(BF16)

        def stage2(piece, lb):
            cols = slice(lb * LANES, (lb + 1) * LANES)
            for k, w in enumerate(gate_piece(e2, piece, lb)):
                ii = piece * keys_per_piece + k
                rows = slice(ii * nk, (ii + 1) * nk)
                gt_new[rows, cols] = gl_old[rows, cols] * w

        def stage3(ii):
            orow = slice(ii * out_rows, (ii + 1) * out_rows)
            acc_ref[orow, :] += _dot(vt_ref[orow, :], gt_old[...])

        mxu_work = []
        for ii in range(keys_per_block):
            mxu_work += [functools.partial(stage1, ii), functools.partial(stage3, ii)]
        vpu_work = [functools.partial(stage2, piece, lb) for piece in range(n_pieces) for lb in range(nlb)]
        order = sorted([((i + 0.5) / len(mxu_work), 0, f) for i, f in enumerate(mxu_work)]
                       + [((i + 0.5) / len(vpu_work), 1, f) for i, f in enumerate(vpu_work)], key=lambda t: t[:2])
        for _, _, piece_fn in order:
            piece_fn()

    @pl.when(s % 2 == 0)
    def _():
        step(gl_a, gl_b, gt_b, gt_a)

    @pl.when(s % 2 == 1)
    def _():
        step(gl_b, gl_a, gt_a, gt_b)

    @pl.when((e3 == ne - 1) & (s >= 2))
    def _():
        y_ref[...] = x1_ref[...] + acc_ref[...].T


def _peer_main(xn2, x1, u_bf, vt_bf, r1, bs, c0, a_s, nh, nk, tm):
    N, D = xn2.shape
    ne, _, te = vt_bf.shape
    R = nk * nh
    n_blocks = (N // tm) * ne
    last = n_blocks - 1
    nlb = tm // LANES
    tile1 = lambda s: jnp.minimum(s, last) // ne
    tile2 = lambda s: jnp.clip(s - 1, 0, last) // ne
    tile3 = lambda s: jnp.clip(s - 2, 0, last) // ne
    by_row = pl.BlockSpec((nlb, R, LANES), lambda s: (tile2(s), 0, 0))
    by_head = pl.BlockSpec((nlb * nh * nk, LANES), lambda s: (tile2(s), 0))
    r1 = r1.reshape(-1, LANES)
    bs = bs.reshape(-1, LANES)
    return pl.pallas_call(
        functools.partial(_peer_main_kernel, nh=nh, nk=nk, ne=ne, n_blocks=n_blocks),
        grid=(n_blocks + 2,),
        in_specs=[pl.BlockSpec((tm, D), lambda s: (tile1(s), 0)),
                  pl.BlockSpec((tm, D), lambda s: (tile3(s), 0)),
                  pl.BlockSpec((te, D), lambda s: (s % ne, 0)),
                  pl.BlockSpec((None, D, te), lambda s: ((s + 2 * ne - 2) % ne, 0, 0)),
                  by_head, by_head, by_row, by_row],
        out_specs=pl.BlockSpec((tm, D), lambda s: (tile3(s), 0)),
        out_shape=jax.ShapeDtypeStruct((N, D), F32),
        scratch_shapes=[pltpu.VMEM((D, tm), BF16), pltpu.VMEM((D, tm), F32)]
        + [pltpu.VMEM((te, tm), BF16)] * 4,
        compiler_params=_cparams(("arbitrary",)),
        name="peer_main",
    )(xn2, x1, u_bf, vt_bf, r1, bs, c0, a_s)


def _rope_tables(pos, head_dim, width):
    half = head_dim // 2
    inv = ROPE_THETA ** (-jnp.arange(half, dtype=F32) / half)
    ang = pos.astype(F32)[:, None] * inv[None, :]
    cos = jnp.cos(ang)
    sin = jnp.sin(ang)
    reps = width // head_dim
    cos_t = jnp.tile(jnp.concatenate([cos, cos], axis=1), (1, reps))
    sin_t = jnp.tile(jnp.concatenate([-sin, sin], axis=1), (1, reps))
    return cos_t, sin_t


def _token_tile(n, pref):
    t = min(pref, n)
    while n % t:
        t //= 2
    return t


def _trunk(x, pos, conv_state, cache_k2, cache_v2, p):
    B, T, D = x.shape
    nh, hd = p["n_heads"], p["head_dim"]
    tm = _token_tile(T, 256)
    cos_t, sin_t = _rope_tables(pos, hd, 128)
    gaa, gb, new_conv = _branch_a(x, p["g1"], p["w_in"], p["b_gate"], p["conv_w"], conv_state, p["wco"], tm)
    k, v, qb, kb, vb = _qkv(x, p["g1"], p["w_in"], p["qg"], p["kg"], cos_t, sin_t, p["grp"], tm, hd)
    if cache_k2 is None:
        on = _attn_prompt(qb, kb, vb, p["lam4"], p["sg"], nh, _token_tile(T, 512), p["lam0"])
    else:
        on = _attn_sample(qb, kb, vb, cache_k2, cache_v2, p["lam4"], p["sg"], nh, p["lam0"])
    N = B * T
    tn = _token_tile(N, 256)
    x1, xn2 = _merge(x.reshape(N, D), gaa.reshape(N, D), gb.reshape(N, D), on.reshape(N, D),
                     p["wao"], p["wo"], p["g2"], tn)
    pnh, nk = p["peer_heads"], p["n_keys"]
    tp = _token_tile(N, 512)
    qg_rows = jnp.broadcast_to(p["peer_qg_col"], (p["peer_qg_col"].shape[0], tp))
    r1, bs, c0, a_s = _peer_select(xn2, p["wqt"], qg_rows, p["bd"], pnh, nk, tp)
    y = _peer_main(xn2, x1, p["u"], p["vt"], r1, bs, c0, a_s, pnh, nk, tp)
    return y.reshape(B, T, D), k, v, new_conv


def kernel(x_prompt, x_sample, cache_k, cache_v, state_conv, norm1_g, w_in, b_gate, conv_w, q_norm_g, k_norm_g,
           lam_q1, lam_k1, lam_q2, lam_k2, subln_g, w_conv_out, w_attn_out, w_o, norm2_g, peer_wq, peer_q_g,
           peer_subkeys, peer_u, peer_v):
    depth = w_in.shape[0]
    assert depth == 1
    l = LAYER_IDX
    D = x_prompt.shape[-1]
    n_heads, head_dim, v_dim = cache_k.shape[3], cache_k.shape[5], cache_v.shape[4]
    assert v_dim == 2 * head_dim and n_heads * v_dim == D and w_in.shape[2] == 8 * D
    n_keys, dk = peer_subkeys.shape[2], peer_subkeys.shape[3]
    assert n_keys == dk
    peer_heads = peer_wq.shape[2] // (2 * dk)
    past = cache_k.shape[2]
    assert past % CHUNK == 0 and (1 << CHUNK_SHIFT) == CHUNK

    eye_h = jnp.eye(peer_heads, dtype=F32)
    grp = np.kron(np.eye(256 // head_dim), np.ones((head_dim, head_dim))).astype(np.float32)
    p = dict(
        n_heads=n_heads, head_dim=head_dim, peer_heads=peer_heads, n_keys=n_keys, lam0=_lambda_init(l),
        g1=norm1_g[l][None, :], w_in=w_in[l].astype(BF16), b_gate=b_gate[l], conv_w=conv_w[l],
        wco=w_conv_out[l].astype(BF16), wao=w_attn_out[l].astype(BF16), wo=w_o[l].astype(BF16),
        qg=jnp.tile(q_norm_g[l], D // head_dim)[None, :], kg=jnp.tile(k_norm_g[l], D // head_dim)[None, :],
        grp=jnp.asarray(grp, BF16),
        lam4=jnp.stack([lam_q1[l], lam_k1[l], lam_q2[l], lam_k2[l]]),
        sg=subln_g[l][None, :], g2=norm2_g[l][None, :],
        wqt=peer_wq[l].reshape(D, peer_heads, 2, dk).transpose(2, 1, 3, 0).reshape(2 * peer_heads * dk, D).astype(BF16),
        peer_qg_col=jnp.broadcast_to(peer_q_g[l].reshape(2, 1, dk), (2, peer_heads, dk)).reshape(-1, 1),
        bd=jnp.einsum("ckd,hg->ckhgd", peer_subkeys[l], eye_h).reshape(2, n_keys * peer_heads, peer_heads * dk).astype(BF16),
        u=peer_u[l].astype(BF16),
        vt=peer_v[l].reshape(-1, PEER_EXPERT_BLOCK, D).transpose(0, 2, 1).astype(BF16),
    )

    B, T, _ = x_prompt.shape
    Bs, Ts, _ = x_sample.shape
    pos_p = jnp.arange(T, dtype=jnp.int32)
    pos_s = past + jnp.arange(Ts, dtype=jnp.int32)
    zero_conv = jnp.zeros((B, 2, D), x_prompt.dtype)
    y_p, k_p, v_p, c_p = _trunk(x_prompt, pos_p, zero_conv, None, None, p)
    y_s, k_s, v_s, c_s = _trunk(x_sample, pos_s, state_conv[l], cache_k[l].reshape(Bs, past, D),
                                cache_v[l].reshape(Bs, past, D), p)
    return (y_p, y_s,
            k_p.reshape(1, B, T, n_heads, 2, head_dim), v_p.reshape(1, B, T, n_heads, v_dim), c_p[None],
            k_s.reshape(1, Bs, Ts, n_heads, 2, head_dim), v_s.reshape(1, Bs, Ts, n_heads, v_dim), c_s[None])
```

```python
import functools
import math

import jax
import jax.numpy as jnp
import numpy as np
from jax import lax
from jax.experimental import pallas as pl
from jax.experimental.pallas import tpu as pltpu

EPS = 1e-6
CHUNK = 64
CHUNK_SHIFT = 6
ROPE_THETA = 10000.0
PEER_TOPK = 16
LAYER_IDX = 0

V7X_VMEM_LIMIT = 56 * 1024 * 1024
LANES = 128
PEER_EXPERT_BLOCK = 1024
MXU_PIECE_ROWS = 256

F32 = jnp.float32
BF16 = jnp.bfloat16


def _lambda_init(layer_idx):
    return 0.8 - 0.6 * math.exp(-0.3 * layer_idx)


def _dot(a, b):
    return jnp.dot(a, b, preferred_element_type=F32)


def _dot_nt(a, b):
    return lax.dot_general(a, b, (((1,), (1,)), ((), ())), preferred_element_type=F32)


def _rmsnorm_rows(x, g):
    r = lax.rsqrt(jnp.mean(x * x, axis=-1, keepdims=True) + EPS)
    return (x * r) * g


def _cparams(sem):
    return pltpu.CompilerParams(dimension_semantics=sem, vmem_limit_bytes=V7X_VMEM_LIMIT)


def _branch_a_kernel(x_ref, g1_ref, wb_ref, wc_ref, wh_ref, wga_ref, wgb_ref, bg_ref, cw_ref, cs_ref,
                     wco_ref, gaa_ref, gb_ref, nc_ref, carry_ref):
    t = pl.program_id(1)

    @pl.when(t == 0)
    def _():
        carry_ref[...] = cs_ref[...]

    x = x_ref[...]
    xn = _rmsnorm_rows(x, g1_ref[...]).astype(BF16)
    hb = _dot(xn, wb_ref[...])
    hc = _dot(xn, wc_ref[...])
    hh = _dot(xn, wh_ref[...])
    u = hc * hh
    tm = u.shape[0]
    row = lax.broadcasted_iota(jnp.int32, u.shape, 0)
    c0 = carry_ref[0:1, :]
    c1 = carry_ref[1:2, :]
    u1 = jnp.where(row == 0, c1, pltpu.roll(u, 1, 0))
    u2 = jnp.where(row == 0, c0, jnp.where(row == 1, c1, pltpu.roll(u, 2, 0)))
    y = cw_ref[0:1, :] * u2 + cw_ref[1:2, :] * u1 + cw_ref[2:3, :] * u
    conv_y = hb * y
    new_c = u[tm - 2:tm, :]
    carry_ref[...] = new_c
    nc_ref[...] = new_c
    a = _dot(conv_y.astype(BF16), wco_ref[...])
    ga = jax.nn.sigmoid(_dot(xn, wga_ref[...]) + bg_ref[0:1, :])
    gb = jax.nn.sigmoid(_dot(xn, wgb_ref[...]) + bg_ref[1:2, :])
    gaa_ref[...] = ga * a
    gb_ref[...] = gb


def _branch_a(x, g1, w_in_bf, b_gate, conv_w, conv_state, wco_bf, tm):
    B, T, D = x.shape
    nt = T // tm
    wspec = lambda j: pl.BlockSpec((D, D), lambda b, t, j=j: (0, j))
    tile = pl.BlockSpec((None, tm, D), lambda b, t: (b, t, 0))
    full2 = lambda r: pl.BlockSpec((r, D), lambda b, t: (0, 0))
    return pl.pallas_call(
        _branch_a_kernel,
        grid=(B, nt),
        in_specs=[tile, full2(1), wspec(0), wspec(1), wspec(2), wspec(6), wspec(7), full2(2), full2(3),
                  pl.BlockSpec((None, 2, D), lambda b, t: (b, 0, 0)), pl.BlockSpec((D, D), lambda b, t: (0, 0))],
        out_specs=[tile, tile, pl.BlockSpec((None, 2, D), lambda b, t: (b, 0, 0))],
        out_shape=[jax.ShapeDtypeStruct((B, T, D), F32), jax.ShapeDtypeStruct((B, T, D), F32),
                   jax.ShapeDtypeStruct((B, 2, D), F32)],
        scratch_shapes=[pltpu.VMEM((2, D), F32)],
        compiler_params=_cparams(("arbitrary", "arbitrary")),
        name="branch_a",
    )(x, g1, w_in_bf, w_in_bf, w_in_bf, w_in_bf, w_in_bf, b_gate, conv_w, conv_state, wco_bf)


def _qkv_kernel(x_ref, g1_ref, wq_ref, wk_ref, wv_ref, qg_ref, kg_ref, cos_ref, sin_ref, grp_ref,
                k_ref, v_ref, qb_ref, kb_ref, vb_ref, *, head_dim):
    x = x_ref[...]
    xn = _rmsnorm_rows(x, g1_ref[...]).astype(BF16)
    q = _dot(xn, wq_ref[...])
    k = _dot(xn, wk_ref[...])
    v = _dot(xn, wv_ref[...])
    D = q.shape[1]
    half = head_dim // 2
    reps = D // cos_ref.shape[1]
    cos = jnp.tile(cos_ref[...], (1, reps))
    sin = jnp.tile(sin_ref[...], (1, reps))
    lane = lax.broadcasted_iota(jnp.int32, q.shape, 1)
    first_half = (lane & half) == 0
    grp = grp_ref[...]
    gw = grp.shape[0]

    def headnorm_rope(z, g):
        z2 = z * z
        hi = z2.astype(BF16)
        lo = (z2 - hi.astype(F32)).astype(BF16)
        parts = []
        for s in range(D // gw):
            sl = slice(s * gw, (s + 1) * gw)
            parts.append(_dot(hi[:, sl], grp) + _dot(lo[:, sl], grp))
        ss = jnp.concatenate(parts, axis=1)
        r = lax.rsqrt(ss * (1.0 / head_dim) + EPS)
        zn = (z * r) * g
        sw = jnp.where(first_half, pltpu.roll(zn, D - half, 1), pltpu.roll(zn, half, 1))
        return zn * cos + sw * sin

    qr = headnorm_rope(q, qg_ref[...])
    kr = headnorm_rope(k, kg_ref[...])
    k_ref[...] = kr
    v_ref[...] = v
    hw = 2 * head_dim
    for dst, val in ((qb_ref, qr * (head_dim ** -0.5)), (kb_ref, kr), (vb_ref, v)):
        val = val.astype(BF16)
        for h in range(D // hw):
            dst[h] = val[:, h * hw:(h + 1) * hw]


def _qkv(x, g1, w_in_bf, qg_row, kg_row, cos_t, sin_t, grp, tm, head_dim):
    B, T, D = x.shape
    nt = T // tm
    hw = 2 * head_dim
    nh = D // hw
    wspec = lambda j: pl.BlockSpec((D, D), lambda b, t, j=j: (0, j))
    tile = pl.BlockSpec((None, tm, D), lambda b, t: (b, t, 0))
    row = pl.BlockSpec((1, D), lambda b, t: (0, 0))
    tab = pl.BlockSpec((tm, cos_t.shape[1]), lambda b, t: (t, 0))
    return pl.pallas_call(
        functools.partial(_qkv_kernel, head_dim=head_dim),
        grid=(B, nt),
        in_specs=[tile, row, wspec(3), wspec(4), wspec(5), row, row, tab, tab,
                  pl.BlockSpec(grp.shape, lambda b, t: (0, 0))],
        out_specs=[tile] * 2 + [pl.BlockSpec((None, nh, tm, hw), lambda b, t: (b, 0, t, 0))] * 3,
        out_shape=[jax.ShapeDtypeStruct((B, T, D), F32)] * 2 + [jax.ShapeDtypeStruct((B, nh, T, hw), BF16)] * 3,
        compiler_params=_cparams(("arbitrary", "arbitrary")),
        name="qkv",
    )(x, g1, w_in_bf, w_in_bf, w_in_bf, qg_row, kg_row, cos_t, sin_t, grp)


def _stack_halves(q, head_dim):
    lane = lax.broadcasted_iota(jnp.int32, q.shape, 1)
    zero = jnp.zeros_like(q)
    return jnp.concatenate([jnp.where(lane < head_dim, q, zero), jnp.where(lane >= head_dim, q, zero)], axis=0)


def _lambda_value(lam_ref, lam0):
    l = lam_ref[...]
    s1 = jnp.sum(l[0:1, :] * l[1:2, :], axis=-1, keepdims=True)
    s2 = jnp.sum(l[2:3, :] * l[3:4, :], axis=-1, keepdims=True)
    return jnp.exp(s1) - jnp.exp(s2) + lam0


def _attn_finish(acc, l, tq, lam, sg, lam0):
    o = acc[:tq] / l[:tq] - lam * (acc[tq:] / l[tq:])
    return _rmsnorm_rows(o, sg) * (1.0 - lam0)


def _attn_prompt_kernel(q_ref, k_ref, v_ref, lam_ref, sg_ref, o_ref, *, tq, head_dim, lam0):
    qi = pl.program_id(2)
    qs = _stack_halves(q_ref[...], head_dim)

    def block(j, carry, masked):
        m, l, acc = carry
        start = pl.multiple_of(j * tq, tq)
        kb = k_ref[pl.ds(start, tq), :]
        vb = v_ref[pl.ds(start, tq), :]
        s = _dot_nt(qs, kb)
        if masked:
            r = lax.broadcasted_iota(jnp.int32, s.shape, 0)
            c = lax.broadcasted_iota(jnp.int32, s.shape, 1)
            rr = jnp.where(r >= tq, r - tq, r)
            vis = (c >> CHUNK_SHIFT) <= (rr >> CHUNK_SHIFT)
            s = jnp.where(vis, s, -1e30)
        m_new = jnp.maximum(m, jnp.max(s, axis=-1, keepdims=True))
        alpha = jnp.exp(m - m_new)
        p = jnp.exp(s - m_new)
        l = alpha * l + jnp.sum(p, axis=-1, keepdims=True)
        acc = alpha * acc + _dot(p.astype(BF16), vb)
        return m_new, l, acc

    init = (jnp.full((2 * tq, 1), -1e30, F32), jnp.zeros((2 * tq, 1), F32),
            jnp.zeros((2 * tq, v_ref.shape[1]), F32))
    carry = lax.fori_loop(0, qi, lambda j, c: block(j, c, False), init)
    _, l, acc = block(qi, carry, True)
    o_ref[...] = _attn_finish(acc, l, tq, _lambda_value(lam_ref, lam0), sg_ref[...], lam0).astype(o_ref.dtype)


def _attn_prompt(qb, kb, vb, lam4, sg_row, n_heads, tq, lam0):
    B, n_heads_, T, hw = qb.shape
    assert n_heads_ == n_heads
    D = n_heads * hw
    head_dim = hw // 2
    assert tq % CHUNK == 0 and T % tq == 0
    return pl.pallas_call(
        functools.partial(_attn_prompt_kernel, tq=tq, head_dim=head_dim, lam0=lam0),
        grid=(B, n_heads, T // tq),
        in_specs=[pl.BlockSpec((None, None, tq, hw), lambda b, h, i: (b, h, i, 0)),
                  pl.BlockSpec((None, None, T, hw), lambda b, h, i: (b, h, 0, 0)),
                  pl.BlockSpec((None, None, T, hw), lambda b, h, i: (b, h, 0, 0)),
                  pl.BlockSpec(lam4.shape, lambda b, h, i: (0, 0)),
                  pl.BlockSpec((1, hw), lambda b, h, i: (0, 0))],
        out_specs=pl.BlockSpec((None, tq, hw), lambda b, h, i: (b, i, h)),
        out_shape=jax.ShapeDtypeStruct((B, T, D), BF16),
        compiler_params=_cparams(("arbitrary", "arbitrary", "arbitrary")),
        name="attn_prompt",
    )(qb, kb, vb, lam4, sg_row)


def _attn_sample_kernel(q_ref, kc_ref, vc_ref, kn_ref, vn_ref, lam_ref, sg_ref, o_ref, *, head_dim, lam0):
    ts = q_ref.shape[0]
    past = kc_ref.shape[0]
    qs = _stack_halves(q_ref[...], head_dim)
    sc = _dot_nt(qs, kc_ref[...].astype(BF16))
    sn = _dot_nt(qs, kn_ref[...])

    def visible(shape, k_off):
        r = lax.broadcasted_iota(jnp.int32, shape, 0)
        c = lax.broadcasted_iota(jnp.int32, shape, 1)
        q_pos = past + jnp.where(r >= ts, r - ts, r)
        return ((c + k_off) >> CHUNK_SHIFT) <= (q_pos >> CHUNK_SHIFT)

    sc = jnp.where(visible(sc.shape, 0), sc, -1e30)
    sn = jnp.where(visible(sn.shape, past), sn, -1e30)
    m = jnp.maximum(jnp.max(sc, axis=-1, keepdims=True), jnp.max(sn, axis=-1, keepdims=True))
    pc = jnp.exp(sc - m)
    pn = jnp.exp(sn - m)
    l = jnp.sum(pc, axis=-1, keepdims=True) + jnp.sum(pn, axis=-1, keepdims=True)
    acc = _dot(pc.astype(BF16), vc_ref[...].astype(BF16)) + _dot(pn.astype(BF16), vn_ref[...])
    o_ref[...] = _attn_finish(acc, l, ts, _lambda_value(lam_ref, lam0), sg_ref[...], lam0).astype(o_ref.dtype)


def _attn_sample(qb, kb, vb, cache_k2, cache_v2, lam4, sg_row, n_heads, lam0):
    B, _, Ts, hw = qb.shape
    D = n_heads * hw
    P = cache_k2.shape[1]
    new = pl.BlockSpec((None, None, Ts, hw), lambda b, h: (b, h, 0, 0))
    old = pl.BlockSpec((None, P, hw), lambda b, h: (b, 0, h))
    return pl.pallas_call(
        functools.partial(_attn_sample_kernel, head_dim=hw // 2, lam0=lam0),
        grid=(B, n_heads),
        in_specs=[new, old, old, new, new, pl.BlockSpec(lam4.shape, lambda b, h: (0, 0)),
                  pl.BlockSpec((1, hw), lambda b, h: (0, 0))],
        out_specs=pl.BlockSpec((None, Ts, hw), lambda b, h: (b, 0, h)),
        out_shape=jax.ShapeDtypeStruct((B, Ts, D), BF16),
        compiler_params=_cparams(("arbitrary", "arbitrary")),
        name="attn_sample",
    )(qb, cache_k2, cache_v2, kb, vb, lam4, sg_row)


def _merge_kernel(x_ref, gaa_ref, gb_ref, on_ref, wao_ref, wo_ref, g2_ref, x1_ref, xn2_ref):
    b = _dot(on_ref[...], wao_ref[...])
    mix = gaa_ref[...] + gb_ref[...] * b
    x1 = x_ref[...] + _dot(mix.astype(BF16), wo_ref[...])
    x1_ref[...] = x1
    xn2_ref[...] = _rmsnorm_rows(x1, g2_ref[...]).astype(BF16)


def _merge(x2, gaa2, gb2, on2, wao_bf, wo_bf, g2, tm):
    N, D = x2.shape
    tile = pl.BlockSpec((tm, D), lambda t: (t, 0))
    wfull = pl.BlockSpec((D, D), lambda t: (0, 0))
    return pl.pallas_call(
        _merge_kernel,
        grid=(N // tm,),
        in_specs=[tile, tile, tile, tile, wfull, wfull, pl.BlockSpec((1, D), lambda t: (0, 0))],
        out_specs=[tile, tile],
        out_shape=[jax.ShapeDtypeStruct((N, D), F32), jax.ShapeDtypeStruct((N, D), BF16)],
        compiler_params=_cparams(("arbitrary",)),
        name="merge",
    )(x2, gaa2, gb2, on2, wao_bf, wo_bf, g2)


def _staircase_pairs(topk):
    return [(a, b) for a in range(topk) for b in range(topk) if (a + 1) * (b + 1) <= topk]


def _peer_select_kernel(xn2_ref, wqt_ref, qg_ref, bd_ref, r1_ref, bs_ref, c0_ref, as_ref,
                        s_ref, work_ref, rank_ref, top_ref, idx_ref, tr_ref, *, nh, nk, topk):
    tm = xn2_ref.shape[0]
    dk = nk
    qp = _dot_nt(wqt_ref[...], xn2_ref[...])
    qg = qg_ref[...]

    for h in range(nh):
        lo = qp[h * dk:(h + 1) * dk, :]
        hi = qp[(nh + h) * dk:(nh + h + 1) * dk, :]
        ss = jnp.sum(lo * lo + hi * hi, axis=0, keepdims=True)
        r = lax.rsqrt(ss * (1.0 / (2 * dk)) + EPS)
        for c, part in ((0, lo), (1, hi)):
            rows = slice((c * nh + h) * dk, (c * nh + h + 1) * dk)
            s_ref[c, pl.ds(h * dk, dk), :] = (part * r) * qg[rows, :]

    for c in range(2):
        qn = s_ref[c].astype(BF16)
        s_ref[c] = _dot(bd_ref[c], qn)

    neg_inf = jnp.float32(-jnp.inf)
    n_chains = 8
    per_chain = nk // n_chains

    for c in range(2):
        work_ref[...] = s_ref[c].reshape(nk, nh, tm)
        if c == 1:
            rank_ref[...] = jnp.full((nk, nh, tm), float(topk), F32)

        def round_body(r, _, c=c):
            chains = []
            for g in range(n_chains):
                k0 = g * per_chain
                m = work_ref[k0]
                idx = jnp.full((nh, tm), k0, jnp.int32)
                for k in range(k0 + 1, k0 + per_chain):
                    w = work_ref[k]
                    gt = w > m
                    m = jnp.where(gt, w, m)
                    idx = jnp.where(gt, k, idx)
                chains.append((m, idx))
            while len(chains) > 1:
                merged = []
                for (ma, ia), (mb, ib) in zip(chains[0::2], chains[1::2]):
                    gt = mb > ma
                    merged.append((jnp.where(gt, mb, ma), jnp.where(gt, ib, ia)))
                chains = merged
            m, idx = chains[0]
            top_ref[c, pl.ds(r, 1)] = m[None]
            if c == 0:
                idx_ref[pl.ds(r, 1)] = idx[None]
            rf = lax.convert_element_type(r, F32)
            for k in range(nk):
                oh = idx == k
                work_ref[k] = jnp.where(oh, neg_inf, work_ref[k])
                if c == 1:
                    rank_ref[k] = jnp.where(oh, rf, rank_ref[k])
            return 0

        lax.fori_loop(0, topk, round_body, 0)

    A = [top_ref[0, a] for a in range(topk)]
    Bv = [top_ref[1, b] for b in range(topk)]

    pairs = _staircase_pairs(topk)
    cand = [A[a] + Bv[b] for (a, b) in pairs]
    n = len(pairs)
    beaten = [jnp.zeros((nh, tm), F32) for _ in range(n)]
    for p in range(n):
        ap, bp = pairs[p]
        for q in range(p + 1, n):
            aq, bq = pairs[q]
            if ap <= aq and bp <= bq:
                beaten[q] = beaten[q] + 1.0
            else:
                t = jnp.where(cand[p] >= cand[q], 1.0, 0.0)
                beaten[q] = beaten[q] + t
                beaten[p] = beaten[p] + (1.0 - t)
    sel = [jnp.where(beaten[p] < float(topk), 1.0, 0.0) for p in range(n)]

    ea = [jnp.exp(A[a] - A[0]) for a in range(topk)]
    eb = [jnp.exp(Bv[b] - Bv[0]) for b in range(topk)]
    z = jnp.zeros((nh, tm), F32)
    cnt = [jnp.zeros((nh, tm), F32) for _ in range(topk)]
    for p, (a, b) in enumerate(pairs):
        z = z + sel[p] * (ea[a] * eb[b])
        cnt[a] = cnt[a] + sel[p]
    inv_z = 1.0 / z

    def emit(out_ref, val):
        val = val.reshape(nk * nh, tm)
        for lb in range(tm // LANES):
            out_ref[lb] = val[:, lb * LANES:(lb + 1) * LANES]

    def emit_by_head(out_ref, val):
        emit(tr_ref, val)
        for lb in range(tm // LANES):
            for h in range(nh):
                out_ref[lb, h] = tr_ref[lb, pl.ds(h, nk, stride=nh), :].astype(BF16)

    s0 = s_ref[0].reshape(nk, nh, tm)
    s1 = s_ref[1].reshape(nk, nh, tm)
    emit(as_ref, jnp.exp(s0 - A[0][None]) * (0.5 * inv_z)[None])
    emit_by_head(bs_ref, jnp.exp(s1 - Bv[0][None]))
    kio = lax.broadcasted_iota(jnp.int32, (nk, nh, tm), 0)
    c0 = jnp.zeros((nk, nh, tm), F32)
    for a in range(topk):
        c0 = jnp.where(kio == idx_ref[a][None], cnt[a][None], c0)
    emit(c0_ref, c0)
    emit_by_head(r1_ref, rank_ref[...])


def _peer_select(xn2, wqt_bf, qg_rows, bd_bf, nh, nk, tm):
    N, D = xn2.shape
    R = nk * nh
    assert tm % LANES == 0
    nlb = tm // LANES
    by_row = pl.BlockSpec((nlb, R, LANES), lambda t: (t, 0, 0))
    by_head = pl.BlockSpec((nlb, nh, nk, LANES), lambda t: (t, 0, 0, 0))
    return pl.pallas_call(
        functools.partial(_peer_select_kernel, nh=nh, nk=nk, topk=PEER_TOPK),
        grid=(N // tm,),
        in_specs=[pl.BlockSpec((tm, D), lambda t: (t, 0)),
                  pl.BlockSpec(wqt_bf.shape, lambda t: (0, 0)),
                  pl.BlockSpec(qg_rows.shape, lambda t: (0, 0)),
                  pl.BlockSpec(bd_bf.shape, lambda t: (0, 0, 0))],
        out_specs=[by_head, by_head, by_row, by_row],
        out_shape=[jax.ShapeDtypeStruct((N // LANES, nh, nk, LANES), BF16)] * 2
        + [jax.ShapeDtypeStruct((N // LANES, R, LANES), F32)] * 2,
        scratch_shapes=[pltpu.VMEM((2, R, tm), F32), pltpu.VMEM((nk, nh, tm), F32),
                        pltpu.VMEM((nk, nh, tm), F32), pltpu.VMEM((2, PEER_TOPK, nh, tm), F32),
                        pltpu.VMEM((PEER_TOPK, nh, tm), jnp.int32), pltpu.VMEM((nlb, R, LANES), F32)],
        compiler_params=_cparams(("arbitrary",)),
        name="peer_select",
    )(xn2, wqt_bf, qg_rows, bd_bf)


def _gelu_twice(x):
    return x + x * lax.erf(x * (2.0 ** -0.5))


BF16_SUBLANE_ROWS = 16


def _bcast_rows_bf16(row, n):
    one = jnp.broadcast_to(row, (BF16_SUBLANE_ROWS, row.shape[1])).astype(BF16)
    return jnp.tile(one, (n // BF16_SUBLANE_ROWS, 1))


def _peer_main_kernel(xn2_ref, x1_ref, u_ref, vt_ref, r1_ref, bs_ref, c0_ref, as_ref, y_ref,
                      xt_ref, acc_ref, gl_a, gl_b, gt_a, gt_b, *, nh, nk, ne, n_blocks):
    s = pl.program_id(0)
    te = u_ref.shape[0]
    tm = xn2_ref.shape[0]
    keys_per_block = te // nk
    e1 = s % ne
    e2 = (s + ne - 1) % ne
    e3 = (s + 2 * ne - 2) % ne

    @pl.when(s == 0)
    def _():
        gl_b[...] = jnp.zeros_like(gl_b)
        gt_a[...] = jnp.zeros_like(gt_a)
        acc_ref[...] = jnp.zeros_like(acc_ref)

    @pl.when((e1 == 0) & (s < n_blocks))
    def _():
        xt_ref[...] = xn2_ref[...].astype(F32).T.astype(BF16)

    @pl.when((e3 == 0) & (s >= 2))
    def _():
        acc_ref[...] = jnp.zeros_like(acc_ref)

    zero = jnp.zeros((nk, LANES), BF16)
    one = jnp.ones((nk, LANES), BF16)

    keys_per_piece = 1
    n_pieces = keys_per_block // keys_per_piece

    def gate_piece(e, piece, lb):
        rows = []
        for ii in range(piece * keys_per_piece, (piece + 1) * keys_per_piece):
            base = pl.multiple_of((e * keys_per_block + ii) * nh, nh)
            rows.append((c0_ref[lb, pl.ds(base, nh), :], as_ref[lb, pl.ds(base, nh), :]))
        w = [zero] * keys_per_piece
        for h in range(nh):
            slab = pl.ds((lb * nh + h) * nk, nk)
            r1 = r1_ref[slab, :]
            bs = bs_ref[slab, :]
            for k, (crow, arow) in enumerate(rows):
                cb = _bcast_rows_bf16(crow[h:h + 1, :], nk)
                ab = _bcast_rows_bf16(arow[h:h + 1, :], nk)
                below = jnp.minimum(jnp.maximum(cb - r1, zero), one)
                w[k] = w[k] + (below * bs) * ab
        return w

    def step(gl_new, gl_old, gt_new, gt_old):
        D = vt_ref.shape[0]
        nlb = tm // LANES
        n_mxu = te // MXU_PIECE_ROWS
        out_rows = D // n_mxu

        def stage1(ii):
            rows = slice(ii * MXU_PIECE_ROWS, (ii + 1) * MXU_PIECE_ROWS)
            gl_new[rows, :] = _gelu_twice(_dot(u_ref[rows, :], xt_ref[...])).astype(BF16)

        def stage2(piece, lb):
            cols = slice(lb * LANES, (lb + 1) * LANES)
            for k, w in enumerate(gate_piece(e2, piece, lb)):
                ii = piece * keys_per_piece + k
                rows = slice(ii * nk, (ii + 1) * nk)
                gt_new[rows, cols] = gl_old[rows, cols] * w

        def stage3(ii):
            orow = slice(ii * out_rows, (ii + 1) * out_rows)
            acc_ref[orow, :] += _dot(vt_ref[orow, :], gt_old[...])

        mxu_work = []
        for ii in range(n_mxu):
            mxu_work += [functools.partial(stage1, ii), functools.partial(stage3, ii)]
        vpu_work = [functools.partial(stage2, piece, lb) for piece in range(n_pieces) for lb in range(nlb)]
        order = sorted([((i + 0.5) / len(mxu_work), 0, f) for i, f in enumerate(mxu_work)]
                       + [((i + 0.5) / len(vpu_work), 1, f) for i, f in enumerate(vpu_work)], key=lambda t: t[:2])
        for _, _, piece_fn in order:
            piece_fn()

    @pl.when(s % 2 == 0)
    def _():
        step(gl_a, gl_b, gt_b, gt_a)

    @pl.when(s % 2 == 1)
    def _():
        step(gl_b, gl_a, gt_a, gt_b)

    @pl.when((e3 == ne - 1) & (s >= 2))
    def _():
        y_ref[...] = x1_ref[...] + acc_ref[...].T


def _peer_main(xn2, x1, u_bf, vt_bf, r1, bs, c0, a_s, nh, nk, tm):
    N, D = xn2.shape
    ne, _, te = vt_bf.shape
    R = nk * nh
    n_blocks = (N // tm) * ne
    last = n_blocks - 1
    nlb = tm // LANES
    tile1 = lambda s: jnp.minimum(s, last) // ne
    tile2 = lambda s: jnp.clip(s - 1, 0, last) // ne
    tile3 = lambda s: jnp.clip(s - 2, 0, last) // ne
    by_row = pl.BlockSpec((nlb, R, LANES), lambda s: (tile2(s), 0, 0))
    by_head = pl.BlockSpec((nlb * nh * nk, LANES), lambda s: (tile2(s), 0))
    r1 = r1.reshape(-1, LANES)
    bs = bs.reshape(-1, LANES)
    return pl.pallas_call(
        functools.partial(_peer_main_kernel, nh=nh, nk=nk, ne=ne, n_blocks=n_blocks),
        grid=(n_blocks + 2,),
        in_specs=[pl.BlockSpec((tm, D), lambda s: (tile1(s), 0)),
                  pl.BlockSpec((tm, D), lambda s: (tile3(s), 0)),
                  pl.BlockSpec((te, D), lambda s: (s % ne, 0)),
                  pl.BlockSpec((None, D, te), lambda s: ((s + 2 * ne - 2) % ne, 0, 0)),
                  by_head, by_head, by_row, by_row],
        out_specs=pl.BlockSpec((tm, D), lambda s: (tile3(s), 0)),
        out_shape=jax.ShapeDtypeStruct((N, D), F32),
        scratch_shapes=[pltpu.VMEM((D, tm), BF16), pltpu.VMEM((D, tm), F32)]
        + [pltpu.VMEM((te, tm), BF16)] * 4,
        compiler_params=_cparams(("arbitrary",)),
        name="peer_main",
    )(xn2, x1, u_bf, vt_bf, r1, bs, c0, a_s)


def _rope_tables(pos, head_dim, width):
    half = head_dim // 2
    inv = ROPE_THETA ** (-jnp.arange(half, dtype=F32) / half)
    ang = pos.astype(F32)[:, None] * inv[None, :]
    cos = jnp.cos(ang)
    sin = jnp.sin(ang)
    reps = width // head_dim
    cos_t = jnp.tile(jnp.concatenate([cos, cos], axis=1), (1, reps))
    sin_t = jnp.tile(jnp.concatenate([-sin, sin], axis=1), (1, reps))
    return cos_t, sin_t


def _token_tile(n, pref):
    t = min(pref, n)
    while n % t:
        t //= 2
    return t


def _trunk(x, pos, conv_state, cache_k2, cache_v2, p):
    B, T, D = x.shape
    nh, hd = p["n_heads"], p["head_dim"]
    tm = _token_tile(T, 256)
    cos_t, sin_t = _rope_tables(pos, hd, 128)
    gaa, gb, new_conv = _branch_a(x, p["g1"], p["w_in"], p["b_gate"], p["conv_w"], conv_state, p["wco"], tm)
    k, v, qb, kb, vb = _qkv(x, p["g1"], p["w_in"], p["qg"], p["kg"], cos_t, sin_t, p["grp"], tm, hd)
    if cache_k2 is None:
        on = _attn_prompt(qb, kb, vb, p["lam4"], p["sg"], nh, _token_tile(T, 512), p["lam0"])
    else:
        on = _attn_sample(qb, kb, vb, cache_k2, cache_v2, p["lam4"], p["sg"], nh, p["lam0"])
    N = B * T
    tn = _token_tile(N, 256)
    x1, xn2 = _merge(x.reshape(N, D), gaa.reshape(N, D), gb.reshape(N, D), on.reshape(N, D),
                     p["wao"], p["wo"], p["g2"], tn)
    pnh, nk = p["peer_heads"], p["n_keys"]
    tp = _token_tile(N, 512)
    qg_rows = jnp.broadcast_to(p["peer_qg_col"], (p["peer_qg_col"].shape[0], tp))
    r1, bs, c0, a_s = _peer_select(xn2, p["wqt"], qg_rows, p["bd"], pnh, nk, tp)
    y = _peer_main(xn2, x1, p["u"], p["vt"], r1, bs, c0, a_s, pnh, nk, tp)
    return y.reshape(B, T, D), k, v, new_conv


def kernel(x_prompt, x_sample, cache_k, cache_v, state_conv, norm1_g, w_in, b_gate, conv_w, q_norm_g, k_norm_g,
           lam_q1, lam_k1, lam_q2, lam_k2, subln_g, w_conv_out, w_attn_out, w_o, norm2_g, peer_wq, peer_q_g,
           peer_subkeys, peer_u, peer_v):
    depth = w_in.shape[0]
    assert depth == 1
    l = LAYER_IDX
    D = x_prompt.shape[-1]
    n_heads, head_dim, v_dim = cache_k.shape[3], cache_k.shape[5], cache_v.shape[4]
    assert v_dim == 2 * head_dim and n_heads * v_dim == D and w_in.shape[2] == 8 * D
    n_keys, dk = peer_subkeys.shape[2], peer_subkeys.shape[3]
    assert n_keys == dk
    peer_heads = peer_wq.shape[2] // (2 * dk)
    past = cache_k.shape[2]
    assert past % CHUNK == 0 and (1 << CHUNK_SHIFT) == CHUNK

    eye_h = jnp.eye(peer_heads, dtype=F32)
    grp = np.kron(np.eye(256 // head_dim), np.ones((head_dim, head_dim))).astype(np.float32)
    p = dict(
        n_heads=n_heads, head_dim=head_dim, peer_heads=peer_heads, n_keys=n_keys, lam0=_lambda_init(l),
        g1=norm1_g[l][None, :], w_in=w_in[l].astype(BF16), b_gate=b_gate[l], conv_w=conv_w[l],
        wco=w_conv_out[l].astype(BF16), wao=w_attn_out[l].astype(BF16), wo=w_o[l].astype(BF16),
        qg=jnp.tile(q_norm_g[l], D // head_dim)[None, :], kg=jnp.tile(k_norm_g[l], D // head_dim)[None, :],
        grp=jnp.asarray(grp, BF16),
        lam4=jnp.stack([lam_q1[l], lam_k1[l], lam_q2[l], lam_k2[l]]),
        sg=subln_g[l][None, :], g2=norm2_g[l][None, :],
        wqt=peer_wq[l].reshape(D, peer_heads, 2, dk).transpose(2, 1, 3, 0).reshape(2 * peer_heads * dk, D).astype(BF16),
        peer_qg_col=jnp.broadcast_to(peer_q_g[l].reshape(2, 1, dk), (2, peer_heads, dk)).reshape(-1, 1),
        bd=jnp.einsum("ckd,hg->ckhgd", peer_subkeys[l], eye_h).reshape(2, n_keys * peer_heads, peer_heads * dk).astype(BF16),
        u=peer_u[l].astype(BF16),
        vt=peer_v[l].reshape(-1, PEER_EXPERT_BLOCK, D).transpose(0, 2, 1).astype(BF16),
    )

    B, T, _ = x_prompt.shape
    Bs, Ts, _ = x_sample.shape
    pos_p = jnp.arange(T, dtype=jnp.int32)
    pos_s = past + jnp.arange(Ts, dtype=jnp.int32)
    zero_conv = jnp.zeros((B, 2, D), x_prompt.dtype)
    y_p, k_p, v_p, c_p = _trunk(x_prompt, pos_p, zero_conv, None, None, p)
    y_s, k_s, v_s, c_s = _trunk(x_sample, pos_s, state_conv[l], cache_k[l].reshape(Bs, past, D),
                                cache_v[l].reshape(Bs, past, D), p)
    return (y_p, y_s,
            k_p.reshape(1, B, T, n_heads, 2, head_dim), v_p.reshape(1, B, T, n_heads, v_dim), c_p[None],
            k_s.reshape(1, Bs, Ts, n_heads, 2, head_dim), v_s.reshape(1, Bs, Ts, n_heads, v_dim), c_s[None])
```

```python
import functools
import math

import jax
import jax.numpy as jnp
import numpy as np
from jax import lax
from jax.experimental import pallas as pl
from jax.experimental.pallas import tpu as pltpu

EPS = 1e-6
CHUNK = 64
CHUNK_SHIFT = 6
ROPE_THETA = 10000.0
PEER_TOPK = 16
LAYER_IDX = 0

V7X_VMEM_LIMIT = 56 * 1024 * 1024
LANES = 128
PEER_EXPERT_BLOCK = 1024
MXU_PIECE_ROWS = 512

F32 = jnp.float32
BF16 = jnp.bfloat16


def _lambda_init(layer_idx):
    return 0.8 - 0.6 * math.exp(-0.3 * layer_idx)


def _dot(a, b):
    return jnp.dot(a, b, preferred_element_type=F32)


def _dot_nt(a, b):
    return lax.dot_general(a, b, (((1,), (1,)), ((), ())), preferred_element_type=F32)


def _rmsnorm_rows(x, g):
    r = lax.rsqrt(jnp.mean(x * x, axis=-1, keepdims=True) + EPS)
    return (x * r) * g


def _cparams(sem):
    return pltpu.CompilerParams(dimension_semantics=sem, vmem_limit_bytes=V7X_VMEM_LIMIT)


def _branch_a_kernel(x_ref, g1_ref, wb_ref, wc_ref, wh_ref, wga_ref, wgb_ref, bg_ref, cw_ref, cs_ref,
                     wco_ref, gaa_ref, gb_ref, nc_ref, carry_ref):
    t = pl.program_id(1)

    @pl.when(t == 0)
    def _():
        carry_ref[...] = cs_ref[...]

    x = x_ref[...]
    xn = _rmsnorm_rows(x, g1_ref[...]).astype(BF16)
    hb = _dot(xn, wb_ref[...])
    hc = _dot(xn, wc_ref[...])
    hh = _dot(xn, wh_ref[...])
    u = hc * hh
    tm = u.shape[0]
    row = lax.broadcasted_iota(jnp.int32, u.shape, 0)
    c0 = carry_ref[0:1, :]
    c1 = carry_ref[1:2, :]
    u1 = jnp.where(row == 0, c1, pltpu.roll(u, 1, 0))
    u2 = jnp.where(row == 0, c0, jnp.where(row == 1, c1, pltpu.roll(u, 2, 0)))
    y = cw_ref[0:1, :] * u2 + cw_ref[1:2, :] * u1 + cw_ref[2:3, :] * u
    conv_y = hb * y
    new_c = u[tm - 2:tm, :]
    carry_ref[...] = new_c
    nc_ref[...] = new_c
    a = _dot(conv_y.astype(BF16), wco_ref[...])
    ga = jax.nn.sigmoid(_dot(xn, wga_ref[...]) + bg_ref[0:1, :])
    gb = jax.nn.sigmoid(_dot(xn, wgb_ref[...]) + bg_ref[1:2, :])
    gaa_ref[...] = ga * a
    gb_ref[...] = gb


def _branch_a(x, g1, w_in_bf, b_gate, conv_w, conv_state, wco_bf, tm):
    B, T, D = x.shape
    nt = T // tm
    wspec = lambda j: pl.BlockSpec((D, D), lambda b, t, j=j: (0, j))
    tile = pl.BlockSpec((None, tm, D), lambda b, t: (b, t, 0))
    full2 = lambda r: pl.BlockSpec((r, D), lambda b, t: (0, 0))
    return pl.pallas_call(
        _branch_a_kernel,
        grid=(B, nt),
        in_specs=[tile, full2(1), wspec(0), wspec(1), wspec(2), wspec(6), wspec(7), full2(2), full2(3),
                  pl.BlockSpec((None, 2, D), lambda b, t: (b, 0, 0)), pl.BlockSpec((D, D), lambda b, t: (0, 0))],
        out_specs=[tile, tile, pl.BlockSpec((None, 2, D), lambda b, t: (b, 0, 0))],
        out_shape=[jax.ShapeDtypeStruct((B, T, D), F32), jax.ShapeDtypeStruct((B, T, D), F32),
                   jax.ShapeDtypeStruct((B, 2, D), F32)],
        scratch_shapes=[pltpu.VMEM((2, D), F32)],
        compiler_params=_cparams(("arbitrary", "arbitrary")),
        name="branch_a",
    )(x, g1, w_in_bf, w_in_bf, w_in_bf, w_in_bf, w_in_bf, b_gate, conv_w, conv_state, wco_bf)


def _qkv_kernel(x_ref, g1_ref, wq_ref, wk_ref, wv_ref, qg_ref, kg_ref, cos_ref, sin_ref, grp_ref,
                k_ref, v_ref, qb_ref, kb_ref, vb_ref, *, head_dim):
    x = x_ref[...]
    xn = _rmsnorm_rows(x, g1_ref[...]).astype(BF16)
    q = _dot(xn, wq_ref[...])
    k = _dot(xn, wk_ref[...])
    v = _dot(xn, wv_ref[...])
    D = q.shape[1]
    half = head_dim // 2
    reps = D // cos_ref.shape[1]
    cos = jnp.tile(cos_ref[...], (1, reps))
    sin = jnp.tile(sin_ref[...], (1, reps))
    lane = lax.broadcasted_iota(jnp.int32, q.shape, 1)
    first_half = (lane & half) == 0
    grp = grp_ref[...]
    gw = grp.shape[0]

    def headnorm_rope(z, g):
        z2 = z * z
        hi = z2.astype(BF16)
        lo = (z2 - hi.astype(F32)).astype(BF16)
        parts = []
        for s in range(D // gw):
            sl = slice(s * gw, (s + 1) * gw)
            parts.append(_dot(hi[:, sl], grp) + _dot(lo[:, sl], grp))
        ss = jnp.concatenate(parts, axis=1)
        r = lax.rsqrt(ss * (1.0 / head_dim) + EPS)
        zn = (z * r) * g
        sw = jnp.where(first_half, pltpu.roll(zn, D - half, 1), pltpu.roll(zn, half, 1))
        return zn * cos + sw * sin

    qr = headnorm_rope(q, qg_ref[...])
    kr = headnorm_rope(k, kg_ref[...])
    k_ref[...] = kr
    v_ref[...] = v
    hw = 2 * head_dim
    for dst, val in ((qb_ref, qr * (head_dim ** -0.5)), (kb_ref, kr), (vb_ref, v)):
        val = val.astype(BF16)
        for h in range(D // hw):
            dst[h] = val[:, h * hw:(h + 1) * hw]


def _qkv(x, g1, w_in_bf, qg_row, kg_row, cos_t, sin_t, grp, tm, head_dim):
    B, T, D = x.shape
    nt = T // tm
    hw = 2 * head_dim
    nh = D // hw
    wspec = lambda j: pl.BlockSpec((D, D), lambda b, t, j=j: (0, j))
    tile = pl.BlockSpec((None, tm, D), lambda b, t: (b, t, 0))
    row = pl.BlockSpec((1, D), lambda b, t: (0, 0))
    tab = pl.BlockSpec((tm, cos_t.shape[1]), lambda b, t: (t, 0))
    return pl.pallas_call(
        functools.partial(_qkv_kernel, head_dim=head_dim),
        grid=(B, nt),
        in_specs=[tile, row, wspec(3), wspec(4), wspec(5), row, row, tab, tab,
                  pl.BlockSpec(grp.shape, lambda b, t: (0, 0))],
        out_specs=[tile] * 2 + [pl.BlockSpec((None, nh, tm, hw), lambda b, t: (b, 0, t, 0))] * 3,
        out_shape=[jax.ShapeDtypeStruct((B, T, D), F32)] * 2 + [jax.ShapeDtypeStruct((B, nh, T, hw), BF16)] * 3,
        compiler_params=_cparams(("arbitrary", "arbitrary")),
        name="qkv",
    )(x, g1, w_in_bf, w_in_bf, w_in_bf, qg_row, kg_row, cos_t, sin_t, grp)


def _stack_halves(q, head_dim):
    lane = lax.broadcasted_iota(jnp.int32, q.shape, 1)
    zero = jnp.zeros_like(q)
    return jnp.concatenate([jnp.where(lane < head_dim, q, zero), jnp.where(lane >= head_dim, q, zero)], axis=0)


def _lambda_value(lam_ref, lam0):
    l = lam_ref[...]
    s1 = jnp.sum(l[0:1, :] * l[1:2, :], axis=-1, keepdims=True)
    s2 = jnp.sum(l[2:3, :] * l[3:4, :], axis=-1, keepdims=True)
    return jnp.exp(s1) - jnp.exp(s2) + lam0


def _attn_finish(acc, l, tq, lam, sg, lam0):
    o = acc[:tq] / l[:tq] - lam * (acc[tq:] / l[tq:])
    return _rmsnorm_rows(o, sg) * (1.0 - lam0)


def _attn_prompt_kernel(q_ref, k_ref, v_ref, lam_ref, sg_ref, o_ref, *, tq, head_dim, lam0):
    qi = pl.program_id(2)
    qs = _stack_halves(q_ref[...], head_dim)

    def block(j, carry, masked):
        m, l, acc = carry
        start = pl.multiple_of(j * tq, tq)
        kb = k_ref[pl.ds(start, tq), :]
        vb = v_ref[pl.ds(start, tq), :]
        s = _dot_nt(qs, kb)
        if masked:
            r = lax.broadcasted_iota(jnp.int32, s.shape, 0)
            c = lax.broadcasted_iota(jnp.int32, s.shape, 1)
            rr = jnp.where(r >= tq, r - tq, r)
            vis = (c >> CHUNK_SHIFT) <= (rr >> CHUNK_SHIFT)
            s = jnp.where(vis, s, -1e30)
        m_new = jnp.maximum(m, jnp.max(s, axis=-1, keepdims=True))
        alpha = jnp.exp(m - m_new)
        p = jnp.exp(s - m_new)
        l = alpha * l + jnp.sum(p, axis=-1, keepdims=True)
        acc = alpha * acc + _dot(p.astype(BF16), vb)
        return m_new, l, acc

    init = (jnp.full((2 * tq, 1), -1e30, F32), jnp.zeros((2 * tq, 1), F32),
            jnp.zeros((2 * tq, v_ref.shape[1]), F32))
    carry = lax.fori_loop(0, qi, lambda j, c: block(j, c, False), init)
    _, l, acc = block(qi, carry, True)
    o_ref[...] = _attn_finish(acc, l, tq, _lambda_value(lam_ref, lam0), sg_ref[...], lam0).astype(o_ref.dtype)


def _attn_prompt(qb, kb, vb, lam4, sg_row, n_heads, tq, lam0):
    B, n_heads_, T, hw = qb.shape
    assert n_heads_ == n_heads
    D = n_heads * hw
    head_dim = hw // 2
    assert tq % CHUNK == 0 and T % tq == 0
    return pl.pallas_call(
        functools.partial(_attn_prompt_kernel, tq=tq, head_dim=head_dim, lam0=lam0),
        grid=(B, n_heads, T // tq),
        in_specs=[pl.BlockSpec((None, None, tq, hw), lambda b, h, i: (b, h, i, 0)),
                  pl.BlockSpec((None, None, T, hw), lambda b, h, i: (b, h, 0, 0)),
                  pl.BlockSpec((None, None, T, hw), lambda b, h, i: (b, h, 0, 0)),
                  pl.BlockSpec(lam4.shape, lambda b, h, i: (0, 0)),
                  pl.BlockSpec((1, hw), lambda b, h, i: (0, 0))],
        out_specs=pl.BlockSpec((None, tq, hw), lambda b, h, i: (b, i, h)),
        out_shape=jax.ShapeDtypeStruct((B, T, D), BF16),
        compiler_params=_cparams(("arbitrary", "arbitrary", "arbitrary")),
        name="attn_prompt",
    )(qb, kb, vb, lam4, sg_row)


def _attn_sample_kernel(q_ref, kc_ref, vc_ref, kn_ref, vn_ref, lam_ref, sg_ref, o_ref, *, head_dim, lam0):
    ts = q_ref.shape[0]
    past = kc_ref.shape[0]
    qs = _stack_halves(q_ref[...], head_dim)
    sc = _dot_nt(qs, kc_ref[...].astype(BF16))
    sn = _dot_nt(qs, kn_ref[...])

    def visible(shape, k_off):
        r = lax.broadcasted_iota(jnp.int32, shape, 0)
        c = lax.broadcasted_iota(jnp.int32, shape, 1)
        q_pos = past + jnp.where(r >= ts, r - ts, r)
        return ((c + k_off) >> CHUNK_SHIFT) <= (q_pos >> CHUNK_SHIFT)

    sc = jnp.where(visible(sc.shape, 0), sc, -1e30)
    sn = jnp.where(visible(sn.shape, past), sn, -1e30)
    m = jnp.maximum(jnp.max(sc, axis=-1, keepdims=True), jnp.max(sn, axis=-1, keepdims=True))
    pc = jnp.exp(sc - m)
    pn = jnp.exp(sn - m)
    l = jnp.sum(pc, axis=-1, keepdims=True) + jnp.sum(pn, axis=-1, keepdims=True)
    acc = _dot(pc.astype(BF16), vc_ref[...].astype(BF16)) + _dot(pn.astype(BF16), vn_ref[...])
    o_ref[...] = _attn_finish(acc, l, ts, _lambda_value(lam_ref, lam0), sg_ref[...], lam0).astype(o_ref.dtype)


def _attn_sample(qb, kb, vb, cache_k2, cache_v2, lam4, sg_row, n_heads, lam0):
    B, _, Ts, hw = qb.shape
    D = n_heads * hw
    P = cache_k2.shape[1]
    new = pl.BlockSpec((None, None, Ts, hw), lambda b, h: (b, h, 0, 0))
    old = pl.BlockSpec((None, P, hw), lambda b, h: (b, 0, h))
    return pl.pallas_call(
        functools.partial(_attn_sample_kernel, head_dim=hw // 2, lam0=lam0),
        grid=(B, n_heads),
        in_specs=[new, old, old, new, new, pl.BlockSpec(lam4.shape, lambda b, h: (0, 0)),
                  pl.BlockSpec((1, hw), lambda b, h: (0, 0))],
        out_specs=pl.BlockSpec((None, Ts, hw), lambda b, h: (b, 0, h)),
        out_shape=jax.ShapeDtypeStruct((B, Ts, D), BF16),
        compiler_params=_cparams(("arbitrary", "arbitrary")),
        name="attn_sample",
    )(qb, cache_k2, cache_v2, kb, vb, lam4, sg_row)


def _merge_kernel(x_ref, gaa_ref, gb_ref, on_ref, wao_ref, wo_ref, g2_ref, x1_ref, xn2_ref):
    b = _dot(on_ref[...], wao_ref[...])
    mix = gaa_ref[...] + gb_ref[...] * b
    x1 = x_ref[...] + _dot(mix.astype(BF16), wo_ref[...])
    x1_ref[...] = x1
    xn2_ref[...] = _rmsnorm_rows(x1, g2_ref[...]).astype(BF16)


def _merge(x2, gaa2, gb2, on2, wao_bf, wo_bf, g2, tm):
    N, D = x2.shape
    tile = pl.BlockSpec((tm, D), lambda t: (t, 0))
    wfull = pl.BlockSpec((D, D), lambda t: (0, 0))
    return pl.pallas_call(
        _merge_kernel,
        grid=(N // tm,),
        in_specs=[tile, tile, tile, tile, wfull, wfull, pl.BlockSpec((1, D), lambda t: (0, 0))],
        out_specs=[tile, tile],
        out_shape=[jax.ShapeDtypeStruct((N, D), F32), jax.ShapeDtypeStruct((N, D), BF16)],
        compiler_params=_cparams(("arbitrary",)),
        name="merge",
    )(x2, gaa2, gb2, on2, wao_bf, wo_bf, g2)


def _staircase_pairs(topk):
    return [(a, b) for a in range(topk) for b in range(topk) if (a + 1) * (b + 1) <= topk]


def _peer_select_kernel(xn2_ref, wqt_ref, qg_ref, bd_ref, r1_ref, bs_ref, c0_ref, as_ref,
                        s_ref, work_ref, rank_ref, top_ref, idx_ref, tr_ref, *, nh, nk, topk):
    tm = xn2_ref.shape[0]
    dk = nk
    qp = _dot_nt(wqt_ref[...], xn2_ref[...])
    qg = qg_ref[...]

    for h in range(nh):
        lo = qp[h * dk:(h + 1) * dk, :]
        hi = qp[(nh + h) * dk:(nh + h + 1) * dk, :]
        ss = jnp.sum(lo * lo + hi * hi, axis=0, keepdims=True)
        r = lax.rsqrt(ss * (1.0 / (2 * dk)) + EPS)
        for c, part in ((0, lo), (1, hi)):
            rows = slice((c * nh + h) * dk, (c * nh + h + 1) * dk)
            s_ref[c, pl.ds(h * dk, dk), :] = (part * r) * qg[rows, :]

    for c in range(2):
        qn = s_ref[c].astype(BF16)
        s_ref[c] = _dot(bd_ref[c], qn)

    neg_inf = jnp.float32(-jnp.inf)
    n_chains = 8
    per_chain = nk // n_chains

    for c in range(2):
        work_ref[...] = s_ref[c].reshape(nk, nh, tm)
        if c == 1:
            rank_ref[...] = jnp.full((nk, nh, tm), float(topk), F32)

        def round_body(r, _, c=c):
            chains = []
            for g in range(n_chains):
                k0 = g * per_chain
                m = work_ref[k0]
                idx = jnp.full((nh, tm), k0, jnp.int32)
                for k in range(k0 + 1, k0 + per_chain):
                    w = work_ref[k]
                    gt = w > m
                    m = jnp.where(gt, w, m)
                    idx = jnp.where(gt, k, idx)
                chains.append((m, idx))
            while len(chains) > 1:
                merged = []
                for (ma, ia), (mb, ib) in zip(chains[0::2], chains[1::2]):
                    gt = mb > ma
                    merged.append((jnp.where(gt, mb, ma), jnp.where(gt, ib, ia)))
                chains = merged
            m, idx = chains[0]
            top_ref[c, pl.ds(r, 1)] = m[None]
            if c == 0:
                idx_ref[pl.ds(r, 1)] = idx[None]
            rf = lax.convert_element_type(r, F32)
            for k in range(nk):
                oh = idx == k
                work_ref[k] = jnp.where(oh, neg_inf, work_ref[k])
                if c == 1:
                    rank_ref[k] = jnp.where(oh, rf, rank_ref[k])
            return 0

        lax.fori_loop(0, topk, round_body, 0)

    A = [top_ref[0, a] for a in range(topk)]
    Bv = [top_ref[1, b] for b in range(topk)]

    pairs = _staircase_pairs(topk)
    cand = [A[a] + Bv[b] for (a, b) in pairs]
    n = len(pairs)
    beaten = [jnp.zeros((nh, tm), F32) for _ in range(n)]
    for p in range(n):
        ap, bp = pairs[p]
        for q in range(p + 1, n):
            aq, bq = pairs[q]
            if ap <= aq and bp <= bq:
                beaten[q] = beaten[q] + 1.0
            else:
                t = jnp.where(cand[p] >= cand[q], 1.0, 0.0)
                beaten[q] = beaten[q] + t
                beaten[p] = beaten[p] + (1.0 - t)
    sel = [jnp.where(beaten[p] < float(topk), 1.0, 0.0) for p in range(n)]

    ea = [jnp.exp(A[a] - A[0]) for a in range(topk)]
    eb = [jnp.exp(Bv[b] - Bv[0]) for b in range(topk)]
    z = jnp.zeros((nh, tm), F32)
    cnt = [jnp.zeros((nh, tm), F32) for _ in range(topk)]
    for p, (a, b) in enumerate(pairs):
        z = z + sel[p] * (ea[a] * eb[b])
        cnt[a] = cnt[a] + sel[p]
    inv_z = 1.0 / z

    def emit(out_ref, val):
        val = val.reshape(nk * nh, tm)
        for lb in range(tm // LANES):
            out_ref[lb] = val[:, lb * LANES:(lb + 1) * LANES]

    def emit_by_head(out_ref, val):
        emit(tr_ref, val)
        for lb in range(tm // LANES):
            for h in range(nh):
                out_ref[lb, h] = tr_ref[lb, pl.ds(h, nk, stride=nh), :].astype(BF16)

    s0 = s_ref[0].reshape(nk, nh, tm)
    s1 = s_ref[1].reshape(nk, nh, tm)
    emit(as_ref, jnp.exp(s0 - A[0][None]) * (0.5 * inv_z)[None])
    emit_by_head(bs_ref, jnp.exp(s1 - Bv[0][None]))
    kio = lax.broadcasted_iota(jnp.int32, (nk, nh, tm), 0)
    c0 = jnp.zeros((nk, nh, tm), F32)
    for a in range(topk):
        c0 = jnp.where(kio == idx_ref[a][None], cnt[a][None], c0)
    emit(c0_ref, c0)
    emit_by_head(r1_ref, rank_ref[...])


def _peer_select(xn2, wqt_bf, qg_rows, bd_bf, nh, nk, tm):
    N, D = xn2.shape
    R = nk * nh
    assert tm % LANES == 0
    nlb = tm // LANES
    by_row = pl.BlockSpec((nlb, R, LANES), lambda t: (t, 0, 0))
    by_head = pl.BlockSpec((nlb, nh, nk, LANES), lambda t: (t, 0, 0, 0))
    return pl.pallas_call(
        functools.partial(_peer_select_kernel, nh=nh, nk=nk, topk=PEER_TOPK),
        grid=(N // tm,),
        in_specs=[pl.BlockSpec((tm, D), lambda t: (t, 0)),
                  pl.BlockSpec(wqt_bf.shape, lambda t: (0, 0)),
                  pl.BlockSpec(qg_rows.shape, lambda t: (0, 0)),
                  pl.BlockSpec(bd_bf.shape, lambda t: (0, 0, 0))],
        out_specs=[by_head, by_head, by_row, by_row],
        out_shape=[jax.ShapeDtypeStruct((N // LANES, nh, nk, LANES), BF16)] * 2
        + [jax.ShapeDtypeStruct((N // LANES, R, LANES), F32)] * 2,
        scratch_shapes=[pltpu.VMEM((2, R, tm), F32), pltpu.VMEM((nk, nh, tm), F32),
                        pltpu.VMEM((nk, nh, tm), F32), pltpu.VMEM((2, PEER_TOPK, nh, tm), F32),
                        pltpu.VMEM((PEER_TOPK, nh, tm), jnp.int32), pltpu.VMEM((nlb, R, LANES), F32)],
        compiler_params=_cparams(("arbitrary",)),
        name="peer_select",
    )(xn2, wqt_bf, qg_rows, bd_bf)


def _gelu_twice(x):
    return x + x * lax.erf(x * (2.0 ** -0.5))


BF16_SUBLANE_ROWS = 16


def _bcast_rows_bf16(row, n):
    one = jnp.broadcast_to(row, (BF16_SUBLANE_ROWS, row.shape[1])).astype(BF16)
    return jnp.tile(one, (n // BF16_SUBLANE_ROWS, 1))


def _peer_main_kernel(xn2_ref, x1_ref, u_ref, vt_ref, r1_ref, bs_ref, c0_ref, as_ref, y_ref,
                      xt_ref, acc_ref, gl_a, gl_b, gt_a, gt_b, *, nh, nk, ne, n_blocks):
    s = pl.program_id(0)
    te = u_ref.shape[0]
    tm = xn2_ref.shape[0]
    keys_per_block = te // nk
    e1 = s % ne
    e2 = (s + ne - 1) % ne
    e3 = (s + 2 * ne - 2) % ne

    @pl.when(s == 0)
    def _():
        gl_b[...] = jnp.zeros_like(gl_b)
        gt_a[...] = jnp.zeros_like(gt_a)
        acc_ref[...] = jnp.zeros_like(acc_ref)

    @pl.when((e1 == 0) & (s < n_blocks))
    def _():
        xt_ref[...] = xn2_ref[...].astype(F32).T.astype(BF16)

    @pl.when((e3 == 0) & (s >= 2))
    def _():
        acc_ref[...] = jnp.zeros_like(acc_ref)

    zero = jnp.zeros((nk, LANES), BF16)
    one = jnp.ones((nk, LANES), BF16)

    keys_per_piece = 1
    n_pieces = keys_per_block // keys_per_piece

    def gate_piece(e, piece, lb):
        rows = []
        for ii in range(piece * keys_per_piece, (piece + 1) * keys_per_piece):
            base = pl.multiple_of((e * keys_per_block + ii) * nh, nh)
            rows.append((c0_ref[lb, pl.ds(base, nh), :], as_ref[lb, pl.ds(base, nh), :]))
        w = [zero] * keys_per_piece
        for h in range(nh):
            slab = pl.ds((lb * nh + h) * nk, nk)
            r1 = r1_ref[slab, :]
            bs = bs_ref[slab, :]
            for k, (crow, arow) in enumerate(rows):
                cb = _bcast_rows_bf16(crow[h:h + 1, :], nk)
                ab = _bcast_rows_bf16(arow[h:h + 1, :], nk)
                below = jnp.minimum(jnp.maximum(cb - r1, zero), one)
                w[k] = w[k] + (below * bs) * ab
        return w

    def step(gl_new, gl_old, gt_new, gt_old):
        D = vt_ref.shape[0]
        nlb = tm // LANES
        n_mxu = te // MXU_PIECE_ROWS
        out_rows = D // n_mxu

        def stage1(ii):
            rows = slice(ii * MXU_PIECE_ROWS, (ii + 1) * MXU_PIECE_ROWS)
            gl_new[rows, :] = _gelu_twice(_dot(u_ref[rows, :], xt_ref[...])).astype(BF16)

        def stage2(piece, lb):
            cols = slice(lb * LANES, (lb + 1) * LANES)
            for k, w in enumerate(gate_piece(e2, piece, lb)):
                ii = piece * keys_per_piece + k
                rows = slice(ii * nk, (ii + 1) * nk)
                gt_new[rows, cols] = gl_old[rows, cols] * w

        def stage3(ii):
            orow = slice(ii * out_rows, (ii + 1) * out_rows)
            acc_ref[orow, :] += _dot(vt_ref[orow, :], gt_old[...])

        mxu_work = []
        for ii in range(n_mxu):
            mxu_work += [functools.partial(stage1, ii), functools.partial(stage3, ii)]
        vpu_work = [functools.partial(stage2, piece, lb) for piece in range(n_pieces) for lb in range(nlb)]
        order = sorted([((i + 0.5) / len(mxu_work), 0, f) for i, f in enumerate(mxu_work)]
                       + [((i + 0.5) / len(vpu_work), 1, f) for i, f in enumerate(vpu_work)], key=lambda t: t[:2])
        for _, _, piece_fn in order:
            piece_fn()

    @pl.when(s % 2 == 0)
    def _():
        step(gl_a, gl_b, gt_b, gt_a)

    @pl.when(s % 2 == 1)
    def _():
        step(gl_b, gl_a, gt_a, gt_b)

    @pl.when((e3 == ne - 1) & (s >= 2))
    def _():
        y_ref[...] = x1_ref[...] + acc_ref[...].T


def _peer_main(xn2, x1, u_bf, vt_bf, r1, bs, c0, a_s, nh, nk, tm):
    N, D = xn2.shape
    ne, _, te = vt_bf.shape
    R = nk * nh
    n_blocks = (N // tm) * ne
    last = n_blocks - 1
    nlb = tm // LANES
    tile1 = lambda s: jnp.minimum(s, last) // ne
    tile2 = lambda s: jnp.clip(s - 1, 0, last) // ne
    tile3 = lambda s: jnp.clip(s - 2, 0, last) // ne
    by_row = pl.BlockSpec((nlb, R, LANES), lambda s: (tile2(s), 0, 0))
    by_head = pl.BlockSpec((nlb * nh * nk, LANES), lambda s: (tile2(s), 0))
    r1 = r1.reshape(-1, LANES)
    bs = bs.reshape(-1, LANES)
    return pl.pallas_call(
        functools.partial(_peer_main_kernel, nh=nh, nk=nk, ne=ne, n_blocks=n_blocks),
        grid=(n_blocks + 2,),
        in_specs=[pl.BlockSpec((tm, D), lambda s: (tile1(s), 0)),
                  pl.BlockSpec((tm, D), lambda s: (tile3(s), 0)),
                  pl.BlockSpec((te, D), lambda s: (s % ne, 0)),
                  pl.BlockSpec((None, D, te), lambda s: ((s + 2 * ne - 2) % ne, 0, 0)),
                  by_head, by_head, by_row, by_row],
        out_specs=pl.BlockSpec((tm, D), lambda s: (tile3(s), 0)),
        out_shape=jax.ShapeDtypeStruct((N, D), F32),
        scratch_shapes=[pltpu.VMEM((D, tm), BF16), pltpu.VMEM((D, tm), F32)]
        + [pltpu.VMEM((te, tm), BF16)] * 4,
        compiler_params=_cparams(("arbitrary",)),
        name="peer_main",
    )(xn2, x1, u_bf, vt_bf, r1, bs, c0, a_s)


def _rope_tables(pos, head_dim, width):
    half = head_dim // 2
    inv = ROPE_THETA ** (-jnp.arange(half, dtype=F32) / half)
    ang = pos.astype(F32)[:, None] * inv[None, :]
    cos = jnp.cos(ang)
    sin = jnp.sin(ang)
    reps = width // head_dim
    cos_t = jnp.tile(jnp.concatenate([cos, cos], axis=1), (1, reps))
    sin_t = jnp.tile(jnp.concatenate([-sin, sin], axis=1), (1, reps))
    return cos_t, sin_t


def _token_tile(n, pref):
    t = min(pref, n)
    while n % t:
        t //= 2
    return t


def _trunk(x, pos, conv_state, cache_k2, cache_v2, p):
    B, T, D = x.shape
    nh, hd = p["n_heads"], p["head_dim"]
    tm = _token_tile(T, 256)
    cos_t, sin_t = _rope_tables(pos, hd, 128)
    gaa, gb, new_conv = _branch_a(x, p["g1"], p["w_in"], p["b_gate"], p["conv_w"], conv_state, p["wco"], tm)
    k, v, qb, kb, vb = _qkv(x, p["g1"], p["w_in"], p["qg"], p["kg"], cos_t, sin_t, p["grp"], tm, hd)
    if cache_k2 is None:
        on = _attn_prompt(qb, kb, vb, p["lam4"], p["sg"], nh, _token_tile(T, 512), p["lam0"])
    else:
        on = _attn_sample(qb, kb, vb, cache_k2, cache_v2, p["lam4"], p["sg"], nh, p["lam0"])
    N = B * T
    tn = _token_tile(N, 256)
    x1, xn2 = _merge(x.reshape(N, D), gaa.reshape(N, D), gb.reshape(N, D), on.reshape(N, D),
                     p["wao"], p["wo"], p["g2"], tn)
    pnh, nk = p["peer_heads"], p["n_keys"]
    tp = _token_tile(N, 512)
    qg_rows = jnp.broadcast_to(p["peer_qg_col"], (p["peer_qg_col"].shape[0], tp))
    r1, bs, c0, a_s = _peer_select(xn2, p["wqt"], qg_rows, p["bd"], pnh, nk, tp)
    y = _peer_main(xn2, x1, p["u"], p["vt"], r1, bs, c0, a_s, pnh, nk, tp)
    return y.reshape(B, T, D), k, v, new_conv


def kernel(x_prompt, x_sample, cache_k, cache_v, state_conv, norm1_g, w_in, b_gate, conv_w, q_norm_g, k_norm_g,
           lam_q1, lam_k1, lam_q2, lam_k2, subln_g, w_conv_out, w_attn_out, w_o, norm2_g, peer_wq, peer_q_g,
           peer_subkeys, peer_u, peer_v):
    depth = w_in.shape[0]
    assert depth == 1
    l = LAYER_IDX
    D = x_prompt.shape[-1]
    n_heads, head_dim, v_dim = cache_k.shape[3], cache_k.shape[5], cache_v.shape[4]
    assert v_dim == 2 * head_dim and n_heads * v_dim == D and w_in.shape[2] == 8 * D
    n_keys, dk = peer_subkeys.shape[2], peer_subkeys.shape[3]
    assert n_keys == dk
    peer_heads = peer_wq.shape[2] // (2 * dk)
    past = cache_k.shape[2]
    assert past % CHUNK == 0 and (1 << CHUNK_SHIFT) == CHUNK

    eye_h = jnp.eye(peer_heads, dtype=F32)
    grp = np.kron(np.eye(256 // head_dim), np.ones((head_dim, head_dim))).astype(np.float32)
    p = dict(
        n_heads=n_heads, head_dim=head_dim, peer_heads=peer_heads, n_keys=n_keys, lam0=_lambda_init(l),
        g1=norm1_g[l][None, :], w_in=w_in[l].astype(BF16), b_gate=b_gate[l], conv_w=conv_w[l],
        wco=w_conv_out[l].astype(BF16), wao=w_attn_out[l].astype(BF16), wo=w_o[l].astype(BF16),
        qg=jnp.tile(q_norm_g[l], D // head_dim)[None, :], kg=jnp.tile(k_norm_g[l], D // head_dim)[None, :],
        grp=jnp.asarray(grp, BF16),
        lam4=jnp.stack([lam_q1[l], lam_k1[l], lam_q2[l], lam_k2[l]]),
        sg=subln_g[l][None, :], g2=norm2_g[l][None, :],
        wqt=peer_wq[l].reshape(D, peer_heads, 2, dk).transpose(2, 1, 3, 0).reshape(2 * peer_heads * dk, D).astype(BF16),
        peer_qg_col=jnp.broadcast_to(peer_q_g[l].reshape(2, 1, dk), (2, peer_heads, dk)).reshape(-1, 1),
        bd=jnp.einsum("ckd,hg->ckhgd", peer_subkeys[l], eye_h).reshape(2, n_keys * peer_heads, peer_heads * dk).astype(BF16),
        u=peer_u[l].astype(BF16),
        vt=peer_v[l].reshape(-1, PEER_EXPERT_BLOCK, D).transpose(0, 2, 1).astype(BF16),
    )

    B, T, _ = x_prompt.shape
    Bs, Ts, _ = x_sample.shape
    pos_p = jnp.arange(T, dtype=jnp.int32)
    pos_s = past + jnp.arange(Ts, dtype=jnp.int32)
    zero_conv = jnp.zeros((B, 2, D), x_prompt.dtype)
    y_p, k_p, v_p, c_p = _trunk(x_prompt, pos_p, zero_conv, None, None, p)
    y_s, k_s, v_s, c_s = _trunk(x_sample, pos_s, state_conv[l], cache_k[l].reshape(Bs, past, D),
                                cache_v[l].reshape(Bs, past, D), p)
    return (y_p, y_s,
            k_p.reshape(1, B, T, n_heads, 2, head_dim), v_p.reshape(1, B, T, n_heads, v_dim), c_p[None],
            k_s.reshape(1, Bs, Ts, n_heads, 2, head_dim), v_s.reshape(1, Bs, Ts, n_heads, v_dim), c_s[None])
```

```python
import functools
import math

import jax
import jax.numpy as jnp
import numpy as np
from jax import lax
from jax.experimental import pallas as pl
from jax.experimental.pallas import tpu as pltpu

EPS = 1e-6
CHUNK = 64
CHUNK_SHIFT = 6
ROPE_THETA = 10000.0
PEER_TOPK = 16
LAYER_IDX = 0

V7X_VMEM_LIMIT = 56 * 1024 * 1024
LANES = 128
PEER_EXPERT_BLOCK = 1024
MXU_PIECE_ROWS = 512

F32 = jnp.float32
BF16 = jnp.bfloat16


def _lambda_init(layer_idx):
    return 0.8 - 0.6 * math.exp(-0.3 * layer_idx)


def _dot(a, b):
    return jnp.dot(a, b, preferred_element_type=F32)


def _dot_nt(a, b):
    return lax.dot_general(a, b, (((1,), (1,)), ((), ())), preferred_element_type=F32)


def _rmsnorm_rows(x, g):
    r = lax.rsqrt(jnp.mean(x * x, axis=-1, keepdims=True) + EPS)
    return (x * r) * g


def _cparams(sem):
    return pltpu.CompilerParams(dimension_semantics=sem, vmem_limit_bytes=V7X_VMEM_LIMIT)


def _branch_a_kernel(x_ref, g1_ref, wb_ref, wc_ref, wh_ref, wga_ref, wgb_ref, bg_ref, cw_ref, cs_ref,
                     wco_ref, gaa_ref, gb_ref, nc_ref, carry_ref):
    t = pl.program_id(1)

    @pl.when(t == 0)
    def _():
        carry_ref[...] = cs_ref[...]

    x = x_ref[...]
    xn = _rmsnorm_rows(x, g1_ref[...]).astype(BF16)
    hb = _dot(xn, wb_ref[...])
    hc = _dot(xn, wc_ref[...])
    hh = _dot(xn, wh_ref[...])
    u = hc * hh
    tm = u.shape[0]
    row = lax.broadcasted_iota(jnp.int32, u.shape, 0)
    c0 = carry_ref[0:1, :]
    c1 = carry_ref[1:2, :]
    u1 = jnp.where(row == 0, c1, pltpu.roll(u, 1, 0))
    u2 = jnp.where(row == 0, c0, jnp.where(row == 1, c1, pltpu.roll(u, 2, 0)))
    y = cw_ref[0:1, :] * u2 + cw_ref[1:2, :] * u1 + cw_ref[2:3, :] * u
    conv_y = hb * y
    new_c = u[tm - 2:tm, :]
    carry_ref[...] = new_c
    nc_ref[...] = new_c
    a = _dot(conv_y.astype(BF16), wco_ref[...])
    ga = jax.nn.sigmoid(_dot(xn, wga_ref[...]) + bg_ref[0:1, :])
    gb = jax.nn.sigmoid(_dot(xn, wgb_ref[...]) + bg_ref[1:2, :])
    gaa_ref[...] = ga * a
    gb_ref[...] = gb


def _branch_a(x, g1, w_in_bf, b_gate, conv_w, conv_state, wco_bf, tm):
    B, T, D = x.shape
    nt = T // tm
    wspec = lambda j: pl.BlockSpec((D, D), lambda b, t, j=j: (0, j))
    tile = pl.BlockSpec((None, tm, D), lambda b, t: (b, t, 0))
    full2 = lambda r: pl.BlockSpec((r, D), lambda b, t: (0, 0))
    return pl.pallas_call(
        _branch_a_kernel,
        grid=(B, nt),
        in_specs=[tile, full2(1), wspec(0), wspec(1), wspec(2), wspec(6), wspec(7), full2(2), full2(3),
                  pl.BlockSpec((None, 2, D), lambda b, t: (b, 0, 0)), pl.BlockSpec((D, D), lambda b, t: (0, 0))],
        out_specs=[tile, tile, pl.BlockSpec((None, 2, D), lambda b, t: (b, 0, 0))],
        out_shape=[jax.ShapeDtypeStruct((B, T, D), F32), jax.ShapeDtypeStruct((B, T, D), F32),
                   jax.ShapeDtypeStruct((B, 2, D), F32)],
        scratch_shapes=[pltpu.VMEM((2, D), F32)],
        compiler_params=_cparams(("arbitrary", "arbitrary")),
        name="branch_a",
    )(x, g1, w_in_bf, w_in_bf, w_in_bf, w_in_bf, w_in_bf, b_gate, conv_w, conv_state, wco_bf)


def _qkv_kernel(x_ref, g1_ref, wq_ref, wk_ref, wv_ref, qg_ref, kg_ref, cos_ref, sin_ref, grp_ref,
                k_ref, v_ref, qb_ref, kb_ref, vb_ref, *, head_dim):
    x = x_ref[...]
    xn = _rmsnorm_rows(x, g1_ref[...]).astype(BF16)
    q = _dot(xn, wq_ref[...])
    k = _dot(xn, wk_ref[...])
    v = _dot(xn, wv_ref[...])
    D = q.shape[1]
    half = head_dim // 2
    reps = D // cos_ref.shape[1]
    cos = jnp.tile(cos_ref[...], (1, reps))
    sin = jnp.tile(sin_ref[...], (1, reps))
    lane = lax.broadcasted_iota(jnp.int32, q.shape, 1)
    first_half = (lane & half) == 0
    grp = grp_ref[...]
    gw = grp.shape[0]

    def headnorm_rope(z, g):
        z2 = z * z
        hi = z2.astype(BF16)
        lo = (z2 - hi.astype(F32)).astype(BF16)
        parts = []
        for s in range(D // gw):
            sl = slice(s * gw, (s + 1) * gw)
            parts.append(_dot(hi[:, sl], grp) + _dot(lo[:, sl], grp))
        ss = jnp.concatenate(parts, axis=1)
        r = lax.rsqrt(ss * (1.0 / head_dim) + EPS)
        zn = (z * r) * g
        sw = jnp.where(first_half, pltpu.roll(zn, D - half, 1), pltpu.roll(zn, half, 1))
        return zn * cos + sw * sin

    qr = headnorm_rope(q, qg_ref[...])
    kr = headnorm_rope(k, kg_ref[...])
    k_ref[...] = kr
    v_ref[...] = v
    hw = 2 * head_dim
    for dst, val in ((qb_ref, qr * (head_dim ** -0.5)), (kb_ref, kr), (vb_ref, v)):
        val = val.astype(BF16)
        for h in range(D // hw):
            dst[h] = val[:, h * hw:(h + 1) * hw]


def _qkv(x, g1, w_in_bf, qg_row, kg_row, cos_t, sin_t, grp, tm, head_dim):
    B, T, D = x.shape
    nt = T // tm
    hw = 2 * head_dim
    nh = D // hw
    wspec = lambda j: pl.BlockSpec((D, D), lambda b, t, j=j: (0, j))
    tile = pl.BlockSpec((None, tm, D), lambda b, t: (b, t, 0))
    row = pl.BlockSpec((1, D), lambda b, t: (0, 0))
    tab = pl.BlockSpec((tm, cos_t.shape[1]), lambda b, t: (t, 0))
    return pl.pallas_call(
        functools.partial(_qkv_kernel, head_dim=head_dim),
        grid=(B, nt),
        in_specs=[tile, row, wspec(3), wspec(4), wspec(5), row, row, tab, tab,
                  pl.BlockSpec(grp.shape, lambda b, t: (0, 0))],
        out_specs=[tile] * 2 + [pl.BlockSpec((None, nh, tm, hw), lambda b, t: (b, 0, t, 0))] * 3,
        out_shape=[jax.ShapeDtypeStruct((B, T, D), F32)] * 2 + [jax.ShapeDtypeStruct((B, nh, T, hw), BF16)] * 3,
        compiler_params=_cparams(("arbitrary", "arbitrary")),
        name="qkv",
    )(x, g1, w_in_bf, w_in_bf, w_in_bf, qg_row, kg_row, cos_t, sin_t, grp)


def _stack_halves(q, head_dim):
    lane = lax.broadcasted_iota(jnp.int32, q.shape, 1)
    zero = jnp.zeros_like(q)
    return jnp.concatenate([jnp.where(lane < head_dim, q, zero), jnp.where(lane >= head_dim, q, zero)], axis=0)


def _lambda_value(lam_ref, lam0):
    l = lam_ref[...]
    s1 = jnp.sum(l[0:1, :] * l[1:2, :], axis=-1, keepdims=True)
    s2 = jnp.sum(l[2:3, :] * l[3:4, :], axis=-1, keepdims=True)
    return jnp.exp(s1) - jnp.exp(s2) + lam0


def _attn_finish(acc, l, tq, lam, sg, lam0):
    o = acc[:tq] / l[:tq] - lam * (acc[tq:] / l[tq:])
    return _rmsnorm_rows(o, sg) * (1.0 - lam0)


def _attn_prompt_kernel(q_ref, k_ref, v_ref, lam_ref, sg_ref, o_ref, *, tq, head_dim, lam0):
    qi = pl.program_id(2)
    qs = _stack_halves(q_ref[...], head_dim)

    def block(j, carry, masked):
        m, l, acc = carry
        start = pl.multiple_of(j * tq, tq)
        kb = k_ref[pl.ds(start, tq), :]
        vb = v_ref[pl.ds(start, tq), :]
        s = _dot_nt(qs, kb)
        if masked:
            r = lax.broadcasted_iota(jnp.int32, s.shape, 0)
            c = lax.broadcasted_iota(jnp.int32, s.shape, 1)
            rr = jnp.where(r >= tq, r - tq, r)
            vis = (c >> CHUNK_SHIFT) <= (rr >> CHUNK_SHIFT)
            s = jnp.where(vis, s, -1e30)
        m_new = jnp.maximum(m, jnp.max(s, axis=-1, keepdims=True))
        alpha = jnp.exp(m - m_new)
        p = jnp.exp(s - m_new)
        l = alpha * l + jnp.sum(p, axis=-1, keepdims=True)
        acc = alpha * acc + _dot(p.astype(BF16), vb)
        return m_new, l, acc

    init = (jnp.full((2 * tq, 1), -1e30, F32), jnp.zeros((2 * tq, 1), F32),
            jnp.zeros((2 * tq, v_ref.shape[1]), F32))
    carry = lax.fori_loop(0, qi, lambda j, c: block(j, c, False), init)
    _, l, acc = block(qi, carry, True)
    o_ref[...] = _attn_finish(acc, l, tq, _lambda_value(lam_ref, lam0), sg_ref[...], lam0).astype(o_ref.dtype)


def _attn_prompt(qb, kb, vb, lam4, sg_row, n_heads, tq, lam0):
    B, n_heads_, T, hw = qb.shape
    assert n_heads_ == n_heads
    D = n_heads * hw
    head_dim = hw // 2
    assert tq % CHUNK == 0 and T % tq == 0
    return pl.pallas_call(
        functools.partial(_attn_prompt_kernel, tq=tq, head_dim=head_dim, lam0=lam0),
        grid=(B, n_heads, T // tq),
        in_specs=[pl.BlockSpec((None, None, tq, hw), lambda b, h, i: (b, h, i, 0)),
                  pl.BlockSpec((None, None, T, hw), lambda b, h, i: (b, h, 0, 0)),
                  pl.BlockSpec((None, None, T, hw), lambda b, h, i: (b, h, 0, 0)),
                  pl.BlockSpec(lam4.shape, lambda b, h, i: (0, 0)),
                  pl.BlockSpec((1, hw), lambda b, h, i: (0, 0))],
        out_specs=pl.BlockSpec((None, tq, hw), lambda b, h, i: (b, i, h)),
        out_shape=jax.ShapeDtypeStruct((B, T, D), BF16),
        compiler_params=_cparams(("arbitrary", "arbitrary", "arbitrary")),
        name="attn_prompt",
    )(qb, kb, vb, lam4, sg_row)


def _attn_sample_kernel(q_ref, kc_ref, vc_ref, kn_ref, vn_ref, lam_ref, sg_ref, o_ref, *, head_dim, lam0):
    ts = q_ref.shape[0]
    past = kc_ref.shape[0]
    qs = _stack_halves(q_ref[...], head_dim)
    sc = _dot_nt(qs, kc_ref[...].astype(BF16))
    sn = _dot_nt(qs, kn_ref[...])

    def visible(shape, k_off):
        r = lax.broadcasted_iota(jnp.int32, shape, 0)
        c = lax.broadcasted_iota(jnp.int32, shape, 1)
        q_pos = past + jnp.where(r >= ts, r - ts, r)
        return ((c + k_off) >> CHUNK_SHIFT) <= (q_pos >> CHUNK_SHIFT)

    sc = jnp.where(visible(sc.shape, 0), sc, -1e30)
    sn = jnp.where(visible(sn.shape, past), sn, -1e30)
    m = jnp.maximum(jnp.max(sc, axis=-1, keepdims=True), jnp.max(sn, axis=-1, keepdims=True))
    pc = jnp.exp(sc - m)
    pn = jnp.exp(sn - m)
    l = jnp.sum(pc, axis=-1, keepdims=True) + jnp.sum(pn, axis=-1, keepdims=True)
    acc = _dot(pc.astype(BF16), vc_ref[...].astype(BF16)) + _dot(pn.astype(BF16), vn_ref[...])
    o_ref[...] = _attn_finish(acc, l, ts, _lambda_value(lam_ref, lam0), sg_ref[...], lam0).astype(o_ref.dtype)


def _attn_sample(qb, kb, vb, cache_k2, cache_v2, lam4, sg_row, n_heads, lam0):
    B, _, Ts, hw = qb.shape
    D = n_heads * hw
    P = cache_k2.shape[1]
    new = pl.BlockSpec((None, None, Ts, hw), lambda b, h: (b, h, 0, 0))
    old = pl.BlockSpec((None, P, hw), lambda b, h: (b, 0, h))
    return pl.pallas_call(
        functools.partial(_attn_sample_kernel, head_dim=hw // 2, lam0=lam0),
        grid=(B, n_heads),
        in_specs=[new, old, old, new, new, pl.BlockSpec(lam4.shape, lambda b, h: (0, 0)),
                  pl.BlockSpec((1, hw), lambda b, h: (0, 0))],
        out_specs=pl.BlockSpec((None, Ts, hw), lambda b, h: (b, 0, h)),
        out_shape=jax.ShapeDtypeStruct((B, Ts, D), BF16),
        compiler_params=_cparams(("arbitrary", "arbitrary")),
        name="attn_sample",
    )(qb, cache_k2, cache_v2, kb, vb, lam4, sg_row)


def _merge_kernel(x_ref, gaa_ref, gb_ref, on_ref, wao_ref, wo_ref, g2_ref, x1_ref, xn2_ref):
    b = _dot(on_ref[...], wao_ref[...])
    mix = gaa_ref[...] + gb_ref[...] * b
    x1 = x_ref[...] + _dot(mix.astype(BF16), wo_ref[...])
    x1_ref[...] = x1
    xn2_ref[...] = _rmsnorm_rows(x1, g2_ref[...]).astype(BF16)


def _merge(x2, gaa2, gb2, on2, wao_bf, wo_bf, g2, tm):
    N, D = x2.shape
    tile = pl.BlockSpec((tm, D), lambda t: (t, 0))
    wfull = pl.BlockSpec((D, D), lambda t: (0, 0))
    return pl.pallas_call(
        _merge_kernel,
        grid=(N // tm,),
        in_specs=[tile, tile, tile, tile, wfull, wfull, pl.BlockSpec((1, D), lambda t: (0, 0))],
        out_specs=[tile, tile],
        out_shape=[jax.ShapeDtypeStruct((N, D), F32), jax.ShapeDtypeStruct((N, D), BF16)],
        compiler_params=_cparams(("arbitrary",)),
        name="merge",
    )(x2, gaa2, gb2, on2, wao_bf, wo_bf, g2)


def _staircase_pairs(topk):
    return [(a, b) for a in range(topk) for b in range(topk) if (a + 1) * (b + 1) <= topk]


def _peer_select_kernel(xn2_ref, wqt_ref, qg_ref, bd_ref, r1_ref, bs_ref, c0_ref, as_ref,
                        s_ref, work_ref, rank_ref, top_ref, idx_ref, tr_ref, *, nh, nk, topk):
    tm = xn2_ref.shape[0]
    dk = nk
    qp = _dot_nt(wqt_ref[...], xn2_ref[...])
    qg = qg_ref[...]

    for h in range(nh):
        lo = qp[h * dk:(h + 1) * dk, :]
        hi = qp[(nh + h) * dk:(nh + h + 1) * dk, :]
        ss = jnp.sum(lo * lo + hi * hi, axis=0, keepdims=True)
        r = lax.rsqrt(ss * (1.0 / (2 * dk)) + EPS)
        for c, part in ((0, lo), (1, hi)):
            rows = slice((c * nh + h) * dk, (c * nh + h + 1) * dk)
            s_ref[c, pl.ds(h * dk, dk), :] = (part * r) * qg[rows, :]

    for c in range(2):
        qn = s_ref[c].astype(BF16)
        s_ref[c] = _dot(bd_ref[c], qn)

    neg_inf = jnp.float32(-jnp.inf)
    n_chains = 8
    per_chain = nk // n_chains

    for c in range(2):
        work_ref[...] = s_ref[c].reshape(nk, nh, tm)
        if c == 1:
            rank_ref[...] = jnp.full((nk, nh, tm), float(topk), F32)

        def round_body(r, _, c=c):
            chains = []
            for g in range(n_chains):
                k0 = g * per_chain
                m = work_ref[k0]
                idx = jnp.full((nh, tm), k0, jnp.int32)
                for k in range(k0 + 1, k0 + per_chain):
                    w = work_ref[k]
                    gt = w > m
                    m = jnp.where(gt, w, m)
                    idx = jnp.where(gt, k, idx)
                chains.append((m, idx))
            while len(chains) > 1:
                merged = []
                for (ma, ia), (mb, ib) in zip(chains[0::2], chains[1::2]):
                    gt = mb > ma
                    merged.append((jnp.where(gt, mb, ma), jnp.where(gt, ib, ia)))
                chains = merged
            m, idx = chains[0]
            top_ref[c, pl.ds(r, 1)] = m[None]
            if c == 0:
                idx_ref[pl.ds(r, 1)] = idx[None]
            rf = lax.convert_element_type(r, F32)
            for k in range(nk):
                oh = idx == k
                work_ref[k] = jnp.where(oh, neg_inf, work_ref[k])
                if c == 1:
                    rank_ref[k] = jnp.where(oh, rf, rank_ref[k])
            return 0

        lax.fori_loop(0, topk, round_body, 0)

    A = [top_ref[0, a] for a in range(topk)]
    Bv = [top_ref[1, b] for b in range(topk)]

    pairs = _staircase_pairs(topk)
    cand = [A[a] + Bv[b] for (a, b) in pairs]
    n = len(pairs)
    beaten = [jnp.zeros((nh, tm), F32) for _ in range(n)]
    for p in range(n):
        ap, bp = pairs[p]
        for q in range(p + 1, n):
            aq, bq = pairs[q]
            if ap <= aq and bp <= bq:
                beaten[q] = beaten[q] + 1.0
            else:
                t = jnp.where(cand[p] >= cand[q], 1.0, 0.0)
                beaten[q] = beaten[q] + t
                beaten[p] = beaten[p] + (1.0 - t)
    sel = [jnp.where(beaten[p] < float(topk), 1.0, 0.0) for p in range(n)]

    ea = [jnp.exp(A[a] - A[0]) for a in range(topk)]
    eb = [jnp.exp(Bv[b] - Bv[0]) for b in range(topk)]
    z = jnp.zeros((nh, tm), F32)
    cnt = [jnp.zeros((nh, tm), F32) for _ in range(topk)]
    for p, (a, b) in enumerate(pairs):
        z = z + sel[p] * (ea[a] * eb[b])
        cnt[a] = cnt[a] + sel[p]
    inv_z = 1.0 / z

    def emit(out_ref, val):
        val = val.reshape(nk * nh, tm)
        for lb in range(tm // LANES):
            out_ref[lb] = val[:, lb * LANES:(lb + 1) * LANES]

    def emit_by_head(out_ref, val):
        emit(tr_ref, val)
        for lb in range(tm // LANES):
            for h in range(nh):
                out_ref[lb, h] = tr_ref[lb, pl.ds(h, nk, stride=nh), :].astype(BF16)

    s0 = s_ref[0].reshape(nk, nh, tm)
    s1 = s_ref[1].reshape(nk, nh, tm)
    emit(as_ref, jnp.exp(s0 - A[0][None]) * (0.5 * inv_z)[None])
    emit_by_head(bs_ref, jnp.exp(s1 - Bv[0][None]))
    kio = lax.broadcasted_iota(jnp.int32, (nk, nh, tm), 0)
    c0 = jnp.zeros((nk, nh, tm), F32)
    for a in range(topk):
        c0 = jnp.where(kio == idx_ref[a][None], cnt[a][None], c0)
    emit(c0_ref, c0)
    emit_by_head(r1_ref, rank_ref[...])


def _peer_select(xn2, wqt_bf, qg_rows, bd_bf, nh, nk, tm):
    N, D = xn2.shape
    R = nk * nh
    assert tm % LANES == 0
    nlb = tm // LANES
    by_row = pl.BlockSpec((nlb, R, LANES), lambda t: (t, 0, 0))
    by_head = pl.BlockSpec((nlb, nh, nk, LANES), lambda t: (t, 0, 0, 0))
    return pl.pallas_call(
        functools.partial(_peer_select_kernel, nh=nh, nk=nk, topk=PEER_TOPK),
        grid=(N // tm,),
        in_specs=[pl.BlockSpec((tm, D), lambda t: (t, 0)),
                  pl.BlockSpec(wqt_bf.shape, lambda t: (0, 0)),
                  pl.BlockSpec(qg_rows.shape, lambda t: (0, 0)),
                  pl.BlockSpec(bd_bf.shape, lambda t: (0, 0, 0))],
        out_specs=[by_head, by_head, by_row, by_row],
        out_shape=[jax.ShapeDtypeStruct((N // LANES, nh, nk, LANES), BF16)] * 2
        + [jax.ShapeDtypeStruct((N // LANES, R, LANES), F32)] * 2,
        scratch_shapes=[pltpu.VMEM((2, R, tm), F32), pltpu.VMEM((nk, nh, tm), F32),
                        pltpu.VMEM((nk, nh, tm), F32), pltpu.VMEM((2, PEER_TOPK, nh, tm), F32),
                        pltpu.VMEM((PEER_TOPK, nh, tm), jnp.int32), pltpu.VMEM((nlb, R, LANES), F32)],
        compiler_params=_cparams(("arbitrary",)),
        name="peer_select",
    )(xn2, wqt_bf, qg_rows, bd_bf)


def _gelu_twice(x):
    return x + x * lax.erf(x * (2.0 ** -0.5))


BF16_SUBLANE_ROWS = 16


def _bcast_rows_bf16(row, n):
    one = jnp.broadcast_to(row, (BF16_SUBLANE_ROWS, row.shape[1])).astype(BF16)
    return jnp.tile(one, (n // BF16_SUBLANE_ROWS, 1))


def _peer_main_kernel(xn2_ref, x1_ref, u_ref, vt_ref, r1_ref, bs_ref, c0_ref, as_ref, y_ref,
                      xt_ref, acc_ref, gl_a, gl_b, gt_a, gt_b, *, nh, nk, ne, n_blocks):
    s = pl.program_id(0)
    te = u_ref.shape[0]
    tm = xn2_ref.shape[0]
    keys_per_block = te // nk
    e1 = s % ne
    e2 = (s + ne - 1) % ne
    e3 = (s + 2 * ne - 2) % ne

    @pl.when(s == 0)
    def _():
        gl_b[...] = jnp.zeros_like(gl_b)
        gt_a[...] = jnp.zeros_like(gt_a)
        acc_ref[...] = jnp.zeros_like(acc_ref)

    @pl.when((e1 == 0) & (s < n_blocks))
    def _():
        xt_ref[...] = xn2_ref[...].astype(F32).T.astype(BF16)

    @pl.when((e3 == 0) & (s >= 2))
    def _():
        acc_ref[...] = jnp.zeros_like(acc_ref)

    zero = jnp.zeros((nk, LANES), BF16)
    one = jnp.ones((nk, LANES), BF16)

    keys_per_piece = 1
    n_pieces = keys_per_block // keys_per_piece

    def gate_piece(e, piece, lb):
        rows = []
        for ii in range(piece * keys_per_piece, (piece + 1) * keys_per_piece):
            base = pl.multiple_of((e * keys_per_block + ii) * nh, nh)
            rows.append((c0_ref[lb, pl.ds(base, nh), :], as_ref[lb, pl.ds(base, nh), :]))
        w = [zero] * keys_per_piece
        for h in range(nh):
            slab = pl.ds((lb * nh + h) * nk, nk)
            r1 = r1_ref[slab, :]
            bs = bs_ref[slab, :]
            for k, (crow, arow) in enumerate(rows):
                cb = _bcast_rows_bf16(crow[h:h + 1, :], nk)
                ab = _bcast_rows_bf16(arow[h:h + 1, :], nk)
                below = jnp.minimum(jnp.maximum(cb - r1, zero), one)
                w[k] = w[k] + (below * bs) * ab
        return w

    def step(gl_new, gl_old, gt_new, gt_old):
        D = vt_ref.shape[0]
        nlb = tm // LANES
        n_mxu = te // MXU_PIECE_ROWS
        out_rows = D // n_mxu

        def stage1(ii):
            rows = slice(ii * MXU_PIECE_ROWS, (ii + 1) * MXU_PIECE_ROWS)
            gl_new[rows, :] = _gelu_twice(_dot(u_ref[rows, :], xt_ref[...])).astype(BF16)

        def stage2(piece, lb):
            cols = slice(lb * LANES, (lb + 1) * LANES)
            for k, w in enumerate(gate_piece(e2, piece, lb)):
                ii = piece * keys_per_piece + k
                rows = slice(ii * nk, (ii + 1) * nk)
                gt_new[rows, cols] = gl_old[rows, cols] * w

        def stage3(ii):
            orow = slice(ii * out_rows, (ii + 1) * out_rows)
            acc_ref[orow, :] += _dot(vt_ref[orow, :], gt_old[...])

        mxu_work = []
        mxu_work = [functools.partial(stage1, ii) for ii in range(n_mxu)]
        mxu_work += [functools.partial(stage3, ii) for ii in range(n_mxu)]
        vpu_work = [functools.partial(stage2, piece, lb) for piece in range(n_pieces) for lb in range(nlb)]
        order = sorted([((i + 0.5) / len(mxu_work), 0, f) for i, f in enumerate(mxu_work)]
                       + [((i + 0.5) / len(vpu_work), 1, f) for i, f in enumerate(vpu_work)], key=lambda t: t[:2])
        for _, _, piece_fn in order:
            piece_fn()

    @pl.when(s % 2 == 0)
    def _():
        step(gl_a, gl_b, gt_b, gt_a)

    @pl.when(s % 2 == 1)
    def _():
        step(gl_b, gl_a, gt_a, gt_b)

    @pl.when((e3 == ne - 1) & (s >= 2))
    def _():
        y_ref[...] = x1_ref[...] + acc_ref[...].T


def _peer_main(xn2, x1, u_bf, vt_bf, r1, bs, c0, a_s, nh, nk, tm):
    N, D = xn2.shape
    ne, _, te = vt_bf.shape
    R = nk * nh
    n_blocks = (N // tm) * ne
    last = n_blocks - 1
    nlb = tm // LANES
    tile1 = lambda s: jnp.minimum(s, last) // ne
    tile2 = lambda s: jnp.clip(s - 1, 0, last) // ne
    tile3 = lambda s: jnp.clip(s - 2, 0, last) // ne
    by_row = pl.BlockSpec((nlb, R, LANES), lambda s: (tile2(s), 0, 0))
    by_head = pl.BlockSpec((nlb * nh * nk, LANES), lambda s: (tile2(s), 0))
    r1 = r1.reshape(-1, LANES)
    bs = bs.reshape(-1, LANES)
    return pl.pallas_call(
        functools.partial(_peer_main_kernel, nh=nh, nk=nk, ne=ne, n_blocks=n_blocks),
        grid=(n_blocks + 2,),
        in_specs=[pl.BlockSpec((tm, D), lambda s: (tile1(s), 0)),
                  pl.BlockSpec((tm, D), lambda s: (tile3(s), 0)),
                  pl.BlockSpec((te, D), lambda s: (s % ne, 0)),
                  pl.BlockSpec((None, D, te), lambda s: ((s + 2 * ne - 2) % ne, 0, 0)),
                  by_head, by_head, by_row, by_row],
        out_specs=pl.BlockSpec((tm, D), lambda s: (tile3(s), 0)),
        out_shape=jax.ShapeDtypeStruct((N, D), F32),
        scratch_shapes=[pltpu.VMEM((D, tm), BF16), pltpu.VMEM((D, tm), F32)]
        + [pltpu.VMEM((te, tm), BF16)] * 4,
        compiler_params=_cparams(("arbitrary",)),
        name="peer_main",
    )(xn2, x1, u_bf, vt_bf, r1, bs, c0, a_s)


def _rope_tables(pos, head_dim, width):
    half = head_dim // 2
    inv = ROPE_THETA ** (-jnp.arange(half, dtype=F32) / half)
    ang = pos.astype(F32)[:, None] * inv[None, :]
    cos = jnp.cos(ang)
    sin = jnp.sin(ang)
    reps = width // head_dim
    cos_t = jnp.tile(jnp.concatenate([cos, cos], axis=1), (1, reps))
    sin_t = jnp.tile(jnp.concatenate([-sin, sin], axis=1), (1, reps))
    return cos_t, sin_t


def _token_tile(n, pref):
    t = min(pref, n)
    while n % t:
        t //= 2
    return t


def _trunk(x, pos, conv_state, cache_k2, cache_v2, p):
    B, T, D = x.shape
    nh, hd = p["n_heads"], p["head_dim"]
    tm = _token_tile(T, 256)
    cos_t, sin_t = _rope_tables(pos, hd, 128)
    gaa, gb, new_conv = _branch_a(x, p["g1"], p["w_in"], p["b_gate"], p["conv_w"], conv_state, p["wco"], tm)
    k, v, qb, kb, vb = _qkv(x, p["g1"], p["w_in"], p["qg"], p["kg"], cos_t, sin_t, p["grp"], tm, hd)
    if cache_k2 is None:
        on = _attn_prompt(qb, kb, vb, p["lam4"], p["sg"], nh, _token_tile(T, 512), p["lam0"])
    else:
        on = _attn_sample(qb, kb, vb, cache_k2, cache_v2, p["lam4"], p["sg"], nh, p["lam0"])
    N = B * T
    tn = _token_tile(N, 256)
    x1, xn2 = _merge(x.reshape(N, D), gaa.reshape(N, D), gb.reshape(N, D), on.reshape(N, D),
                     p["wao"], p["wo"], p["g2"], tn)
    pnh, nk = p["peer_heads"], p["n_keys"]
    tp = _token_tile(N, 512)
    qg_rows = jnp.broadcast_to(p["peer_qg_col"], (p["peer_qg_col"].shape[0], tp))
    r1, bs, c0, a_s = _peer_select(xn2, p["wqt"], qg_rows, p["bd"], pnh, nk, tp)
    y = _peer_main(xn2, x1, p["u"], p["vt"], r1, bs, c0, a_s, pnh, nk, tp)
    return y.reshape(B, T, D), k, v, new_conv


def kernel(x_prompt, x_sample, cache_k, cache_v, state_conv, norm1_g, w_in, b_gate, conv_w, q_norm_g, k_norm_g,
           lam_q1, lam_k1, lam_q2, lam_k2, subln_g, w_conv_out, w_attn_out, w_o, norm2_g, peer_wq, peer_q_g,
           peer_subkeys, peer_u, peer_v):
    depth = w_in.shape[0]
    assert depth == 1
    l = LAYER_IDX
    D = x_prompt.shape[-1]
    n_heads, head_dim, v_dim = cache_k.shape[3], cache_k.shape[5], cache_v.shape[4]
    assert v_dim == 2 * head_dim and n_heads * v_dim == D and w_in.shape[2] == 8 * D
    n_keys, dk = peer_subkeys.shape[2], peer_subkeys.shape[3]
    assert n_keys == dk
    peer_heads = peer_wq.shape[2] // (2 * dk)
    past = cache_k.shape[2]
    assert past % CHUNK == 0 and (1 << CHUNK_SHIFT) == CHUNK

    eye_h = jnp.eye(peer_heads, dtype=F32)
    grp = np.kron(np.eye(256 // head_dim), np.ones((head_dim, head_dim))).astype(np.float32)
    p = dict(
        n_heads=n_heads, head_dim=head_dim, peer_heads=peer_heads, n_keys=n_keys, lam0=_lambda_init(l),
        g1=norm1_g[l][None, :], w_in=w_in[l].astype(BF16), b_gate=b_gate[l], conv_w=conv_w[l],
        wco=w_conv_out[l].astype(BF16), wao=w_attn_out[l].astype(BF16), wo=w_o[l].astype(BF16),
        qg=jnp.tile(q_norm_g[l], D // head_dim)[None, :], kg=jnp.tile(k_norm_g[l], D // head_dim)[None, :],
        grp=jnp.asarray(grp, BF16),
        lam4=jnp.stack([lam_q1[l], lam_k1[l], lam_q2[l], lam_k2[l]]),
        sg=subln_g[l][None, :], g2=norm2_g[l][None, :],
        wqt=peer_wq[l].reshape(D, peer_heads, 2, dk).transpose(2, 1, 3, 0).reshape(2 * peer_heads * dk, D).astype(BF16),
        peer_qg_col=jnp.broadcast_to(peer_q_g[l].reshape(2, 1, dk), (2, peer_heads, dk)).reshape(-1, 1),
        bd=jnp.einsum("ckd,hg->ckhgd", peer_subkeys[l], eye_h).reshape(2, n_keys * peer_heads, peer_heads * dk).astype(BF16),
        u=peer_u[l].astype(BF16),
        vt=peer_v[l].reshape(-1, PEER_EXPERT_BLOCK, D).transpose(0, 2, 1).astype(BF16),
    )

    B, T, _ = x_prompt.shape
    Bs, Ts, _ = x_sample.shape
    pos_p = jnp.arange(T, dtype=jnp.int32)
    pos_s = past + jnp.arange(Ts, dtype=jnp.int32)
    zero_conv = jnp.zeros((B, 2, D), x_prompt.dtype)
    y_p, k_p, v_p, c_p = _trunk(x_prompt, pos_p, zero_conv, None, None, p)
    y_s, k_s, v_s, c_s = _trunk(x_sample, pos_s, state_conv[l], cache_k[l].reshape(Bs, past, D),
                                cache_v[l].reshape(Bs, past, D), p)
    return (y_p, y_s,
            k_p.reshape(1, B, T, n_heads, 2, head_dim), v_p.reshape(1, B, T, n_heads, v_dim), c_p[None],
            k_s.reshape(1, Bs, Ts, n_heads, 2, head_dim), v_s.reshape(1, Bs, Ts, n_heads, v_dim), c_s[None])
```

```python
import functools
import math

import jax
import jax.numpy as jnp
import numpy as np
from jax import lax
from jax.experimental import pallas as pl
from jax.experimental.pallas import tpu as pltpu

EPS = 1e-6
CHUNK = 64
CHUNK_SHIFT = 6
ROPE_THETA = 10000.0
PEER_TOPK = 16
LAYER_IDX = 0

V7X_VMEM_LIMIT = 56 * 1024 * 1024
LANES = 128
PEER_EXPERT_BLOCK = 2048
MXU_PIECE_ROWS = 1024

F32 = jnp.float32
BF16 = jnp.bfloat16


def _lambda_init(layer_idx):
    return 0.8 - 0.6 * math.exp(-0.3 * layer_idx)


def _dot(a, b):
    return jnp.dot(a, b, preferred_element_type=F32)


def _dot_nt(a, b):
    return lax.dot_general(a, b, (((1,), (1,)), ((), ())), preferred_element_type=F32)


def _rmsnorm_rows(x, g):
    r = lax.rsqrt(jnp.mean(x * x, axis=-1, keepdims=True) + EPS)
    return (x * r) * g


def _cparams(sem):
    return pltpu.CompilerParams(dimension_semantics=sem, vmem_limit_bytes=V7X_VMEM_LIMIT)


def _branch_a_kernel(x_ref, g1_ref, wb_ref, wc_ref, wh_ref, wga_ref, wgb_ref, bg_ref, cw_ref, cs_ref,
                     wco_ref, gaa_ref, gb_ref, nc_ref, carry_ref):
    t = pl.program_id(1)

    @pl.when(t == 0)
    def _():
        carry_ref[...] = cs_ref[...]

    x = x_ref[...]
    xn = _rmsnorm_rows(x, g1_ref[...]).astype(BF16)
    hb = _dot(xn, wb_ref[...])
    hc = _dot(xn, wc_ref[...])
    hh = _dot(xn, wh_ref[...])
    u = hc * hh
    tm = u.shape[0]
    row = lax.broadcasted_iota(jnp.int32, u.shape, 0)
    c0 = carry_ref[0:1, :]
    c1 = carry_ref[1:2, :]
    u1 = jnp.where(row == 0, c1, pltpu.roll(u, 1, 0))
    u2 = jnp.where(row == 0, c0, jnp.where(row == 1, c1, pltpu.roll(u, 2, 0)))
    y = cw_ref[0:1, :] * u2 + cw_ref[1:2, :] * u1 + cw_ref[2:3, :] * u
    conv_y = hb * y
    new_c = u[tm - 2:tm, :]
    carry_ref[...] = new_c
    nc_ref[...] = new_c
    a = _dot(conv_y.astype(BF16), wco_ref[...])
    ga = jax.nn.sigmoid(_dot(xn, wga_ref[...]) + bg_ref[0:1, :])
    gb = jax.nn.sigmoid(_dot(xn, wgb_ref[...]) + bg_ref[1:2, :])
    gaa_ref[...] = ga * a
    gb_ref[...] = gb


def _branch_a(x, g1, w_in_bf, b_gate, conv_w, conv_state, wco_bf, tm):
    B, T, D = x.shape
    nt = T // tm
    wspec = lambda j: pl.BlockSpec((D, D), lambda b, t, j=j: (0, j))
    tile = pl.BlockSpec((None, tm, D), lambda b, t: (b, t, 0))
    full2 = lambda r: pl.BlockSpec((r, D), lambda b, t: (0, 0))
    return pl.pallas_call(
        _branch_a_kernel,
        grid=(B, nt),
        in_specs=[tile, full2(1), wspec(0), wspec(1), wspec(2), wspec(6), wspec(7), full2(2), full2(3),
                  pl.BlockSpec((None, 2, D), lambda b, t: (b, 0, 0)), pl.BlockSpec((D, D), lambda b, t: (0, 0))],
        out_specs=[tile, tile, pl.BlockSpec((None, 2, D), lambda b, t: (b, 0, 0))],
        out_shape=[jax.ShapeDtypeStruct((B, T, D), F32), jax.ShapeDtypeStruct((B, T, D), F32),
                   jax.ShapeDtypeStruct((B, 2, D), F32)],
        scratch_shapes=[pltpu.VMEM((2, D), F32)],
        compiler_params=_cparams(("arbitrary", "arbitrary")),
        name="branch_a",
    )(x, g1, w_in_bf, w_in_bf, w_in_bf, w_in_bf, w_in_bf, b_gate, conv_w, conv_state, wco_bf)


def _qkv_kernel(x_ref, g1_ref, wq_ref, wk_ref, wv_ref, qg_ref, kg_ref, cos_ref, sin_ref, grp_ref,
                k_ref, v_ref, qb_ref, kb_ref, vb_ref, *, head_dim):
    x = x_ref[...]
    xn = _rmsnorm_rows(x, g1_ref[...]).astype(BF16)
    q = _dot(xn, wq_ref[...])
    k = _dot(xn, wk_ref[...])
    v = _dot(xn, wv_ref[...])
    D = q.shape[1]
    half = head_dim // 2
    reps = D // cos_ref.shape[1]
    cos = jnp.tile(cos_ref[...], (1, reps))
    sin = jnp.tile(sin_ref[...], (1, reps))
    lane = lax.broadcasted_iota(jnp.int32, q.shape, 1)
    first_half = (lane & half) == 0
    grp = grp_ref[...]
    gw = grp.shape[0]

    def headnorm_rope(z, g):
        z2 = z * z
        hi = z2.astype(BF16)
        lo = (z2 - hi.astype(F32)).astype(BF16)
        parts = []
        for s in range(D // gw):
            sl = slice(s * gw, (s + 1) * gw)
            parts.append(_dot(hi[:, sl], grp) + _dot(lo[:, sl], grp))
        ss = jnp.concatenate(parts, axis=1)
        r = lax.rsqrt(ss * (1.0 / head_dim) + EPS)
        zn = (z * r) * g
        sw = jnp.where(first_half, pltpu.roll(zn, D - half, 1), pltpu.roll(zn, half, 1))
        return zn * cos + sw * sin

    qr = headnorm_rope(q, qg_ref[...])
    kr = headnorm_rope(k, kg_ref[...])
    k_ref[...] = kr
    v_ref[...] = v
    hw = 2 * head_dim
    for dst, val in ((qb_ref, qr * (head_dim ** -0.5)), (kb_ref, kr), (vb_ref, v)):
        val = val.astype(BF16)
        for h in range(D // hw):
            dst[h] = val[:, h * hw:(h + 1) * hw]


def _qkv(x, g1, w_in_bf, qg_row, kg_row, cos_t, sin_t, grp, tm, head_dim):
    B, T, D = x.shape
    nt = T // tm
    hw = 2 * head_dim
    nh = D // hw
    wspec = lambda j: pl.BlockSpec((D, D), lambda b, t, j=j: (0, j))
    tile = pl.BlockSpec((None, tm, D), lambda b, t: (b, t, 0))
    row = pl.BlockSpec((1, D), lambda b, t: (0, 0))
    tab = pl.BlockSpec((tm, cos_t.shape[1]), lambda b, t: (t, 0))
    return pl.pallas_call(
        functools.partial(_qkv_kernel, head_dim=head_dim),
        grid=(B, nt),
        in_specs=[tile, row, wspec(3), wspec(4), wspec(5), row, row, tab, tab,
                  pl.BlockSpec(grp.shape, lambda b, t: (0, 0))],
        out_specs=[tile] * 2 + [pl.BlockSpec((None, nh, tm, hw), lambda b, t: (b, 0, t, 0))] * 3,
        out_shape=[jax.ShapeDtypeStruct((B, T, D), F32)] * 2 + [jax.ShapeDtypeStruct((B, nh, T, hw), BF16)] * 3,
        compiler_params=_cparams(("arbitrary", "arbitrary")),
        name="qkv",
    )(x, g1, w_in_bf, w_in_bf, w_in_bf, qg_row, kg_row, cos_t, sin_t, grp)


def _stack_halves(q, head_dim):
    lane = lax.broadcasted_iota(jnp.int32, q.shape, 1)
    zero = jnp.zeros_like(q)
    return jnp.concatenate([jnp.where(lane < head_dim, q, zero), jnp.where(lane >= head_dim, q, zero)], axis=0)


def _lambda_value(lam_ref, lam0):
    l = lam_ref[...]
    s1 = jnp.sum(l[0:1, :] * l[1:2, :], axis=-1, keepdims=True)
    s2 = jnp.sum(l[2:3, :] * l[3:4, :], axis=-1, keepdims=True)
    return jnp.exp(s1) - jnp.exp(s2) + lam0


def _attn_finish(acc, l, tq, lam, sg, lam0):
    o = acc[:tq] / l[:tq] - lam * (acc[tq:] / l[tq:])
    return _rmsnorm_rows(o, sg) * (1.0 - lam0)


def _attn_prompt_kernel(q_ref, k_ref, v_ref, lam_ref, sg_ref, o_ref, *, tq, head_dim, lam0):
    qi = pl.program_id(2)
    qs = _stack_halves(q_ref[...], head_dim)

    def block(j, carry, masked):
        m, l, acc = carry
        start = pl.multiple_of(j * tq, tq)
        kb = k_ref[pl.ds(start, tq), :]
        vb = v_ref[pl.ds(start, tq), :]
        s = _dot_nt(qs, kb)
        if masked:
            r = lax.broadcasted_iota(jnp.int32, s.shape, 0)
            c = lax.broadcasted_iota(jnp.int32, s.shape, 1)
            rr = jnp.where(r >= tq, r - tq, r)
            vis = (c >> CHUNK_SHIFT) <= (rr >> CHUNK_SHIFT)
            s = jnp.where(vis, s, -1e30)
        m_new = jnp.maximum(m, jnp.max(s, axis=-1, keepdims=True))
        alpha = jnp.exp(m - m_new)
        p = jnp.exp(s - m_new)
        l = alpha * l + jnp.sum(p, axis=-1, keepdims=True)
        acc = alpha * acc + _dot(p.astype(BF16), vb)
        return m_new, l, acc

    init = (jnp.full((2 * tq, 1), -1e30, F32), jnp.zeros((2 * tq, 1), F32),
            jnp.zeros((2 * tq, v_ref.shape[1]), F32))
    carry = lax.fori_loop(0, qi, lambda j, c: block(j, c, False), init)
    _, l, acc = block(qi, carry, True)
    o_ref[...] = _attn_finish(acc, l, tq, _lambda_value(lam_ref, lam0), sg_ref[...], lam0).astype(o_ref.dtype)


def _attn_prompt(qb, kb, vb, lam4, sg_row, n_heads, tq, lam0):
    B, n_heads_, T, hw = qb.shape
    assert n_heads_ == n_heads
    D = n_heads * hw
    head_dim = hw // 2
    assert tq % CHUNK == 0 and T % tq == 0
    return pl.pallas_call(
        functools.partial(_attn_prompt_kernel, tq=tq, head_dim=head_dim, lam0=lam0),
        grid=(B, n_heads, T // tq),
        in_specs=[pl.BlockSpec((None, None, tq, hw), lambda b, h, i: (b, h, i, 0)),
                  pl.BlockSpec((None, None, T, hw), lambda b, h, i: (b, h, 0, 0)),
                  pl.BlockSpec((None, None, T, hw), lambda b, h, i: (b, h, 0, 0)),
                  pl.BlockSpec(lam4.shape, lambda b, h, i: (0, 0)),
                  pl.BlockSpec((1, hw), lambda b, h, i: (0, 0))],
        out_specs=pl.BlockSpec((None, tq, hw), lambda b, h, i: (b, i, h)),
        out_shape=jax.ShapeDtypeStruct((B, T, D), BF16),
        compiler_params=_cparams(("arbitrary", "arbitrary", "arbitrary")),
        name="attn_prompt",
    )(qb, kb, vb, lam4, sg_row)


def _attn_sample_kernel(q_ref, kc_ref, vc_ref, kn_ref, vn_ref, lam_ref, sg_ref, o_ref, *, head_dim, lam0):
    ts = q_ref.shape[0]
    past = kc_ref.shape[0]
    qs = _stack_halves(q_ref[...], head_dim)
    sc = _dot_nt(qs, kc_ref[...].astype(BF16))
    sn = _dot_nt(qs, kn_ref[...])

    def visible(shape, k_off):
        r = lax.broadcasted_iota(jnp.int32, shape, 0)
        c = lax.broadcasted_iota(jnp.int32, shape, 1)
        q_pos = past + jnp.where(r >= ts, r - ts, r)
        return ((c + k_off) >> CHUNK_SHIFT) <= (q_pos >> CHUNK_SHIFT)

    sc = jnp.where(visible(sc.shape, 0), sc, -1e30)
    sn = jnp.where(visible(sn.shape, past), sn, -1e30)
    m = jnp.maximum(jnp.max(sc, axis=-1, keepdims=True), jnp.max(sn, axis=-1, keepdims=True))
    pc = jnp.exp(sc - m)
    pn = jnp.exp(sn - m)
    l = jnp.sum(pc, axis=-1, keepdims=True) + jnp.sum(pn, axis=-1, keepdims=True)
    acc = _dot(pc.astype(BF16), vc_ref[...].astype(BF16)) + _dot(pn.astype(BF16), vn_ref[...])
    o_ref[...] = _attn_finish(acc, l, ts, _lambda_value(lam_ref, lam0), sg_ref[...], lam0).astype(o_ref.dtype)


def _attn_sample(qb, kb, vb, cache_k2, cache_v2, lam4, sg_row, n_heads, lam0):
    B, _, Ts, hw = qb.shape
    D = n_heads * hw
    P = cache_k2.shape[1]
    new = pl.BlockSpec((None, None, Ts, hw), lambda b, h: (b, h, 0, 0))
    old = pl.BlockSpec((None, P, hw), lambda b, h: (b, 0, h))
    return pl.pallas_call(
        functools.partial(_attn_sample_kernel, head_dim=hw // 2, lam0=lam0),
        grid=(B, n_heads),
        in_specs=[new, old, old, new, new, pl.BlockSpec(lam4.shape, lambda b, h: (0, 0)),
                  pl.BlockSpec((1, hw), lambda b, h: (0, 0))],
        out_specs=pl.BlockSpec((None, Ts, hw), lambda b, h: (b, 0, h)),
        out_shape=jax.ShapeDtypeStruct((B, Ts, D), BF16),
        compiler_params=_cparams(("arbitrary", "arbitrary")),
        name="attn_sample",
    )(qb, cache_k2, cache_v2, kb, vb, lam4, sg_row)


def _merge_kernel(x_ref, gaa_ref, gb_ref, on_ref, wao_ref, wo_ref, g2_ref, x1_ref, xn2_ref):
    b = _dot(on_ref[...], wao_ref[...])
    mix = gaa_ref[...] + gb_ref[...] * b
    x1 = x_ref[...] + _dot(mix.astype(BF16), wo_ref[...])
    x1_ref[...] = x1
    xn2_ref[...] = _rmsnorm_rows(x1, g2_ref[...]).astype(BF16)


def _merge(x2, gaa2, gb2, on2, wao_bf, wo_bf, g2, tm):
    N, D = x2.shape
    tile = pl.BlockSpec((tm, D), lambda t: (t, 0))
    wfull = pl.BlockSpec((D, D), lambda t: (0, 0))
    return pl.pallas_call(
        _merge_kernel,
        grid=(N // tm,),
        in_specs=[tile, tile, tile, tile, wfull, wfull, pl.BlockSpec((1, D), lambda t: (0, 0))],
        out_specs=[tile, tile],
        out_shape=[jax.ShapeDtypeStruct((N, D), F32), jax.ShapeDtypeStruct((N, D), BF16)],
        compiler_params=_cparams(("arbitrary",)),
        name="merge",
    )(x2, gaa2, gb2, on2, wao_bf, wo_bf, g2)


def _staircase_pairs(topk):
    return [(a, b) for a in range(topk) for b in range(topk) if (a + 1) * (b + 1) <= topk]


def _peer_select_kernel(xn2_ref, wqt_ref, qg_ref, bd_ref, r1_ref, bs_ref, c0_ref, as_ref,
                        s_ref, work_ref, rank_ref, top_ref, idx_ref, tr_ref, *, nh, nk, topk):
    tm = xn2_ref.shape[0]
    dk = nk
    qp = _dot_nt(wqt_ref[...], xn2_ref[...])
    qg = qg_ref[...]

    for h in range(nh):
        lo = qp[h * dk:(h + 1) * dk, :]
        hi = qp[(nh + h) * dk:(nh + h + 1) * dk, :]
        ss = jnp.sum(lo * lo + hi * hi, axis=0, keepdims=True)
        r = lax.rsqrt(ss * (1.0 / (2 * dk)) + EPS)
        for c, part in ((0, lo), (1, hi)):
            rows = slice((c * nh + h) * dk, (c * nh + h + 1) * dk)
            s_ref[c, pl.ds(h * dk, dk), :] = (part * r) * qg[rows, :]

    for c in range(2):
        qn = s_ref[c].astype(BF16)
        s_ref[c] = _dot(bd_ref[c], qn)

    neg_inf = jnp.float32(-jnp.inf)
    n_chains = 8
    per_chain = nk // n_chains

    for c in range(2):
        work_ref[...] = s_ref[c].reshape(nk, nh, tm)
        if c == 1:
            rank_ref[...] = jnp.full((nk, nh, tm), float(topk), F32)

        def round_body(r, _, c=c):
            chains = []
            for g in range(n_chains):
                k0 = g * per_chain
                m = work_ref[k0]
                idx = jnp.full((nh, tm), k0, jnp.int32)
                for k in range(k0 + 1, k0 + per_chain):
                    w = work_ref[k]
                    gt = w > m
                    m = jnp.where(gt, w, m)
                    idx = jnp.where(gt, k, idx)
                chains.append((m, idx))
            while len(chains) > 1:
                merged = []
                for (ma, ia), (mb, ib) in zip(chains[0::2], chains[1::2]):
                    gt = mb > ma
                    merged.append((jnp.where(gt, mb, ma), jnp.where(gt, ib, ia)))
                chains = merged
            m, idx = chains[0]
            top_ref[c, pl.ds(r, 1)] = m[None]
            if c == 0:
                idx_ref[pl.ds(r, 1)] = idx[None]
            rf = lax.convert_element_type(r, F32)
            for k in range(nk):
                oh = idx == k
                work_ref[k] = jnp.where(oh, neg_inf, work_ref[k])
                if c == 1:
                    rank_ref[k] = jnp.where(oh, rf, rank_ref[k])
            return 0

        lax.fori_loop(0, topk, round_body, 0)

    A = [top_ref[0, a] for a in range(topk)]
    Bv = [top_ref[1, b] for b in range(topk)]

    pairs = _staircase_pairs(topk)
    cand = [A[a] + Bv[b] for (a, b) in pairs]
    n = len(pairs)
    beaten = [jnp.zeros((nh, tm), F32) for _ in range(n)]
    for p in range(n):
        ap, bp = pairs[p]
        for q in range(p + 1, n):
            aq, bq = pairs[q]
            if ap <= aq and bp <= bq:
                beaten[q] = beaten[q] + 1.0
            else:
                t = jnp.where(cand[p] >= cand[q], 1.0, 0.0)
                beaten[q] = beaten[q] + t
                beaten[p] = beaten[p] + (1.0 - t)
    sel = [jnp.where(beaten[p] < float(topk), 1.0, 0.0) for p in range(n)]

    ea = [jnp.exp(A[a] - A[0]) for a in range(topk)]
    eb = [jnp.exp(Bv[b] - Bv[0]) for b in range(topk)]
    z = jnp.zeros((nh, tm), F32)
    cnt = [jnp.zeros((nh, tm), F32) for _ in range(topk)]
    for p, (a, b) in enumerate(pairs):
        z = z + sel[p] * (ea[a] * eb[b])
        cnt[a] = cnt[a] + sel[p]
    inv_z = 1.0 / z

    def emit(out_ref, val):
        val = val.reshape(nk * nh, tm)
        for lb in range(tm // LANES):
            out_ref[lb] = val[:, lb * LANES:(lb + 1) * LANES]

    def emit_by_head(out_ref, val):
        emit(tr_ref, val)
        for lb in range(tm // LANES):
            for h in range(nh):
                out_ref[lb, h] = tr_ref[lb, pl.ds(h, nk, stride=nh), :].astype(BF16)

    s0 = s_ref[0].reshape(nk, nh, tm)
    s1 = s_ref[1].reshape(nk, nh, tm)
    emit(as_ref, jnp.exp(s0 - A[0][None]) * (0.5 * inv_z)[None])
    emit_by_head(bs_ref, jnp.exp(s1 - Bv[0][None]))
    kio = lax.broadcasted_iota(jnp.int32, (nk, nh, tm), 0)
    c0 = jnp.zeros((nk, nh, tm), F32)
    for a in range(topk):
        c0 = jnp.where(kio == idx_ref[a][None], cnt[a][None], c0)
    emit(c0_ref, c0)
    emit_by_head(r1_ref, rank_ref[...])


def _peer_select(xn2, wqt_bf, qg_rows, bd_bf, nh, nk, tm):
    N, D = xn2.shape
    R = nk * nh
    assert tm % LANES == 0
    nlb = tm // LANES
    by_row = pl.BlockSpec((nlb, R, LANES), lambda t: (t, 0, 0))
    by_head = pl.BlockSpec((nlb, nh, nk, LANES), lambda t: (t, 0, 0, 0))
    return pl.pallas_call(
        functools.partial(_peer_select_kernel, nh=nh, nk=nk, topk=PEER_TOPK),
        grid=(N // tm,),
        in_specs=[pl.BlockSpec((tm, D), lambda t: (t, 0)),
                  pl.BlockSpec(wqt_bf.shape, lambda t: (0, 0)),
                  pl.BlockSpec(qg_rows.shape, lambda t: (0, 0)),
                  pl.BlockSpec(bd_bf.shape, lambda t: (0, 0, 0))],
        out_specs=[by_head, by_head, by_row, by_row],
        out_shape=[jax.ShapeDtypeStruct((N // LANES, nh, nk, LANES), BF16)] * 2
        + [jax.ShapeDtypeStruct((N // LANES, R, LANES), F32)] * 2,
        scratch_shapes=[pltpu.VMEM((2, R, tm), F32), pltpu.VMEM((nk, nh, tm), F32),
                        pltpu.VMEM((nk, nh, tm), F32), pltpu.VMEM((2, PEER_TOPK, nh, tm), F32),
                        pltpu.VMEM((PEER_TOPK, nh, tm), jnp.int32), pltpu.VMEM((nlb, R, LANES), F32)],
        compiler_params=_cparams(("arbitrary",)),
        name="peer_select",
    )(xn2, wqt_bf, qg_rows, bd_bf)


def _gelu_twice(x):
    return x + x * lax.erf(x * (2.0 ** -0.5))


BF16_SUBLANE_ROWS = 16


def _bcast_rows_bf16(row, n):
    one = jnp.broadcast_to(row, (BF16_SUBLANE_ROWS, row.shape[1])).astype(BF16)
    return jnp.tile(one, (n // BF16_SUBLANE_ROWS, 1))


def _peer_main_kernel(xn2_ref, x1_ref, u_ref, vt_ref, r1_ref, bs_ref, c0_ref, as_ref, y_ref,
                      xt_ref, acc_ref, gl_a, gl_b, gt_a, gt_b, *, nh, nk, ne, n_blocks):
    s = pl.program_id(0)
    te = u_ref.shape[0]
    tm = xn2_ref.shape[0]
    keys_per_block = te // nk
    e1 = s % ne
    e2 = (s + ne - 1) % ne
    e3 = (s + 2 * ne - 2) % ne

    @pl.when(s == 0)
    def _():
        gl_b[...] = jnp.zeros_like(gl_b)
        gt_a[...] = jnp.zeros_like(gt_a)
        acc_ref[...] = jnp.zeros_like(acc_ref)

    @pl.when((e1 == 0) & (s < n_blocks))
    def _():
        xt_ref[...] = xn2_ref[...].astype(F32).T.astype(BF16)

    @pl.when((e3 == 0) & (s >= 2))
    def _():
        acc_ref[...] = jnp.zeros_like(acc_ref)

    zero = jnp.zeros((nk, LANES), BF16)
    one = jnp.ones((nk, LANES), BF16)

    keys_per_piece = 1
    n_pieces = keys_per_block // keys_per_piece

    def gate_piece(e, piece, lb):
        rows = []
        for ii in range(piece * keys_per_piece, (piece + 1) * keys_per_piece):
            base = pl.multiple_of((e * keys_per_block + ii) * nh, nh)
            rows.append((c0_ref[lb, pl.ds(base, nh), :], as_ref[lb, pl.ds(base, nh), :]))
        w = [zero] * keys_per_piece
        for h in range(nh):
            slab = pl.ds((lb * nh + h) * nk, nk)
            r1 = r1_ref[slab, :]
            bs = bs_ref[slab, :]
            for k, (crow, arow) in enumerate(rows):
                cb = _bcast_rows_bf16(crow[h:h + 1, :], nk)
                ab = _bcast_rows_bf16(arow[h:h + 1, :], nk)
                below = jnp.minimum(jnp.maximum(cb - r1, zero), one)
                w[k] = w[k] + (below * bs) * ab
        return w

    def step(gl_new, gl_old, gt_new, gt_old):
        D = vt_ref.shape[0]
        nlb = tm // LANES
        n_mxu = te // MXU_PIECE_ROWS
        out_rows = D // n_mxu

        def stage1(ii):
            rows = slice(ii * MXU_PIECE_ROWS, (ii + 1) * MXU_PIECE_ROWS)
            gl_new[rows, :] = _gelu_twice(_dot(u_ref[rows, :], xt_ref[...])).astype(BF16)

        def stage2(piece, lb):
            cols = slice(lb * LANES, (lb + 1) * LANES)
            for k, w in enumerate(gate_piece(e2, piece, lb)):
                ii = piece * keys_per_piece + k
                rows = slice(ii * nk, (ii + 1) * nk)
                gt_new[rows, cols] = gl_old[rows, cols] * w

        def stage3(ii):
            orow = slice(ii * out_rows, (ii + 1) * out_rows)
            acc_ref[orow, :] += _dot(vt_ref[orow, :], gt_old[...])

        mxu_work = []
        for ii in range(n_mxu):
            mxu_work += [functools.partial(stage1, ii), functools.partial(stage3, ii)]
        vpu_work = [functools.partial(stage2, piece, lb) for piece in range(n_pieces) for lb in range(nlb)]
        order = sorted([((i + 0.5) / len(mxu_work), 0, f) for i, f in enumerate(mxu_work)]
                       + [((i + 0.5) / len(vpu_work), 1, f) for i, f in enumerate(vpu_work)], key=lambda t: t[:2])
        for _, _, piece_fn in order:
            piece_fn()

    @pl.when(s % 2 == 0)
    def _():
        step(gl_a, gl_b, gt_b, gt_a)

    @pl.when(s % 2 == 1)
    def _():
        step(gl_b, gl_a, gt_a, gt_b)

    @pl.when((e3 == ne - 1) & (s >= 2))
    def _():
        y_ref[...] = x1_ref[...] + acc_ref[...].T


def _peer_main(xn2, x1, u_bf, vt_bf, r1, bs, c0, a_s, nh, nk, tm):
    N, D = xn2.shape
    ne, _, te = vt_bf.shape
    R = nk * nh
    n_blocks = (N // tm) * ne
    last = n_blocks - 1
    nlb = tm // LANES
    tile1 = lambda s: jnp.minimum(s, last) // ne
    tile2 = lambda s: jnp.clip(s - 1, 0, last) // ne
    tile3 = lambda s: jnp.clip(s - 2, 0, last) // ne
    by_row = pl.BlockSpec((nlb, R, LANES), lambda s: (tile2(s), 0, 0))
    by_head = pl.BlockSpec((nlb * nh * nk, LANES), lambda s: (tile2(s), 0))
    r1 = r1.reshape(-1, LANES)
    bs = bs.reshape(-1, LANES)
    return pl.pallas_call(
        functools.partial(_peer_main_kernel, nh=nh, nk=nk, ne=ne, n_blocks=n_blocks),
        grid=(n_blocks + 2,),
        in_specs=[pl.BlockSpec((tm, D), lambda s: (tile1(s), 0)),
                  pl.BlockSpec((tm, D), lambda s: (tile3(s), 0)),
                  pl.BlockSpec((te, D), lambda s: (s % ne, 0)),
                  pl.BlockSpec((None, D, te), lambda s: ((s + 2 * ne - 2) % ne, 0, 0)),
                  by_head, by_head, by_row, by_row],
        out_specs=pl.BlockSpec((tm, D), lambda s: (tile3(s), 0)),
        out_shape=jax.ShapeDtypeStruct((N, D), F32),
        scratch_shapes=[pltpu.VMEM((D, tm), BF16), pltpu.VMEM((D, tm), F32)]
        + [pltpu.VMEM((te, tm), BF16)] * 4,
        compiler_params=_cparams(("arbitrary",)),
        name="peer_main",
    )(xn2, x1, u_bf, vt_bf, r1, bs, c0, a_s)


def _rope_tables(pos, head_dim, width):
    half = head_dim // 2
    inv = ROPE_THETA ** (-jnp.arange(half, dtype=F32) / half)
    ang = pos.astype(F32)[:, None] * inv[None, :]
    cos = jnp.cos(ang)
    sin = jnp.sin(ang)
    reps = width // head_dim
    cos_t = jnp.tile(jnp.concatenate([cos, cos], axis=1), (1, reps))
    sin_t = jnp.tile(jnp.concatenate([-sin, sin], axis=1), (1, reps))
    return cos_t, sin_t


def _token_tile(n, pref):
    t = min(pref, n)
    while n % t:
        t //= 2
    return t


def _trunk(x, pos, conv_state, cache_k2, cache_v2, p):
    B, T, D = x.shape
    nh, hd = p["n_heads"], p["head_dim"]
    tm = _token_tile(T, 256)
    cos_t, sin_t = _rope_tables(pos, hd, 128)
    gaa, gb, new_conv = _branch_a(x, p["g1"], p["w_in"], p["b_gate"], p["conv_w"], conv_state, p["wco"], tm)
    k, v, qb, kb, vb = _qkv(x, p["g1"], p["w_in"], p["qg"], p["kg"], cos_t, sin_t, p["grp"], tm, hd)
    if cache_k2 is None:
        on = _attn_prompt(qb, kb, vb, p["lam4"], p["sg"], nh, _token_tile(T, 512), p["lam0"])
    else:
        on = _attn_sample(qb, kb, vb, cache_k2, cache_v2, p["lam4"], p["sg"], nh, p["lam0"])
    N = B * T
    tn = _token_tile(N, 256)
    x1, xn2 = _merge(x.reshape(N, D), gaa.reshape(N, D), gb.reshape(N, D), on.reshape(N, D),
                     p["wao"], p["wo"], p["g2"], tn)
    pnh, nk = p["peer_heads"], p["n_keys"]
    tp = _token_tile(N, 512)
    qg_rows = jnp.broadcast_to(p["peer_qg_col"], (p["peer_qg_col"].shape[0], tp))
    r1, bs, c0, a_s = _peer_select(xn2, p["wqt"], qg_rows, p["bd"], pnh, nk, tp)
    y = _peer_main(xn2, x1, p["u"], p["vt"], r1, bs, c0, a_s, pnh, nk, tp)
    return y.reshape(B, T, D), k, v, new_conv


def kernel(x_prompt, x_sample, cache_k, cache_v, state_conv, norm1_g, w_in, b_gate, conv_w, q_norm_g, k_norm_g,
           lam_q1, lam_k1, lam_q2, lam_k2, subln_g, w_conv_out, w_attn_out, w_o, norm2_g, peer_wq, peer_q_g,
           peer_subkeys, peer_u, peer_v):
    depth = w_in.shape[0]
    assert depth == 1
    l = LAYER_IDX
    D = x_prompt.shape[-1]
    n_heads, head_dim, v_dim = cache_k.shape[3], cache_k.shape[5], cache_v.shape[4]
    assert v_dim == 2 * head_dim and n_heads * v_dim == D and w_in.shape[2] == 8 * D
    n_keys, dk = peer_subkeys.shape[2], peer_subkeys.shape[3]
    assert n_keys == dk
    peer_heads = peer_wq.shape[2] // (2 * dk)
    past = cache_k.shape[2]
    assert past % CHUNK == 0 and (1 << CHUNK_SHIFT) == CHUNK

    eye_h = jnp.eye(peer_heads, dtype=F32)
    grp = np.kron(np.eye(256 // head_dim), np.ones((head_dim, head_dim))).astype(np.float32)
    p = dict(
        n_heads=n_heads, head_dim=head_dim, peer_heads=peer_heads, n_keys=n_keys, lam0=_lambda_init(l),
        g1=norm1_g[l][None, :], w_in=w_in[l].astype(BF16), b_gate=b_gate[l], conv_w=conv_w[l],
        wco=w_conv_out[l].astype(BF16), wao=w_attn_out[l].astype(BF16), wo=w_o[l].astype(BF16),
        qg=jnp.tile(q_norm_g[l], D // head_dim)[None, :], kg=jnp.tile(k_norm_g[l], D // head_dim)[None, :],
        grp=jnp.asarray(grp, BF16),
        lam4=jnp.stack([lam_q1[l], lam_k1[l], lam_q2[l], lam_k2[l]]),
        sg=subln_g[l][None, :], g2=norm2_g[l][None, :],
        wqt=peer_wq[l].reshape(D, peer_heads, 2, dk).transpose(2, 1, 3, 0).reshape(2 * peer_heads * dk, D).astype(BF16),
        peer_qg_col=jnp.broadcast_to(peer_q_g[l].reshape(2, 1, dk), (2, peer_heads, dk)).reshape(-1, 1),
        bd=jnp.einsum("ckd,hg->ckhgd", peer_subkeys[l], eye_h).reshape(2, n_keys * peer_heads, peer_heads * dk).astype(BF16),
        u=peer_u[l].astype(BF16),
        vt=peer_v[l].reshape(-1, PEER_EXPERT_BLOCK, D).transpose(0, 2, 1).astype(BF16),
    )

    B, T, _ = x_prompt.shape
    Bs, Ts, _ = x_sample.shape
    pos_p = jnp.arange(T, dtype=jnp.int32)
    pos_s = past + jnp.arange(Ts, dtype=jnp.int32)
    zero_conv = jnp.zeros((B, 2, D), x_prompt.dtype)
    y_p, k_p, v_p, c_p = _trunk(x_prompt, pos_p, zero_conv, None, None, p)
    y_s, k_s, v_s, c_s = _trunk(x_sample, pos_s, state_conv[l], cache_k[l].reshape(Bs, past, D),
                                cache_v[l].reshape(Bs, past, D), p)
    return (y_p, y_s,
            k_p.reshape(1, B, T, n_heads, 2, head_dim), v_p.reshape(1, B, T, n_heads, v_dim), c_p[None],
            k_s.reshape(1, Bs, Ts, n_heads, 2, head_dim), v_s.reshape(1, Bs, Ts, n_heads, v_dim), c_s[None])
```

```python
import functools
import math

import jax
import jax.numpy as jnp
import numpy as np
from jax import lax
from jax.experimental import pallas as pl
from jax.experimental.pallas import tpu as pltpu

EPS = 1e-6
CHUNK = 64
CHUNK_SHIFT = 6
ROPE_THETA = 10000.0
PEER_TOPK = 16
LAYER_IDX = 0

V7X_VMEM_LIMIT = 56 * 1024 * 1024
LANES = 128
PEER_EXPERT_BLOCK = 2048
MXU_PIECE_ROWS = 2048

F32 = jnp.float32
BF16 = jnp.bfloat16


def _lambda_init(layer_idx):
    return 0.8 - 0.6 * math.exp(-0.3 * layer_idx)


def _dot(a, b):
    return jnp.dot(a, b, preferred_element_type=F32)


def _dot_nt(a, b):
    return lax.dot_general(a, b, (((1,), (1,)), ((), ())), preferred_element_type=F32)


def _rmsnorm_rows(x, g):
    r = lax.rsqrt(jnp.mean(x * x, axis=-1, keepdims=True) + EPS)
    return (x * r) * g


def _cparams(sem):
    return pltpu.CompilerParams(dimension_semantics=sem, vmem_limit_bytes=V7X_VMEM_LIMIT)


def _branch_a_kernel(x_ref, g1_ref, wb_ref, wc_ref, wh_ref, wga_ref, wgb_ref, bg_ref, cw_ref, cs_ref,
                     wco_ref, gaa_ref, gb_ref, nc_ref, carry_ref):
    t = pl.program_id(1)

    @pl.when(t == 0)
    def _():
        carry_ref[...] = cs_ref[...]

    x = x_ref[...]
    xn = _rmsnorm_rows(x, g1_ref[...]).astype(BF16)
    hb = _dot(xn, wb_ref[...])
    hc = _dot(xn, wc_ref[...])
    hh = _dot(xn, wh_ref[...])
    u = hc * hh
    tm = u.shape[0]
    row = lax.broadcasted_iota(jnp.int32, u.shape, 0)
    c0 = carry_ref[0:1, :]
    c1 = carry_ref[1:2, :]
    u1 = jnp.where(row == 0, c1, pltpu.roll(u, 1, 0))
    u2 = jnp.where(row == 0, c0, jnp.where(row == 1, c1, pltpu.roll(u, 2, 0)))
    y = cw_ref[0:1, :] * u2 + cw_ref[1:2, :] * u1 + cw_ref[2:3, :] * u
    conv_y = hb * y
    new_c = u[tm - 2:tm, :]
    carry_ref[...] = new_c
    nc_ref[...] = new_c
    a = _dot(conv_y.astype(BF16), wco_ref[...])
    ga = jax.nn.sigmoid(_dot(xn, wga_ref[...]) + bg_ref[0:1, :])
    gb = jax.nn.sigmoid(_dot(xn, wgb_ref[...]) + bg_ref[1:2, :])
    gaa_ref[...] = ga * a
    gb_ref[...] = gb


def _branch_a(x, g1, w_in_bf, b_gate, conv_w, conv_state, wco_bf, tm):
    B, T, D = x.shape
    nt = T // tm
    wspec = lambda j: pl.BlockSpec((D, D), lambda b, t, j=j: (0, j))
    tile = pl.BlockSpec((None, tm, D), lambda b, t: (b, t, 0))
    full2 = lambda r: pl.BlockSpec((r, D), lambda b, t: (0, 0))
    return pl.pallas_call(
        _branch_a_kernel,
        grid=(B, nt),
        in_specs=[tile, full2(1), wspec(0), wspec(1), wspec(2), wspec(6), wspec(7), full2(2), full2(3),
                  pl.BlockSpec((None, 2, D), lambda b, t: (b, 0, 0)), pl.BlockSpec((D, D), lambda b, t: (0, 0))],
        out_specs=[tile, tile, pl.BlockSpec((None, 2, D), lambda b, t: (b, 0, 0))],
        out_shape=[jax.ShapeDtypeStruct((B, T, D), F32), jax.ShapeDtypeStruct((B, T, D), F32),
                   jax.ShapeDtypeStruct((B, 2, D), F32)],
        scratch_shapes=[pltpu.VMEM((2, D), F32)],
        compiler_params=_cparams(("arbitrary", "arbitrary")),
        name="branch_a",
    )(x, g1, w_in_bf, w_in_bf, w_in_bf, w_in_bf, w_in_bf, b_gate, conv_w, conv_state, wco_bf)


def _qkv_kernel(x_ref, g1_ref, wq_ref, wk_ref, wv_ref, qg_ref, kg_ref, cos_ref, sin_ref, grp_ref,
                k_ref, v_ref, qb_ref, kb_ref, vb_ref, *, head_dim):
    x = x_ref[...]
    xn = _rmsnorm_rows(x, g1_ref[...]).astype(BF16)
    q = _dot(xn, wq_ref[...])
    k = _dot(xn, wk_ref[...])
    v = _dot(xn, wv_ref[...])
    D = q.shape[1]
    half = head_dim // 2
    reps = D // cos_ref.shape[1]
    cos = jnp.tile(cos_ref[...], (1, reps))
    sin = jnp.tile(sin_ref[...], (1, reps))
    lane = lax.broadcasted_iota(jnp.int32, q.shape, 1)
    first_half = (lane & half) == 0
    grp = grp_ref[...]
    gw = grp.shape[0]

    def headnorm_rope(z, g):
        z2 = z * z
        hi = z2.astype(BF16)
        lo = (z2 - hi.astype(F32)).astype(BF16)
        parts = []
        for s in range(D // gw):
            sl = slice(s * gw, (s + 1) * gw)
            parts.append(_dot(hi[:, sl], grp) + _dot(lo[:, sl], grp))
        ss = jnp.concatenate(parts, axis=1)
        r = lax.rsqrt(ss * (1.0 / head_dim) + EPS)
        zn = (z * r) * g
        sw = jnp.where(first_half, pltpu.roll(zn, D - half, 1), pltpu.roll(zn, half, 1))
        return zn * cos + sw * sin

    qr = headnorm_rope(q, qg_ref[...])
    kr = headnorm_rope(k, kg_ref[...])
    k_ref[...] = kr
    v_ref[...] = v
    hw = 2 * head_dim
    for dst, val in ((qb_ref, qr * (head_dim ** -0.5)), (kb_ref, kr), (vb_ref, v)):
        val = val.astype(BF16)
        for h in range(D // hw):
            dst[h] = val[:, h * hw:(h + 1) * hw]


def _qkv(x, g1, w_in_bf, qg_row, kg_row, cos_t, sin_t, grp, tm, head_dim):
    B, T, D = x.shape
    nt = T // tm
    hw = 2 * head_dim
    nh = D // hw
    wspec = lambda j: pl.BlockSpec((D, D), lambda b, t, j=j: (0, j))
    tile = pl.BlockSpec((None, tm, D), lambda b, t: (b, t, 0))
    row = pl.BlockSpec((1, D), lambda b, t: (0, 0))
    tab = pl.BlockSpec((tm, cos_t.shape[1]), lambda b, t: (t, 0))
    return pl.pallas_call(
        functools.partial(_qkv_kernel, head_dim=head_dim),
        grid=(B, nt),
        in_specs=[tile, row, wspec(3), wspec(4), wspec(5), row, row, tab, tab,
                  pl.BlockSpec(grp.shape, lambda b, t: (0, 0))],
        out_specs=[tile] * 2 + [pl.BlockSpec((None, nh, tm, hw), lambda b, t: (b, 0, t, 0))] * 3,
        out_shape=[jax.ShapeDtypeStruct((B, T, D), F32)] * 2 + [jax.ShapeDtypeStruct((B, nh, T, hw), BF16)] * 3,
        compiler_params=_cparams(("arbitrary", "arbitrary")),
        name="qkv",
    )(x, g1, w_in_bf, w_in_bf, w_in_bf, qg_row, kg_row, cos_t, sin_t, grp)


def _stack_halves(q, head_dim):
    lane = lax.broadcasted_iota(jnp.int32, q.shape, 1)
    zero = jnp.zeros_like(q)
    return jnp.concatenate([jnp.where(lane < head_dim, q, zero), jnp.where(lane >= head_dim, q, zero)], axis=0)


def _lambda_value(lam_ref, lam0):
    l = lam_ref[...]
    s1 = jnp.sum(l[0:1, :] * l[1:2, :], axis=-1, keepdims=True)
    s2 = jnp.sum(l[2:3, :] * l[3:4, :], axis=-1, keepdims=True)
    return jnp.exp(s1) - jnp.exp(s2) + lam0


def _attn_finish(acc, l, tq, lam, sg, lam0):
    o = acc[:tq] / l[:tq] - lam * (acc[tq:] / l[tq:])
    return _rmsnorm_rows(o, sg) * (1.0 - lam0)


def _attn_prompt_kernel(q_ref, k_ref, v_ref, lam_ref, sg_ref, o_ref, *, tq, head_dim, lam0):
    qi = pl.program_id(2)
    qs = _stack_halves(q_ref[...], head_dim)

    def block(j, carry, masked):
        m, l, acc = carry
        start = pl.multiple_of(j * tq, tq)
        kb = k_ref[pl.ds(start, tq), :]
        vb = v_ref[pl.ds(start, tq), :]
        s = _dot_nt(qs, kb)
        if masked:
            r = lax.broadcasted_iota(jnp.int32, s.shape, 0)
            c = lax.broadcasted_iota(jnp.int32, s.shape, 1)
            rr = jnp.where(r >= tq, r - tq, r)
            vis = (c >> CHUNK_SHIFT) <= (rr >> CHUNK_SHIFT)
            s = jnp.where(vis, s, -1e30)
        m_new = jnp.maximum(m, jnp.max(s, axis=-1, keepdims=True))
        alpha = jnp.exp(m - m_new)
        p = jnp.exp(s - m_new)
        l = alpha * l + jnp.sum(p, axis=-1, keepdims=True)
        acc = alpha * acc + _dot(p.astype(BF16), vb)
        return m_new, l, acc

    init = (jnp.full((2 * tq, 1), -1e30, F32), jnp.zeros((2 * tq, 1), F32),
            jnp.zeros((2 * tq, v_ref.shape[1]), F32))
    carry = lax.fori_loop(0, qi, lambda j, c: block(j, c, False), init)
    _, l, acc = block(qi, carry, True)
    o_ref[...] = _attn_finish(acc, l, tq, _lambda_value(lam_ref, lam0), sg_ref[...], lam0).astype(o_ref.dtype)


def _attn_prompt(qb, kb, vb, lam4, sg_row, n_heads, tq, lam0):
    B, n_heads_, T, hw = qb.shape
    assert n_heads_ == n_heads
    D = n_heads * hw
    head_dim = hw // 2
    assert tq % CHUNK == 0 and T % tq == 0
    return pl.pallas_call(
        functools.partial(_attn_prompt_kernel, tq=tq, head_dim=head_dim, lam0=lam0),
        grid=(B, n_heads, T // tq),
        in_specs=[pl.BlockSpec((None, None, tq, hw), lambda b, h, i: (b, h, i, 0)),
                  pl.BlockSpec((None, None, T, hw), lambda b, h, i: (b, h, 0, 0)),
                  pl.BlockSpec((None, None, T, hw), lambda b, h, i: (b, h, 0, 0)),
                  pl.BlockSpec(lam4.shape, lambda b, h, i: (0, 0)),
                  pl.BlockSpec((1, hw), lambda b, h, i: (0, 0))],
        out_specs=pl.BlockSpec((None, tq, hw), lambda b, h, i: (b, i, h)),
        out_shape=jax.ShapeDtypeStruct((B, T, D), BF16),
        compiler_params=_cparams(("arbitrary", "arbitrary", "arbitrary")),
        name="attn_prompt",
    )(qb, kb, vb, lam4, sg_row)


def _attn_sample_kernel(q_ref, kc_ref, vc_ref, kn_ref, vn_ref, lam_ref, sg_ref, o_ref, *, head_dim, lam0):
    ts = q_ref.shape[0]
    past = kc_ref.shape[0]
    qs = _stack_halves(q_ref[...], head_dim)
    sc = _dot_nt(qs, kc_ref[...].astype(BF16))
    sn = _dot_nt(qs, kn_ref[...])

    def visible(shape, k_off):
        r = lax.broadcasted_iota(jnp.int32, shape, 0)
        c = lax.broadcasted_iota(jnp.int32, shape, 1)
        q_pos = past + jnp.where(r >= ts, r - ts, r)
        return ((c + k_off) >> CHUNK_SHIFT) <= (q_pos >> CHUNK_SHIFT)

    sc = jnp.where(visible(sc.shape, 0), sc, -1e30)
    sn = jnp.where(visible(sn.shape, past), sn, -1e30)
    m = jnp.maximum(jnp.max(sc, axis=-1, keepdims=True), jnp.max(sn, axis=-1, keepdims=True))
    pc = jnp.exp(sc - m)
    pn = jnp.exp(sn - m)
    l = jnp.sum(pc, axis=-1, keepdims=True) + jnp.sum(pn, axis=-1, keepdims=True)
    acc = _dot(pc.astype(BF16), vc_ref[...].astype(BF16)) + _dot(pn.astype(BF16), vn_ref[...])
    o_ref[...] = _attn_finish(acc, l, ts, _lambda_value(lam_ref, lam0), sg_ref[...], lam0).astype(o_ref.dtype)


def _attn_sample(qb, kb, vb, cache_k2, cache_v2, lam4, sg_row, n_heads, lam0):
    B, _, Ts, hw = qb.shape
    D = n_heads * hw
    P = cache_k2.shape[1]
    new = pl.BlockSpec((None, None, Ts, hw), lambda b, h: (b, h, 0, 0))
    old = pl.BlockSpec((None, P, hw), lambda b, h: (b, 0, h))
    return pl.pallas_call(
        functools.partial(_attn_sample_kernel, head_dim=hw // 2, lam0=lam0),
        grid=(B, n_heads),
        in_specs=[new, old, old, new, new, pl.BlockSpec(lam4.shape, lambda b, h: (0, 0)),
                  pl.BlockSpec((1, hw), lambda b, h: (0, 0))],
        out_specs=pl.BlockSpec((None, Ts, hw), lambda b, h: (b, 0, h)),
        out_shape=jax.ShapeDtypeStruct((B, Ts, D), BF16),
        compiler_params=_cparams(("arbitrary", "arbitrary")),
        name="attn_sample",
    )(qb, cache_k2, cache_v2, kb, vb, lam4, sg_row)


def _merge_kernel(x_ref, gaa_ref, gb_ref, on_ref, wao_ref, wo_ref, g2_ref, x1_ref, xn2_ref):
    b = _dot(on_ref[...], wao_ref[...])
    mix = gaa_ref[...] + gb_ref[...] * b
    x1 = x_ref[...] + _dot(mix.astype(BF16), wo_ref[...])
    x1_ref[...] = x1
    xn2_ref[...] = _rmsnorm_rows(x1, g2_ref[...]).astype(BF16)


def _merge(x2, gaa2, gb2, on2, wao_bf, wo_bf, g2, tm):
    N, D = x2.shape
    tile = pl.BlockSpec((tm, D), lambda t: (t, 0))
    wfull = pl.BlockSpec((D, D), lambda t: (0, 0))
    return pl.pallas_call(
        _merge_kernel,
        grid=(N // tm,),
        in_specs=[tile, tile, tile, tile, wfull, wfull, pl.BlockSpec((1, D), lambda t: (0, 0))],
        out_specs=[tile, tile],
        out_shape=[jax.ShapeDtypeStruct((N, D), F32), jax.ShapeDtypeStruct((N, D), BF16)],
        compiler_params=_cparams(("arbitrary",)),
        name="merge",
    )(x2, gaa2, gb2, on2, wao_bf, wo_bf, g2)


def _staircase_pairs(topk):
    return [(a, b) for a in range(topk) for b in range(topk) if (a + 1) * (b + 1) <= topk]


def _peer_select_kernel(xn2_ref, wqt_ref, qg_ref, bd_ref, r1_ref, bs_ref, c0_ref, as_ref,
                        s_ref, work_ref, rank_ref, top_ref, idx_ref, tr_ref, *, nh, nk, topk):
    tm = xn2_ref.shape[0]
    dk = nk
    qp = _dot_nt(wqt_ref[...], xn2_ref[...])
    qg = qg_ref[...]

    for h in range(nh):
        lo = qp[h * dk:(h + 1) * dk, :]
        hi = qp[(nh + h) * dk:(nh + h + 1) * dk, :]
        ss = jnp.sum(lo * lo + hi * hi, axis=0, keepdims=True)
        r = lax.rsqrt(ss * (1.0 / (2 * dk)) + EPS)
        for c, part in ((0, lo), (1, hi)):
            rows = slice((c * nh + h) * dk, (c * nh + h + 1) * dk)
            s_ref[c, pl.ds(h * dk, dk), :] = (part * r) * qg[rows, :]

    for c in range(2):
        qn = s_ref[c].astype(BF16)
        s_ref[c] = _dot(bd_ref[c], qn)

    neg_inf = jnp.float32(-jnp.inf)
    n_chains = 8
    per_chain = nk // n_chains

    for c in range(2):
        work_ref[...] = s_ref[c].reshape(nk, nh, tm)
        if c == 1:
            rank_ref[...] = jnp.full((nk, nh, tm), float(topk), F32)

        def round_body(r, _, c=c):
            chains = []
            for g in range(n_chains):
                k0 = g * per_chain
                m = work_ref[k0]
                idx = jnp.full((nh, tm), k0, jnp.int32)
                for k in range(k0 + 1, k0 + per_chain):
                    w = work_ref[k]
                    gt = w > m
                    m = jnp.where(gt, w, m)
                    idx = jnp.where(gt, k, idx)
                chains.append((m, idx))
            while len(chains) > 1:
                merged = []
                for (ma, ia), (mb, ib) in zip(chains[0::2], chains[1::2]):
                    gt = mb > ma
                    merged.append((jnp.where(gt, mb, ma), jnp.where(gt, ib, ia)))
                chains = merged
            m, idx = chains[0]
            top_ref[c, pl.ds(r, 1)] = m[None]
            if c == 0:
                idx_ref[pl.ds(r, 1)] = idx[None]
            rf = lax.convert_element_type(r, F32)
            for k in range(nk):
                oh = idx == k
                work_ref[k] = jnp.where(oh, neg_inf, work_ref[k])
                if c == 1:
                    rank_ref[k] = jnp.where(oh, rf, rank_ref[k])
            return 0

        lax.fori_loop(0, topk, round_body, 0)

    A = [top_ref[0, a] for a in range(topk)]
    Bv = [top_ref[1, b] for b in range(topk)]

    pairs = _staircase_pairs(topk)
    cand = [A[a] + Bv[b] for (a, b) in pairs]
    n = len(pairs)
    beaten = [jnp.zeros((nh, tm), F32) for _ in range(n)]
    for p in range(n):
        ap, bp = pairs[p]
        for q in range(p + 1, n):
            aq, bq = pairs[q]
            if ap <= aq and bp <= bq:
                beaten[q] = beaten[q] + 1.0
            else:
                t = jnp.where(cand[p] >= cand[q], 1.0, 0.0)
                beaten[q] = beaten[q] + t
                beaten[p] = beaten[p] + (1.0 - t)
    sel = [jnp.where(beaten[p] < float(topk), 1.0, 0.0) for p in range(n)]

    ea = [jnp.exp(A[a] - A[0]) for a in range(topk)]
    eb = [jnp.exp(Bv[b] - Bv[0]) for b in range(topk)]
    z = jnp.zeros((nh, tm), F32)
    cnt = [jnp.zeros((nh, tm), F32) for _ in range(topk)]
    for p, (a, b) in enumerate(pairs):
        z = z + sel[p] * (ea[a] * eb[b])
        cnt[a] = cnt[a] + sel[p]
    inv_z = 1.0 / z

    def emit(out_ref, val):
        val = val.reshape(nk * nh, tm)
        for lb in range(tm // LANES):
            out_ref[lb] = val[:, lb * LANES:(lb + 1) * LANES]

    def emit_by_head(out_ref, val):
        emit(tr_ref, val)
        for lb in range(tm // LANES):
            for h in range(nh):
                out_ref[lb, h] = tr_ref[lb, pl.ds(h, nk, stride=nh), :].astype(BF16)

    s0 = s_ref[0].reshape(nk, nh, tm)
    s1 = s_ref[1].reshape(nk, nh, tm)
    emit(as_ref, jnp.exp(s0 - A[0][None]) * (0.5 * inv_z)[None])
    emit_by_head(bs_ref, jnp.exp(s1 - Bv[0][None]))
    kio = lax.broadcasted_iota(jnp.int32, (nk, nh, tm), 0)
    c0 = jnp.zeros((nk, nh, tm), F32)
    for a in range(topk):
        c0 = jnp.where(kio == idx_ref[a][None], cnt[a][None], c0)
    emit(c0_ref, c0)
    emit_by_head(r1_ref, rank_ref[...])


def _peer_select(xn2, wqt_bf, qg_rows, bd_bf, nh, nk, tm):
    N, D = xn2.shape
    R = nk * nh
    assert tm % LANES == 0
    nlb = tm // LANES
    by_row = pl.BlockSpec((nlb, R, LANES), lambda t: (t, 0, 0))
    by_head = pl.BlockSpec((nlb, nh, nk, LANES), lambda t: (t, 0, 0, 0))
    return pl.pallas_call(
        functools.partial(_peer_select_kernel, nh=nh, nk=nk, topk=PEER_TOPK),
        grid=(N // tm,),
        in_specs=[pl.BlockSpec((tm, D), lambda t: (t, 0)),
                  pl.BlockSpec(wqt_bf.shape, lambda t: (0, 0)),
                  pl.BlockSpec(qg_rows.shape, lambda t: (0, 0)),
                  pl.BlockSpec(bd_bf.shape, lambda t: (0, 0, 0))],
        out_specs=[by_head, by_head, by_row, by_row],
        out_shape=[jax.ShapeDtypeStruct((N // LANES, nh, nk, LANES), BF16)] * 2
        + [jax.ShapeDtypeStruct((N // LANES, R, LANES), F32)] * 2,
        scratch_shapes=[pltpu.VMEM((2, R, tm), F32), pltpu.VMEM((nk, nh, tm), F32),
                        pltpu.VMEM((nk, nh, tm), F32), pltpu.VMEM((2, PEER_TOPK, nh, tm), F32),
                        pltpu.VMEM((PEER_TOPK, nh, tm), jnp.int32), pltpu.VMEM((nlb, R, LANES), F32)],
        compiler_params=_cparams(("arbitrary",)),
        name="peer_select",
    )(xn2, wqt_bf, qg_rows, bd_bf)


def _gelu_twice(x):
    return x + x * lax.erf(x * (2.0 ** -0.5))


BF16_SUBLANE_ROWS = 16


def _bcast_rows_bf16(row, n):
    one = jnp.broadcast_to(row, (BF16_SUBLANE_ROWS, row.shape[1])).astype(BF16)
    return jnp.tile(one, (n // BF16_SUBLANE_ROWS, 1))


def _peer_main_kernel(xn2_ref, x1_ref, u_ref, vt_ref, r1_ref, bs_ref, c0_ref, as_ref, y_ref,
                      xt_ref, acc_ref, gl_a, gl_b, gt_a, gt_b, *, nh, nk, ne, n_blocks):
    s = pl.program_id(0)
    te = u_ref.shape[0]
    tm = xn2_ref.shape[0]
    keys_per_block = te // nk
    e1 = s % ne
    e2 = (s + ne - 1) % ne
    e3 = (s + 2 * ne - 2) % ne

    @pl.when(s == 0)
    def _():
        gl_b[...] = jnp.zeros_like(gl_b)
        gt_a[...] = jnp.zeros_like(gt_a)
        acc_ref[...] = jnp.zeros_like(acc_ref)

    @pl.when((e1 == 0) & (s < n_blocks))
    def _():
        xt_ref[...] = xn2_ref[...].astype(F32).T.astype(BF16)

    @pl.when((e3 == 0) & (s >= 2))
    def _():
        acc_ref[...] = jnp.zeros_like(acc_ref)

    zero = jnp.zeros((nk, LANES), BF16)
    one = jnp.ones((nk, LANES), BF16)

    keys_per_piece = 1
    n_pieces = keys_per_block // keys_per_piece

    def gate_piece(e, piece, lb):
        rows = []
        for ii in range(piece * keys_per_piece, (piece + 1) * keys_per_piece):
            base = pl.multiple_of((e * keys_per_block + ii) * nh, nh)
            rows.append((c0_ref[lb, pl.ds(base, nh), :], as_ref[lb, pl.ds(base, nh), :]))
        w = [zero] * keys_per_piece
        for h in range(nh):
            slab = pl.ds((lb * nh + h) * nk, nk)
            r1 = r1_ref[slab, :]
            bs = bs_ref[slab, :]
            for k, (crow, arow) in enumerate(rows):
                cb = _bcast_rows_bf16(crow[h:h + 1, :], nk)
                ab = _bcast_rows_bf16(arow[h:h + 1, :], nk)
                below = jnp.minimum(jnp.maximum(cb - r1, zero), one)
                w[k] = w[k] + (below * bs) * ab
        return w

    def step(gl_new, gl_old, gt_new, gt_old):
        D = vt_ref.shape[0]
        nlb = tm // LANES
        n_mxu = te // MXU_PIECE_ROWS
        out_rows = D // n_mxu

        def stage1(ii):
            rows = slice(ii * MXU_PIECE_ROWS, (ii + 1) * MXU_PIECE_ROWS)
            gl_new[rows, :] = _gelu_twice(_dot(u_ref[rows, :], xt_ref[...])).astype(BF16)

        def stage2(piece, lb):
            cols = slice(lb * LANES, (lb + 1) * LANES)
            for k, w in enumerate(gate_piece(e2, piece, lb)):
                ii = piece * keys_per_piece + k
                rows = slice(ii * nk, (ii + 1) * nk)
                gt_new[rows, cols] = gl_old[rows, cols] * w

        def stage3(ii):
            orow = slice(ii * out_rows, (ii + 1) * out_rows)
            acc_ref[orow, :] += _dot(vt_ref[orow, :], gt_old[...])

        mxu_work = []
        for ii in range(n_mxu):
            mxu_work += [functools.partial(stage1, ii), functools.partial(stage3, ii)]
        vpu_work = [functools.partial(stage2, piece, lb) for piece in range(n_pieces) for lb in range(nlb)]
        order = sorted([((i + 0.5) / len(mxu_work), 0, f) for i, f in enumerate(mxu_work)]
                       + [((i + 0.5) / len(vpu_work), 1, f) for i, f in enumerate(vpu_work)], key=lambda t: t[:2])
        for _, _, piece_fn in order:
            piece_fn()

    @pl.when(s % 2 == 0)
    def _():
        step(gl_a, gl_b, gt_b, gt_a)

    @pl.when(s % 2 == 1)
    def _():
        step(gl_b, gl_a, gt_a, gt_b)

    @pl.when((e3 == ne - 1) & (s >= 2))
    def _():
        y_ref[...] = x1_ref[...] + acc_ref[...].T


def _peer_main(xn2, x1, u_bf, vt_bf, r1, bs, c0, a_s, nh, nk, tm):
    N, D = xn2.shape
    ne, _, te = vt_bf.shape
    R = nk * nh
    n_blocks = (N // tm) * ne
    last = n_blocks - 1
    nlb = tm // LANES
    tile1 = lambda s: jnp.minimum(s, last) // ne
    tile2 = lambda s: jnp.clip(s - 1, 0, last) // ne
    tile3 = lambda s: jnp.clip(s - 2, 0, last) // ne
    by_row = pl.BlockSpec((nlb, R, LANES), lambda s: (tile2(s), 0, 0))
    by_head = pl.BlockSpec((nlb * nh * nk, LANES), lambda s: (tile2(s), 0))
    r1 = r1.reshape(-1, LANES)
    bs = bs.reshape(-1, LANES)
    return pl.pallas_call(
        functools.partial(_peer_main_kernel, nh=nh, nk=nk, ne=ne, n_blocks=n_blocks),
        grid=(n_blocks + 2,),
        in_specs=[pl.BlockSpec((tm, D), lambda s: (tile1(s), 0)),
                  pl.BlockSpec((tm, D), lambda s: (tile3(s), 0)),
                  pl.BlockSpec((te, D), lambda s: (s % ne, 0)),
                  pl.BlockSpec((None, D, te), lambda s: ((s + 2 * ne - 2) % ne, 0, 0)),
                  by_head, by_head, by_row, by_row],
        out_specs=pl.BlockSpec((tm, D), lambda s: (tile3(s), 0)),
        out_shape=jax.ShapeDtypeStruct((N, D), F32),
        scratch_shapes=[pltpu.VMEM((D, tm), BF16), pltpu.VMEM((D, tm), F32)]
        + [pltpu.VMEM((te, tm), BF16)] * 4,
        compiler_params=_cparams(("arbitrary",)),
        name="peer_main",
    )(xn2, x1, u_bf, vt_bf, r1, bs, c0, a_s)


def _rope_tables(pos, head_dim, width):
    half = head_dim // 2
    inv = ROPE_THETA ** (-jnp.arange(half, dtype=F32) / half)
    ang = pos.astype(F32)[:, None] * inv[None, :]
    cos = jnp.cos(ang)
    sin = jnp.sin(ang)
    reps = width // head_dim
    cos_t = jnp.tile(jnp.concatenate([cos, cos], axis=1), (1, reps))
    sin_t = jnp.tile(jnp.concatenate([-sin, sin], axis=1), (1, reps))
    return cos_t, sin_t


def _token_tile(n, pref):
    t = min(pref, n)
    while n % t:
        t //= 2
    return t


def _trunk(x, pos, conv_state, cache_k2, cache_v2, p):
    B, T, D = x.shape
    nh, hd = p["n_heads"], p["head_dim"]
    tm = _token_tile(T, 256)
    cos_t, sin_t = _rope_tables(pos, hd, 128)
    gaa, gb, new_conv = _branch_a(x, p["g1"], p["w_in"], p["b_gate"], p["conv_w"], conv_state, p["wco"], tm)
    k, v, qb, kb, vb = _qkv(x, p["g1"], p["w_in"], p["qg"], p["kg"], cos_t, sin_t, p["grp"], tm, hd)
    if cache_k2 is None:
        on = _attn_prompt(qb, kb, vb, p["lam4"], p["sg"], nh, _token_tile(T, 512), p["lam0"])
    else:
        on = _attn_sample(qb, kb, vb, cache_k2, cache_v2, p["lam4"], p["sg"], nh, p["lam0"])
    N = B * T
    tn = _token_tile(N, 256)
    x1, xn2 = _merge(x.reshape(N, D), gaa.reshape(N, D), gb.reshape(N, D), on.reshape(N, D),
                     p["wao"], p["wo"], p["g2"], tn)
    pnh, nk = p["peer_heads"], p["n_keys"]
    tp = _token_tile(N, 512)
    qg_rows = jnp.broadcast_to(p["peer_qg_col"], (p["peer_qg_col"].shape[0], tp))
    r1, bs, c0, a_s = _peer_select(xn2, p["wqt"], qg_rows, p["bd"], pnh, nk, tp)
    y = _peer_main(xn2, x1, p["u"], p["vt"], r1, bs, c0, a_s, pnh, nk, tp)
    return y.reshape(B, T, D), k, v, new_conv


def kernel(x_prompt, x_sample, cache_k, cache_v, state_conv, norm1_g, w_in, b_gate, conv_w, q_norm_g, k_norm_g,
           lam_q1, lam_k1, lam_q2, lam_k2, subln_g, w_conv_out, w_attn_out, w_o, norm2_g, peer_wq, peer_q_g,
           peer_subkeys, peer_u, peer_v):
    depth = w_in.shape[0]
    assert depth == 1
    l = LAYER_IDX
    D = x_prompt.shape[-1]
    n_heads, head_dim, v_dim = cache_k.shape[3], cache_k.shape[5], cache_v.shape[4]
    assert v_dim == 2 * head_dim and n_heads * v_dim == D and w_in.shape[2] == 8 * D
    n_keys, dk = peer_subkeys.shape[2], peer_subkeys.shape[3]
    assert n_keys == dk
    peer_heads = peer_wq.shape[2] // (2 * dk)
    past = cache_k.shape[2]
    assert past % CHUNK == 0 and (1 << CHUNK_SHIFT) == CHUNK

    eye_h = jnp.eye(peer_heads, dtype=F32)
    grp = np.kron(np.eye(256 // head_dim), np.ones((head_dim, head_dim))).astype(np.float32)
    p = dict(
        n_heads=n_heads, head_dim=head_dim, peer_heads=peer_heads, n_keys=n_keys, lam0=_lambda_init(l),
        g1=norm1_g[l][None, :], w_in=w_in[l].astype(BF16), b_gate=b_gate[l], conv_w=conv_w[l],
        wco=w_conv_out[l].astype(BF16), wao=w_attn_out[l].astype(BF16), wo=w_o[l].astype(BF16),
        qg=jnp.tile(q_norm_g[l], D // head_dim)[None, :], kg=jnp.tile(k_norm_g[l], D // head_dim)[None, :],
        grp=jnp.asarray(grp, BF16),
        lam4=jnp.stack([lam_q1[l], lam_k1[l], lam_q2[l], lam_k2[l]]),
        sg=subln_g[l][None, :], g2=norm2_g[l][None, :],
        wqt=peer_wq[l].reshape(D, peer_heads, 2, dk).transpose(2, 1, 3, 0).reshape(2 * peer_heads * dk, D).astype(BF16),
        peer_qg_col=jnp.broadcast_to(peer_q_g[l].reshape(2, 1, dk), (2, peer_heads, dk)).reshape(-1, 1),
        bd=jnp.einsum("ckd,hg->ckhgd", peer_subkeys[l], eye_h).reshape(2, n_keys * peer_heads, peer_heads * dk).astype(BF16),
        u=peer_u[l].astype(BF16),
        vt=peer_v[l].reshape(-1, PEER_EXPERT_BLOCK, D).transpose(0, 2, 1).astype(BF16),
    )

    B, T, _ = x_prompt.shape
    Bs, Ts, _ = x_sample.shape
    pos_p = jnp.arange(T, dtype=jnp.int32)
    pos_s = past + jnp.arange(Ts, dtype=jnp.int32)
    zero_conv = jnp.zeros((B, 2, D), x_prompt.dtype)
    y_p, k_p, v_p, c_p = _trunk(x_prompt, pos_p, zero_conv, None, None, p)
    y_s, k_s, v_s, c_s = _trunk(x_sample, pos_s, state_conv[l], cache_k[l].reshape(Bs, past, D),
                                cache_v[l].reshape(Bs, past, D), p)
    return (y_p, y_s,
            k_p.reshape(1, B, T, n_heads, 2, head_dim), v_p.reshape(1, B, T, n_heads, v_dim), c_p[None],
            k_s.reshape(1, Bs, Ts, n_heads, 2, head_dim), v_s.reshape(1, Bs, Ts, n_heads, v_dim), c_s[None])
```

```python
import functools
import math

import jax
import jax.numpy as jnp
import numpy as np
from jax import lax
from jax.experimental import pallas as pl
from jax.experimental.pallas import tpu as pltpu

EPS = 1e-6
CHUNK = 64
CHUNK_SHIFT = 6
ROPE_THETA = 10000.0
PEER_TOPK = 16
LAYER_IDX = 0

V7X_VMEM_LIMIT = 56 * 1024 * 1024
LANES = 128
PEER_EXPERT_BLOCK = 2048
MXU_PIECE_ROWS = 1024

F32 = jnp.float32
BF16 = jnp.bfloat16


def _lambda_init(layer_idx):
    return 0.8 - 0.6 * math.exp(-0.3 * layer_idx)


def _dot(a, b):
    return jnp.dot(a, b, preferred_element_type=F32)


def _dot_nt(a, b):
    return lax.dot_general(a, b, (((1,), (1,)), ((), ())), preferred_element_type=F32)


def _rmsnorm_rows(x, g):
    r = lax.rsqrt(jnp.mean(x * x, axis=-1, keepdims=True) + EPS)
    return (x * r) * g


def _cparams(sem):
    return pltpu.CompilerParams(dimension_semantics=sem, vmem_limit_bytes=V7X_VMEM_LIMIT)


def _branch_a_kernel(x_ref, g1_ref, wb_ref, wc_ref, wh_ref, wga_ref, wgb_ref, bg_ref, cw_ref, cs_ref,
                     wco_ref, gaa_ref, gb_ref, nc_ref, carry_ref):
    t = pl.program_id(1)

    @pl.when(t == 0)
    def _():
        carry_ref[...] = cs_ref[...]

    x = x_ref[...]
    xn = _rmsnorm_rows(x, g1_ref[...]).astype(BF16)
    hb = _dot(xn, wb_ref[...])
    hc = _dot(xn, wc_ref[...])
    hh = _dot(xn, wh_ref[...])
    u = hc * hh
    tm = u.shape[0]
    row = lax.broadcasted_iota(jnp.int32, u.shape, 0)
    c0 = carry_ref[0:1, :]
    c1 = carry_ref[1:2, :]
    u1 = jnp.where(row == 0, c1, pltpu.roll(u, 1, 0))
    u2 = jnp.where(row == 0, c0, jnp.where(row == 1, c1, pltpu.roll(u, 2, 0)))
    y = cw_ref[0:1, :] * u2 + cw_ref[1:2, :] * u1 + cw_ref[2:3, :] * u
    conv_y = hb * y
    new_c = u[tm - 2:tm, :]
    carry_ref[...] = new_c
    nc_ref[...] = new_c
    a = _dot(conv_y.astype(BF16), wco_ref[...])
    ga = jax.nn.sigmoid(_dot(xn, wga_ref[...]) + bg_ref[0:1, :])
    gb = jax.nn.sigmoid(_dot(xn, wgb_ref[...]) + bg_ref[1:2, :])
    gaa_ref[...] = ga * a
    gb_ref[...] = gb


def _branch_a(x, g1, w_in_bf, b_gate, conv_w, conv_state, wco_bf, tm):
    B, T, D = x.shape
    nt = T // tm
    wspec = lambda j: pl.BlockSpec((D, D), lambda b, t, j=j: (0, j))
    tile = pl.BlockSpec((None, tm, D), lambda b, t: (b, t, 0))
    full2 = lambda r: pl.BlockSpec((r, D), lambda b, t: (0, 0))
    return pl.pallas_call(
        _branch_a_kernel,
        grid=(B, nt),
        in_specs=[tile, full2(1), wspec(0), wspec(1), wspec(2), wspec(6), wspec(7), full2(2), full2(3),
                  pl.BlockSpec((None, 2, D), lambda b, t: (b, 0, 0)), pl.BlockSpec((D, D), lambda b, t: (0, 0))],
        out_specs=[tile, tile, pl.BlockSpec((None, 2, D), lambda b, t: (b, 0, 0))],
        out_shape=[jax.ShapeDtypeStruct((B, T, D), F32), jax.ShapeDtypeStruct((B, T, D), F32),
                   jax.ShapeDtypeStruct((B, 2, D), F32)],
        scratch_shapes=[pltpu.VMEM((2, D), F32)],
        compiler_params=_cparams(("arbitrary", "arbitrary")),
        name="branch_a",
    )(x, g1, w_in_bf, w_in_bf, w_in_bf, w_in_bf, w_in_bf, b_gate, conv_w, conv_state, wco_bf)


def _qkv_kernel(x_ref, g1_ref, wq_ref, wk_ref, wv_ref, qg_ref, kg_ref, cos_ref, sin_ref, grp_ref,
                k_ref, v_ref, qb_ref, kb_ref, vb_ref, *, head_dim):
    x = x_ref[...]
    xn = _rmsnorm_rows(x, g1_ref[...]).astype(BF16)
    q = _dot(xn, wq_ref[...])
    k = _dot(xn, wk_ref[...])
    v = _dot(xn, wv_ref[...])
    D = q.shape[1]
    half = head_dim // 2
    reps = D // cos_ref.shape[1]
    cos = jnp.tile(cos_ref[...], (1, reps))
    sin = jnp.tile(sin_ref[...], (1, reps))
    lane = lax.broadcasted_iota(jnp.int32, q.shape, 1)
    first_half = (lane & half) == 0
    grp = grp_ref[...]
    gw = grp.shape[0]

    def headnorm_rope(z, g):
        z2 = z * z
        hi = z2.astype(BF16)
        lo = (z2 - hi.astype(F32)).astype(BF16)
        parts = []
        for s in range(D // gw):
            sl = slice(s * gw, (s + 1) * gw)
            parts.append(_dot(hi[:, sl], grp) + _dot(lo[:, sl], grp))
        ss = jnp.concatenate(parts, axis=1)
        r = lax.rsqrt(ss * (1.0 / head_dim) + EPS)
        zn = (z * r) * g
        sw = jnp.where(first_half, pltpu.roll(zn, D - half, 1), pltpu.roll(zn, half, 1))
        return zn * cos + sw * sin

    qr = headnorm_rope(q, qg_ref[...])
    kr = headnorm_rope(k, kg_ref[...])
    k_ref[...] = kr
    v_ref[...] = v
    hw = 2 * head_dim
    for dst, val in ((qb_ref, qr * (head_dim ** -0.5)), (kb_ref, kr), (vb_ref, v)):
        val = val.astype(BF16)
        for h in range(D // hw):
            dst[h] = val[:, h * hw:(h + 1) * hw]


def _qkv(x, g1, w_in_bf, qg_row, kg_row, cos_t, sin_t, grp, tm, head_dim):
    B, T, D = x.shape
    nt = T // tm
    hw = 2 * head_dim
    nh = D // hw
    wspec = lambda j: pl.BlockSpec((D, D), lambda b, t, j=j: (0, j))
    tile = pl.BlockSpec((None, tm, D), lambda b, t: (b, t, 0))
    row = pl.BlockSpec((1, D), lambda b, t: (0, 0))
    tab = pl.BlockSpec((tm, cos_t.shape[1]), lambda b, t: (t, 0))
    return pl.pallas_call(
        functools.partial(_qkv_kernel, head_dim=head_dim),
        grid=(B, nt),
        in_specs=[tile, row, wspec(3), wspec(4), wspec(5), row, row, tab, tab,
                  pl.BlockSpec(grp.shape, lambda b, t: (0, 0))],
        out_specs=[tile] * 2 + [pl.BlockSpec((None, nh, tm, hw), lambda b, t: (b, 0, t, 0))] * 3,
        out_shape=[jax.ShapeDtypeStruct((B, T, D), F32)] * 2 + [jax.ShapeDtypeStruct((B, nh, T, hw), BF16)] * 3,
        compiler_params=_cparams(("arbitrary", "arbitrary")),
        name="qkv",
    )(x, g1, w_in_bf, w_in_bf, w_in_bf, qg_row, kg_row, cos_t, sin_t, grp)


def _stack_halves(q, head_dim):
    lane = lax.broadcasted_iota(jnp.int32, q.shape, 1)
    zero = jnp.zeros_like(q)
    return jnp.concatenate([jnp.where(lane < head_dim, q, zero), jnp.where(lane >= head_dim, q, zero)], axis=0)


def _lambda_value(lam_ref, lam0):
    l = lam_ref[...]
    s1 = jnp.sum(l[0:1, :] * l[1:2, :], axis=-1, keepdims=True)
    s2 = jnp.sum(l[2:3, :] * l[3:4, :], axis=-1, keepdims=True)
    return jnp.exp(s1) - jnp.exp(s2) + lam0


def _attn_finish(acc, l, tq, lam, sg, lam0):
    o = acc[:tq] / l[:tq] - lam * (acc[tq:] / l[tq:])
    return _rmsnorm_rows(o, sg) * (1.0 - lam0)


def _attn_prompt_kernel(q_ref, k_ref, v_ref, lam_ref, sg_ref, o_ref, *, tq, head_dim, lam0):
    qi = pl.program_id(2)
    qs = _stack_halves(q_ref[...], head_dim)

    def block(j, carry, masked):
        m, l, acc = carry
        kb = k_ref[j * tq:(j + 1) * tq, :]
        vb = v_ref[j * tq:(j + 1) * tq, :]
        s = _dot_nt(qs, kb)
        if masked:
            r = lax.broadcasted_iota(jnp.int32, s.shape, 0)
            c = lax.broadcasted_iota(jnp.int32, s.shape, 1)
            rr = jnp.where(r >= tq, r - tq, r)
            vis = (c >> CHUNK_SHIFT) <= (rr >> CHUNK_SHIFT)
            s = jnp.where(vis, s, -1e30)
        m_new = jnp.maximum(m, jnp.max(s, axis=-1, keepdims=True))
        alpha = jnp.exp(m - m_new)
        p = jnp.exp(s - m_new)
        l = alpha * l + jnp.sum(p, axis=-1, keepdims=True)
        acc = alpha * acc + _dot(p.astype(BF16), vb)
        return m_new, l, acc

    init = (jnp.full((2 * tq, 1), -1e30, F32), jnp.zeros((2 * tq, 1), F32),
            jnp.zeros((2 * tq, v_ref.shape[1]), F32))
    for n_full in range(k_ref.shape[0] // tq):
        @pl.when(qi == n_full)
        def _(n_full=n_full):
            carry = init
            for j in range(n_full):
                carry = block(j, carry, False)
            _, l, acc = block(n_full, carry, True)
            o_ref[...] = _attn_finish(acc, l, tq, _lambda_value(lam_ref, lam0), sg_ref[...],
                                      lam0).astype(o_ref.dtype)


def _attn_prompt(qb, kb, vb, lam4, sg_row, n_heads, tq, lam0):
    B, n_heads_, T, hw = qb.shape
    assert n_heads_ == n_heads
    D = n_heads * hw
    head_dim = hw // 2
    assert tq % CHUNK == 0 and T % tq == 0
    return pl.pallas_call(
        functools.partial(_attn_prompt_kernel, tq=tq, head_dim=head_dim, lam0=lam0),
        grid=(B, n_heads, T // tq),
        in_specs=[pl.BlockSpec((None, None, tq, hw), lambda b, h, i: (b, h, i, 0)),
                  pl.BlockSpec((None, None, T, hw), lambda b, h, i: (b, h, 0, 0)),
                  pl.BlockSpec((None, None, T, hw), lambda b, h, i: (b, h, 0, 0)),
                  pl.BlockSpec(lam4.shape, lambda b, h, i: (0, 0)),
                  pl.BlockSpec((1, hw), lambda b, h, i: (0, 0))],
        out_specs=pl.BlockSpec((None, tq, hw), lambda b, h, i: (b, i, h)),
        out_shape=jax.ShapeDtypeStruct((B, T, D), BF16),
        compiler_params=_cparams(("arbitrary", "arbitrary", "arbitrary")),
        name="attn_prompt",
    )(qb, kb, vb, lam4, sg_row)


def _attn_sample_kernel(q_ref, kc_ref, vc_ref, kn_ref, vn_ref, lam_ref, sg_ref, o_ref, *, head_dim, lam0):
    ts = q_ref.shape[0]
    past = kc_ref.shape[0]
    qs = _stack_halves(q_ref[...], head_dim)
    sc = _dot_nt(qs, kc_ref[...].astype(BF16))
    sn = _dot_nt(qs, kn_ref[...])

    def visible(shape, k_off):
        r = lax.broadcasted_iota(jnp.int32, shape, 0)
        c = lax.broadcasted_iota(jnp.int32, shape, 1)
        q_pos = past + jnp.where(r >= ts, r - ts, r)
        return ((c + k_off) >> CHUNK_SHIFT) <= (q_pos >> CHUNK_SHIFT)

    sc = jnp.where(visible(sc.shape, 0), sc, -1e30)
    sn = jnp.where(visible(sn.shape, past), sn, -1e30)
    m = jnp.maximum(jnp.max(sc, axis=-1, keepdims=True), jnp.max(sn, axis=-1, keepdims=True))
    pc = jnp.exp(sc - m)
    pn = jnp.exp(sn - m)
    l = jnp.sum(pc, axis=-1, keepdims=True) + jnp.sum(pn, axis=-1, keepdims=True)
    acc = _dot(pc.astype(BF16), vc_ref[...].astype(BF16)) + _dot(pn.astype(BF16), vn_ref[...])
    o_ref[...] = _attn_finish(acc, l, ts, _lambda_value(lam_ref, lam0), sg_ref[...], lam0).astype(o_ref.dtype)


def _attn_sample(qb, kb, vb, cache_k2, cache_v2, lam4, sg_row, n_heads, lam0):
    B, _, Ts, hw = qb.shape
    D = n_heads * hw
    P = cache_k2.shape[1]
    new = pl.BlockSpec((None, None, Ts, hw), lambda b, h: (b, h, 0, 0))
    old = pl.BlockSpec((None, P, hw), lambda b, h: (b, 0, h))
    return pl.pallas_call(
        functools.partial(_attn_sample_kernel, head_dim=hw // 2, lam0=lam0),
        grid=(B, n_heads),
        in_specs=[new, old, old, new, new, pl.BlockSpec(lam4.shape, lambda b, h: (0, 0)),
                  pl.BlockSpec((1, hw), lambda b, h: (0, 0))],
        out_specs=pl.BlockSpec((None, Ts, hw), lambda b, h: (b, 0, h)),
        out_shape=jax.ShapeDtypeStruct((B, Ts, D), BF16),
        compiler_params=_cparams(("arbitrary", "arbitrary")),
        name="attn_sample",
    )(qb, cache_k2, cache_v2, kb, vb, lam4, sg_row)


def _merge_kernel(x_ref, gaa_ref, gb_ref, on_ref, wao_ref, wo_ref, g2_ref, x1_ref, xn2_ref):
    b = _dot(on_ref[...], wao_ref[...])
    mix = gaa_ref[...] + gb_ref[...] * b
    x1 = x_ref[...] + _dot(mix.astype(BF16), wo_ref[...])
    x1_ref[...] = x1
    xn2_ref[...] = _rmsnorm_rows(x1, g2_ref[...]).astype(BF16)


def _merge(x2, gaa2, gb2, on2, wao_bf, wo_bf, g2, tm):
    N, D = x2.shape
    tile = pl.BlockSpec((tm, D), lambda t: (t, 0))
    wfull = pl.BlockSpec((D, D), lambda t: (0, 0))
    return pl.pallas_call(
        _merge_kernel,
        grid=(N // tm,),
        in_specs=[tile, tile, tile, tile, wfull, wfull, pl.BlockSpec((1, D), lambda t: (0, 0))],
        out_specs=[tile, tile],
        out_shape=[jax.ShapeDtypeStruct((N, D), F32), jax.ShapeDtypeStruct((N, D), BF16)],
        compiler_params=_cparams(("arbitrary",)),
        name="merge",
    )(x2, gaa2, gb2, on2, wao_bf, wo_bf, g2)


def _staircase_pairs(topk):
    return [(a, b) for a in range(topk) for b in range(topk) if (a + 1) * (b + 1) <= topk]


def _peer_select_kernel(xn2_ref, wqt_ref, qg_ref, bd_ref, r1_ref, bs_ref, c0_ref, as_ref,
                        s_ref, work_ref, rank_ref, top_ref, idx_ref, tr_ref, *, nh, nk, topk):
    tm = xn2_ref.shape[0]
    dk = nk
    qp = _dot_nt(wqt_ref[...], xn2_ref[...])
    qg = qg_ref[...]

    for h in range(nh):
        lo = qp[h * dk:(h + 1) * dk, :]
        hi = qp[(nh + h) * dk:(nh + h + 1) * dk, :]
        ss = jnp.sum(lo * lo + hi * hi, axis=0, keepdims=True)
        r = lax.rsqrt(ss * (1.0 / (2 * dk)) + EPS)
        for c, part in ((0, lo), (1, hi)):
            rows = slice((c * nh + h) * dk, (c * nh + h + 1) * dk)
            s_ref[c, pl.ds(h * dk, dk), :] = (part * r) * qg[rows, :]

    for c in range(2):
        qn = s_ref[c].astype(BF16)
        s_ref[c] = _dot(bd_ref[c], qn)

    neg_inf = jnp.float32(-jnp.inf)
    n_chains = 8
    per_chain = nk // n_chains

    for c in range(2):
        work_ref[...] = s_ref[c].reshape(nk, nh, tm)
        if c == 1:
            rank_ref[...] = jnp.full((nk, nh, tm), float(topk), F32)

        def round_body(r, _, c=c):
            chains = []
            for g in range(n_chains):
                k0 = g * per_chain
                m = work_ref[k0]
                idx = jnp.full((nh, tm), k0, jnp.int32)
                for k in range(k0 + 1, k0 + per_chain):
                    w = work_ref[k]
                    gt = w > m
                    m = jnp.where(gt, w, m)
                    idx = jnp.where(gt, k, idx)
                chains.append((m, idx))
            while len(chains) > 1:
                merged = []
                for (ma, ia), (mb, ib) in zip(chains[0::2], chains[1::2]):
                    gt = mb > ma
                    merged.append((jnp.where(gt, mb, ma), jnp.where(gt, ib, ia)))
                chains = merged
            m, idx = chains[0]
            top_ref[c, pl.ds(r, 1)] = m[None]
            if c == 0:
                idx_ref[pl.ds(r, 1)] = idx[None]
            rf = lax.convert_element_type(r, F32)
            for k in range(nk):
                oh = idx == k
                work_ref[k] = jnp.where(oh, neg_inf, work_ref[k])
                if c == 1:
                    rank_ref[k] = jnp.where(oh, rf, rank_ref[k])
            return 0

        lax.fori_loop(0, topk, round_body, 0)

    A = [top_ref[0, a] for a in range(topk)]
    Bv = [top_ref[1, b] for b in range(topk)]

    pairs = _staircase_pairs(topk)
    cand = [A[a] + Bv[b] for (a, b) in pairs]
    n = len(pairs)
    beaten = [jnp.zeros((nh, tm), F32) for _ in range(n)]
    for p in range(n):
        ap, bp = pairs[p]
        for q in range(p + 1, n):
            aq, bq = pairs[q]
            if ap <= aq and bp <= bq:
                beaten[q] = beaten[q] + 1.0
            else:
                t = jnp.where(cand[p] >= cand[q], 1.0, 0.0)
                beaten[q] = beaten[q] + t
                beaten[p] = beaten[p] + (1.0 - t)
    sel = [jnp.where(beaten[p] < float(topk), 1.0, 0.0) for p in range(n)]

    ea = [jnp.exp(A[a] - A[0]) for a in range(topk)]
    eb = [jnp.exp(Bv[b] - Bv[0]) for b in range(topk)]
    z = jnp.zeros((nh, tm), F32)
    cnt = [jnp.zeros((nh, tm), F32) for _ in range(topk)]
    for p, (a, b) in enumerate(pairs):
        z = z + sel[p] * (ea[a] * eb[b])
        cnt[a] = cnt[a] + sel[p]
    inv_z = 1.0 / z

    def emit(out_ref, val):
        val = val.reshape(nk * nh, tm)
        for lb in range(tm // LANES):
            out_ref[lb] = val[:, lb * LANES:(lb + 1) * LANES]

    def emit_by_head(out_ref, val):
        emit(tr_ref, val)
        for lb in range(tm // LANES):
            for h in range(nh):
                out_ref[lb, h] = tr_ref[lb, pl.ds(h, nk, stride=nh), :].astype(BF16)

    s0 = s_ref[0].reshape(nk, nh, tm)
    s1 = s_ref[1].reshape(nk, nh, tm)
    emit(as_ref, jnp.exp(s0 - A[0][None]) * (0.5 * inv_z)[None])
    emit_by_head(bs_ref, jnp.exp(s1 - Bv[0][None]))
    kio = lax.broadcasted_iota(jnp.int32, (nk, nh, tm), 0)
    c0 = jnp.zeros((nk, nh, tm), F32)
    for a in range(topk):
        c0 = jnp.where(kio == idx_ref[a][None], cnt[a][None], c0)
    emit(c0_ref, c0)
    emit_by_head(r1_ref, rank_ref[...])


def _peer_select(xn2, wqt_bf, qg_rows, bd_bf, nh, nk, tm):
    N, D = xn2.shape
    R = nk * nh
    assert tm % LANES == 0
    nlb = tm // LANES
    by_row = pl.BlockSpec((nlb, R, LANES), lambda t: (t, 0, 0))
    by_head = pl.BlockSpec((nlb, nh, nk, LANES), lambda t: (t, 0, 0, 0))
    return pl.pallas_call(
        functools.partial(_peer_select_kernel, nh=nh, nk=nk, topk=PEER_TOPK),
        grid=(N // tm,),
        in_specs=[pl.BlockSpec((tm, D), lambda t: (t, 0)),
                  pl.BlockSpec(wqt_bf.shape, lambda t: (0, 0)),
                  pl.BlockSpec(qg_rows.shape, lambda t: (0, 0)),
                  pl.BlockSpec(bd_bf.shape, lambda t: (0, 0, 0))],
        out_specs=[by_head, by_head, by_row, by_row],
        out_shape=[jax.ShapeDtypeStruct((N // LANES, nh, nk, LANES), BF16)] * 2
        + [jax.ShapeDtypeStruct((N // LANES, R, LANES), F32)] * 2,
        scratch_shapes=[pltpu.VMEM((2, R, tm), F32), pltpu.VMEM((nk, nh, tm), F32),
                        pltpu.VMEM((nk, nh, tm), F32), pltpu.VMEM((2, PEER_TOPK, nh, tm), F32),
                        pltpu.VMEM((PEER_TOPK, nh, tm), jnp.int32), pltpu.VMEM((nlb, R, LANES), F32)],
        compiler_params=_cparams(("arbitrary",)),
        name="peer_select",
    )(xn2, wqt_bf, qg_rows, bd_bf)


def _gelu_twice(x):
    return x + x * lax.erf(x * (2.0 ** -0.5))


BF16_SUBLANE_ROWS = 16


def _bcast_rows_bf16(row, n):
    one = jnp.broadcast_to(row, (BF16_SUBLANE_ROWS, row.shape[1])).astype(BF16)
    return jnp.tile(one, (n // BF16_SUBLANE_ROWS, 1))


def _peer_main_kernel(xn2_ref, x1_ref, u_ref, vt_ref, r1_ref, bs_ref, c0_ref, as_ref, y_ref,
                      xt_ref, acc_ref, gl_a, gl_b, gt_a, gt_b, *, nh, nk, ne, n_blocks):
    s = pl.program_id(0)
    te = u_ref.shape[0]
    tm = xn2_ref.shape[0]
    keys_per_block = te // nk
    e1 = s % ne
    e2 = (s + ne - 1) % ne
    e3 = (s + 2 * ne - 2) % ne

    @pl.when(s == 0)
    def _():
        gl_b[...] = jnp.zeros_like(gl_b)
        gt_a[...] = jnp.zeros_like(gt_a)
        acc_ref[...] = jnp.zeros_like(acc_ref)

    @pl.when((e1 == 0) & (s < n_blocks))
    def _():
        xt_ref[...] = xn2_ref[...].astype(F32).T.astype(BF16)

    @pl.when((e3 == 0) & (s >= 2))
    def _():
        acc_ref[...] = jnp.zeros_like(acc_ref)

    zero = jnp.zeros((nk, LANES), BF16)
    one = jnp.ones((nk, LANES), BF16)

    keys_per_piece = 1
    n_pieces = keys_per_block // keys_per_piece

    def gate_piece(e, piece, lb):
        rows = []
        for ii in range(piece * keys_per_piece, (piece + 1) * keys_per_piece):
            base = pl.multiple_of((e * keys_per_block + ii) * nh, nh)
            rows.append((c0_ref[lb, pl.ds(base, nh), :], as_ref[lb, pl.ds(base, nh), :]))
        w = [zero] * keys_per_piece
        for h in range(nh):
            slab = pl.ds((lb * nh + h) * nk, nk)
            r1 = r1_ref[slab, :]
            bs = bs_ref[slab, :]
            for k, (crow, arow) in enumerate(rows):
                cb = _bcast_rows_bf16(crow[h:h + 1, :], nk)
                ab = _bcast_rows_bf16(arow[h:h + 1, :], nk)
                below = jnp.minimum(jnp.maximum(cb - r1, zero), one)
                w[k] = w[k] + (below * bs) * ab
        return w

    def step(gl_new, gl_old, gt_new, gt_old):
        D = vt_ref.shape[0]
        nlb = tm // LANES
        n_mxu = te // MXU_PIECE_ROWS
        out_rows = D // n_mxu

        def stage1(ii):
            rows = slice(ii * MXU_PIECE_ROWS, (ii + 1) * MXU_PIECE_ROWS)
            gl_new[rows, :] = _gelu_twice(_dot(u_ref[rows, :], xt_ref[...])).astype(BF16)

        def stage2(piece, lb):
            cols = slice(lb * LANES, (lb + 1) * LANES)
            for k, w in enumerate(gate_piece(e2, piece, lb)):
                ii = piece * keys_per_piece + k
                rows = slice(ii * nk, (ii + 1) * nk)
                gt_new[rows, cols] = gl_old[rows, cols] * w

        def stage3(ii):
            orow = slice(ii * out_rows, (ii + 1) * out_rows)
            acc_ref[orow, :] += _dot(vt_ref[orow, :], gt_old[...])

        mxu_work = []
        for ii in range(n_mxu):
            mxu_work += [functools.partial(stage1, ii), functools.partial(stage3, ii)]
        vpu_work = [functools.partial(stage2, piece, lb) for piece in range(n_pieces) for lb in range(nlb)]
        order = sorted([((i + 0.5) / len(mxu_work), 0, f) for i, f in enumerate(mxu_work)]
                       + [((i + 0.5) / len(vpu_work), 1, f) for i, f in enumerate(vpu_work)], key=lambda t: t[:2])
        for _, _, piece_fn in order:
            piece_fn()

    @pl.when(s % 2 == 0)
    def _():
        step(gl_a, gl_b, gt_b, gt_a)

    @pl.when(s % 2 == 1)
    def _():
        step(gl_b, gl_a, gt_a, gt_b)

    @pl.when((e3 == ne - 1) & (s >= 2))
    def _():
        y_ref[...] = x1_ref[...] + acc_ref[...].T


def _peer_main(xn2, x1, u_bf, vt_bf, r1, bs, c0, a_s, nh, nk, tm):
    N, D = xn2.shape
    ne, _, te = vt_bf.shape
    R = nk * nh
    n_blocks = (N // tm) * ne
    last = n_blocks - 1
    nlb = tm // LANES
    tile1 = lambda s: jnp.minimum(s, last) // ne
    tile2 = lambda s: jnp.clip(s - 1, 0, last) // ne
    tile3 = lambda s: jnp.clip(s - 2, 0, last) // ne
    by_row = pl.BlockSpec((nlb, R, LANES), lambda s: (tile2(s), 0, 0))
    by_head = pl.BlockSpec((nlb * nh * nk, LANES), lambda s: (tile2(s), 0))
    r1 = r1.reshape(-1, LANES)
    bs = bs.reshape(-1, LANES)
    return pl.pallas_call(
        functools.partial(_peer_main_kernel, nh=nh, nk=nk, ne=ne, n_blocks=n_blocks),
        grid=(n_blocks + 2,),
        in_specs=[pl.BlockSpec((tm, D), lambda s: (tile1(s), 0)),
                  pl.BlockSpec((tm, D), lambda s: (tile3(s), 0)),
                  pl.BlockSpec((te, D), lambda s: (s % ne, 0)),
                  pl.BlockSpec((None, D, te), lambda s: ((s + 2 * ne - 2) % ne, 0, 0)),
                  by_head, by_head, by_row, by_row],
        out_specs=pl.BlockSpec((tm, D), lambda s: (tile3(s), 0)),
        out_shape=jax.ShapeDtypeStruct((N, D), F32),
        scratch_shapes=[pltpu.VMEM((D, tm), BF16), pltpu.VMEM((D, tm), F32)]
        + [pltpu.VMEM((te, tm), BF16)] * 4,
        compiler_params=_cparams(("arbitrary",)),
        name="peer_main",
    )(xn2, x1, u_bf, vt_bf, r1, bs, c0, a_s)


def _rope_tables(pos, head_dim, width):
    half = head_dim // 2
    inv = ROPE_THETA ** (-jnp.arange(half, dtype=F32) / half)
    ang = pos.astype(F32)[:, None] * inv[None, :]
    cos = jnp.cos(ang)
    sin = jnp.sin(ang)
    reps = width // head_dim
    cos_t = jnp.tile(jnp.concatenate([cos, cos], axis=1), (1, reps))
    sin_t = jnp.tile(jnp.concatenate([-sin, sin], axis=1), (1, reps))
    return cos_t, sin_t


def _token_tile(n, pref):
    t = min(pref, n)
    while n % t:
        t //= 2
    return t


def _trunk(x, pos, conv_state, cache_k2, cache_v2, p):
    B, T, D = x.shape
    nh, hd = p["n_heads"], p["head_dim"]
    tm = _token_tile(T, 256)
    cos_t, sin_t = _rope_tables(pos, hd, 128)
    gaa, gb, new_conv = _branch_a(x, p["g1"], p["w_in"], p["b_gate"], p["conv_w"], conv_state, p["wco"], tm)
    k, v, qb, kb, vb = _qkv(x, p["g1"], p["w_in"], p["qg"], p["kg"], cos_t, sin_t, p["grp"], tm, hd)
    if cache_k2 is None:
        on = _attn_prompt(qb, kb, vb, p["lam4"], p["sg"], nh, _token_tile(T, 512), p["lam0"])
    else:
        on = _attn_sample(qb, kb, vb, cache_k2, cache_v2, p["lam4"], p["sg"], nh, p["lam0"])
    N = B * T
    tn = _token_tile(N, 256)
    x1, xn2 = _merge(x.reshape(N, D), gaa.reshape(N, D), gb.reshape(N, D), on.reshape(N, D),
                     p["wao"], p["wo"], p["g2"], tn)
    pnh, nk = p["peer_heads"], p["n_keys"]
    tp = _token_tile(N, 512)
    qg_rows = jnp.broadcast_to(p["peer_qg_col"], (p["peer_qg_col"].shape[0], tp))
    r1, bs, c0, a_s = _peer_select(xn2, p["wqt"], qg_rows, p["bd"], pnh, nk, tp)
    y = _peer_main(xn2, x1, p["u"], p["vt"], r1, bs, c0, a_s, pnh, nk, tp)
    return y.reshape(B, T, D), k, v, new_conv


def kernel(x_prompt, x_sample, cache_k, cache_v, state_conv, norm1_g, w_in, b_gate, conv_w, q_norm_g, k_norm_g,
           lam_q1, lam_k1, lam_q2, lam_k2, subln_g, w_conv_out, w_attn_out, w_o, norm2_g, peer_wq, peer_q_g,
           peer_subkeys, peer_u, peer_v):
    depth = w_in.shape[0]
    assert depth == 1
    l = LAYER_IDX
    D = x_prompt.shape[-1]
    n_heads, head_dim, v_dim = cache_k.shape[3], cache_k.shape[5], cache_v.shape[4]
    assert v_dim == 2 * head_dim and n_heads * v_dim == D and w_in.shape[2] == 8 * D
    n_keys, dk = peer_subkeys.shape[2], peer_subkeys.shape[3]
    assert n_keys == dk
    peer_heads = peer_wq.shape[2] // (2 * dk)
    past = cache_k.shape[2]
    assert past % CHUNK == 0 and (1 << CHUNK_SHIFT) == CHUNK

    eye_h = jnp.eye(peer_heads, dtype=F32)
    grp = np.kron(np.eye(256 // head_dim), np.ones((head_dim, head_dim))).astype(np.float32)
    p = dict(
        n_heads=n_heads, head_dim=head_dim, peer_heads=peer_heads, n_keys=n_keys, lam0=_lambda_init(l),
        g1=norm1_g[l][None, :], w_in=w_in[l].astype(BF16), b_gate=b_gate[l], conv_w=conv_w[l],
        wco=w_conv_out[l].astype(BF16), wao=w_attn_out[l].astype(BF16), wo=w_o[l].astype(BF16),
        qg=jnp.tile(q_norm_g[l], D // head_dim)[None, :], kg=jnp.tile(k_norm_g[l], D // head_dim)[None, :],
        grp=jnp.asarray(grp, BF16),
        lam4=jnp.stack([lam_q1[l], lam_k1[l], lam_q2[l], lam_k2[l]]),
        sg=subln_g[l][None, :], g2=norm2_g[l][None, :],
        wqt=peer_wq[l].reshape(D, peer_heads, 2, dk).transpose(2, 1, 3, 0).reshape(2 * peer_heads * dk, D).astype(BF16),
        peer_qg_col=jnp.broadcast_to(peer_q_g[l].reshape(2, 1, dk), (2, peer_heads, dk)).reshape(-1, 1),
        bd=jnp.einsum("ckd,hg->ckhgd", peer_subkeys[l], eye_h).reshape(2, n_keys * peer_heads, peer_heads * dk).astype(BF16),
        u=peer_u[l].astype(BF16),
        vt=peer_v[l].reshape(-1, PEER_EXPERT_BLOCK, D).transpose(0, 2, 1).astype(BF16),
    )

    B, T, _ = x_prompt.shape
    Bs, Ts, _ = x_sample.shape
    pos_p = jnp.arange(T, dtype=jnp.int32)
    pos_s = past + jnp.arange(Ts, dtype=jnp.int32)
    zero_conv = jnp.zeros((B, 2, D), x_prompt.dtype)
    y_p, k_p, v_p, c_p = _trunk(x_prompt, pos_p, zero_conv, None, None, p)
    y_s, k_s, v_s, c_s = _trunk(x_sample, pos_s, state_conv[l], cache_k[l].reshape(Bs, past, D),
                                cache_v[l].reshape(Bs, past, D), p)
    return (y_p, y_s,
            k_p.reshape(1, B, T, n_heads, 2, head_dim), v_p.reshape(1, B, T, n_heads, v_dim), c_p[None],
            k_s.reshape(1, Bs, Ts, n_heads, 2, head_dim), v_s.reshape(1, Bs, Ts, n_heads, v_dim), c_s[None])
```

```python
import functools
import math

import jax
import jax.numpy as jnp
import numpy as np
from jax import lax
from jax.experimental import pallas as pl
from jax.experimental.pallas import tpu as pltpu

EPS = 1e-6
CHUNK = 64
CHUNK_SHIFT = 6
ROPE_THETA = 10000.0
PEER_TOPK = 16
LAYER_IDX = 0

V7X_VMEM_LIMIT = 56 * 1024 * 1024
LANES = 128
PEER_EXPERT_BLOCK = 2048
MXU_PIECE_ROWS = 1024

F32 = jnp.float32
BF16 = jnp.bfloat16


def _lambda_init(layer_idx):
    return 0.8 - 0.6 * math.exp(-0.3 * layer_idx)


def _dot(a, b):
    return jnp.dot(a, b, preferred_element_type=F32)


def _dot_nt(a, b):
    return lax.dot_general(a, b, (((1,), (1,)), ((), ())), preferred_element_type=F32)


def _rmsnorm_rows(x, g):
    r = lax.rsqrt(jnp.mean(x * x, axis=-1, keepdims=True) + EPS)
    return (x * r) * g


def _cparams(sem):
    return pltpu.CompilerParams(dimension_semantics=sem, vmem_limit_bytes=V7X_VMEM_LIMIT)


def _branch_a_kernel(x_ref, g1_ref, wb_ref, wc_ref, wh_ref, wga_ref, wgb_ref, bg_ref, cw_ref, cs_ref,
                     wco_ref, gaa_ref, gb_ref, nc_ref, carry_ref):
    t = pl.program_id(1)

    @pl.when(t == 0)
    def _():
        carry_ref[...] = cs_ref[...]

    x = x_ref[...]
    xn = _rmsnorm_rows(x, g1_ref[...]).astype(BF16)
    hb = _dot(xn, wb_ref[...])
    hc = _dot(xn, wc_ref[...])
    hh = _dot(xn, wh_ref[...])
    u = hc * hh
    tm = u.shape[0]
    row = lax.broadcasted_iota(jnp.int32, u.shape, 0)
    c0 = carry_ref[0:1, :]
    c1 = carry_ref[1:2, :]
    u1 = jnp.where(row == 0, c1, pltpu.roll(u, 1, 0))
    u2 = jnp.where(row == 0, c0, jnp.where(row == 1, c1, pltpu.roll(u, 2, 0)))
    y = cw_ref[0:1, :] * u2 + cw_ref[1:2, :] * u1 + cw_ref[2:3, :] * u
    conv_y = hb * y
    new_c = u[tm - 2:tm, :]
    carry_ref[...] = new_c
    nc_ref[...] = new_c
    a = _dot(conv_y.astype(BF16), wco_ref[...])
    ga = jax.nn.sigmoid(_dot(xn, wga_ref[...]) + bg_ref[0:1, :])
    gb = jax.nn.sigmoid(_dot(xn, wgb_ref[...]) + bg_ref[1:2, :])
    gaa_ref[...] = ga * a
    gb_ref[...] = gb


def _branch_a(x, g1, w_in_bf, b_gate, conv_w, conv_state, wco_bf, tm):
    B, T, D = x.shape
    nt = T // tm
    wspec = lambda j: pl.BlockSpec((D, D), lambda b, t, j=j: (0, j))
    tile = pl.BlockSpec((None, tm, D), lambda b, t: (b, t, 0))
    full2 = lambda r: pl.BlockSpec((r, D), lambda b, t: (0, 0))
    return pl.pallas_call(
        _branch_a_kernel,
        grid=(B, nt),
        in_specs=[tile, full2(1), wspec(0), wspec(1), wspec(2), wspec(6), wspec(7), full2(2), full2(3),
                  pl.BlockSpec((None, 2, D), lambda b, t: (b, 0, 0)), pl.BlockSpec((D, D), lambda b, t: (0, 0))],
        out_specs=[tile, tile, pl.BlockSpec((None, 2, D), lambda b, t: (b, 0, 0))],
        out_shape=[jax.ShapeDtypeStruct((B, T, D), F32), jax.ShapeDtypeStruct((B, T, D), F32),
                   jax.ShapeDtypeStruct((B, 2, D), F32)],
        scratch_shapes=[pltpu.VMEM((2, D), F32)],
        compiler_params=_cparams(("arbitrary", "arbitrary")),
        name="branch_a",
    )(x, g1, w_in_bf, w_in_bf, w_in_bf, w_in_bf, w_in_bf, b_gate, conv_w, conv_state, wco_bf)


def _qkv_kernel(x_ref, g1_ref, wq_ref, wk_ref, wv_ref, qg_ref, kg_ref, cos_ref, sin_ref, grp_ref,
                k_ref, v_ref, qb_ref, kb_ref, vb_ref, *, head_dim):
    x = x_ref[...]
    xn = _rmsnorm_rows(x, g1_ref[...]).astype(BF16)
    q = _dot(xn, wq_ref[...])
    k = _dot(xn, wk_ref[...])
    v = _dot(xn, wv_ref[...])
    D = q.shape[1]
    half = head_dim // 2
    reps = D // cos_ref.shape[1]
    cos = jnp.tile(cos_ref[...], (1, reps))
    sin = jnp.tile(sin_ref[...], (1, reps))
    lane = lax.broadcasted_iota(jnp.int32, q.shape, 1)
    first_half = (lane & half) == 0
    grp = grp_ref[...]
    gw = grp.shape[0]

    def headnorm_rope(z, g):
        z2 = z * z
        hi = z2.astype(BF16)
        lo = (z2 - hi.astype(F32)).astype(BF16)
        parts = []
        for s in range(D // gw):
            sl = slice(s * gw, (s + 1) * gw)
            parts.append(_dot(hi[:, sl], grp) + _dot(lo[:, sl], grp))
        ss = jnp.concatenate(parts, axis=1)
        r = lax.rsqrt(ss * (1.0 / head_dim) + EPS)
        zn = (z * r) * g
        sw = jnp.where(first_half, pltpu.roll(zn, D - half, 1), pltpu.roll(zn, half, 1))
        return zn * cos + sw * sin

    qr = headnorm_rope(q, qg_ref[...])
    kr = headnorm_rope(k, kg_ref[...])
    k_ref[...] = kr
    v_ref[...] = v
    hw = 2 * head_dim
    for dst, val in ((qb_ref, qr * (head_dim ** -0.5)), (kb_ref, kr), (vb_ref, v)):
        val = val.astype(BF16)
        for h in range(D // hw):
            dst[h] = val[:, h * hw:(h + 1) * hw]


def _qkv(x, g1, w_in_bf, qg_row, kg_row, cos_t, sin_t, grp, tm, head_dim):
    B, T, D = x.shape
    nt = T // tm
    hw = 2 * head_dim
    nh = D // hw
    wspec = lambda j: pl.BlockSpec((D, D), lambda b, t, j=j: (0, j))
    tile = pl.BlockSpec((None, tm, D), lambda b, t: (b, t, 0))
    row = pl.BlockSpec((1, D), lambda b, t: (0, 0))
    tab = pl.BlockSpec((tm, cos_t.shape[1]), lambda b, t: (t, 0))
    return pl.pallas_call(
        functools.partial(_qkv_kernel, head_dim=head_dim),
        grid=(B, nt),
        in_specs=[tile, row, wspec(3), wspec(4), wspec(5), row, row, tab, tab,
                  pl.BlockSpec(grp.shape, lambda b, t: (0, 0))],
        out_specs=[tile] * 2 + [pl.BlockSpec((None, nh, tm, hw), lambda b, t: (b, 0, t, 0))] * 3,
        out_shape=[jax.ShapeDtypeStruct((B, T, D), F32)] * 2 + [jax.ShapeDtypeStruct((B, nh, T, hw), BF16)] * 3,
        compiler_params=_cparams(("arbitrary", "arbitrary")),
        name="qkv",
    )(x, g1, w_in_bf, w_in_bf, w_in_bf, qg_row, kg_row, cos_t, sin_t, grp)


def _stack_halves(q, head_dim):
    lane = lax.broadcasted_iota(jnp.int32, q.shape, 1)
    zero = jnp.zeros_like(q)
    return jnp.concatenate([jnp.where(lane < head_dim, q, zero), jnp.where(lane >= head_dim, q, zero)], axis=0)


def _lambda_value(lam_ref, lam0):
    l = lam_ref[...]
    s1 = jnp.sum(l[0:1, :] * l[1:2, :], axis=-1, keepdims=True)
    s2 = jnp.sum(l[2:3, :] * l[3:4, :], axis=-1, keepdims=True)
    return jnp.exp(s1) - jnp.exp(s2) + lam0


def _attn_finish(acc, l, tq, lam, sg, lam0):
    o = acc[:tq] / l[:tq] - lam * (acc[tq:] / l[tq:])
    return _rmsnorm_rows(o, sg) * (1.0 - lam0)


def _attn_prompt_kernel(q_ref, k_ref, v_ref, lam_ref, sg_ref, o_ref, *, tq, head_dim, lam0):
    qi = pl.program_id(2)
    qs = _stack_halves(q_ref[...], head_dim)

    def block(j, carry, masked):
        m, l, acc = carry
        kb = k_ref[j * tq:(j + 1) * tq, :]
        vb = v_ref[j * tq:(j + 1) * tq, :]
        s = _dot_nt(qs, kb)
        if masked:
            r = lax.broadcasted_iota(jnp.int32, s.shape, 0)
            c = lax.broadcasted_iota(jnp.int32, s.shape, 1)
            rr = jnp.where(r >= tq, r - tq, r)
            vis = (c >> CHUNK_SHIFT) <= (rr >> CHUNK_SHIFT)
            s = jnp.where(vis, s, -1e30)
        m_new = jnp.maximum(m, jnp.max(s, axis=-1, keepdims=True))
        alpha = jnp.exp(m - m_new)
        p = jnp.exp(s - m_new)
        l = alpha * l + jnp.sum(p, axis=-1, keepdims=True)
        acc = alpha * acc + _dot(p.astype(BF16), vb)
        return m_new, l, acc

    init = (jnp.full((2 * tq, 1), -1e30, F32), jnp.zeros((2 * tq, 1), F32),
            jnp.zeros((2 * tq, v_ref.shape[1]), F32))
    for n_full in range(k_ref.shape[0] // tq):
        @pl.when(qi == n_full)
        def _(n_full=n_full):
            carry = init
            for j in range(n_full):
                carry = block(j, carry, False)
            _, l, acc = block(n_full, carry, True)
            o_ref[...] = _attn_finish(acc, l, tq, _lambda_value(lam_ref, lam0), sg_ref[...],
                                      lam0).astype(o_ref.dtype)


def _attn_prompt(qb, kb, vb, lam4, sg_row, n_heads, tq, lam0):
    B, n_heads_, T, hw = qb.shape
    assert n_heads_ == n_heads
    D = n_heads * hw
    head_dim = hw // 2
    assert tq % CHUNK == 0 and T % tq == 0
    return pl.pallas_call(
        functools.partial(_attn_prompt_kernel, tq=tq, head_dim=head_dim, lam0=lam0),
        grid=(B, n_heads, T // tq),
        in_specs=[pl.BlockSpec((None, None, tq, hw), lambda b, h, i: (b, h, i, 0)),
                  pl.BlockSpec((None, None, T, hw), lambda b, h, i: (b, h, 0, 0)),
                  pl.BlockSpec((None, None, T, hw), lambda b, h, i: (b, h, 0, 0)),
                  pl.BlockSpec(lam4.shape, lambda b, h, i: (0, 0)),
                  pl.BlockSpec((1, hw), lambda b, h, i: (0, 0))],
        out_specs=pl.BlockSpec((None, tq, hw), lambda b, h, i: (b, i, h)),
        out_shape=jax.ShapeDtypeStruct((B, T, D), BF16),
        compiler_params=_cparams(("arbitrary", "arbitrary", "arbitrary")),
        name="attn_prompt",
    )(qb, kb, vb, lam4, sg_row)


def _attn_sample_kernel(q_ref, kc_ref, vc_ref, kn_ref, vn_ref, lam_ref, sg_ref, o_ref, *, head_dim, lam0):
    ts = q_ref.shape[0]
    past = kc_ref.shape[0]
    qs = _stack_halves(q_ref[...], head_dim)
    sc = _dot_nt(qs, kc_ref[...].astype(BF16))
    sn = _dot_nt(qs, kn_ref[...])

    def visible(shape, k_off):
        r = lax.broadcasted_iota(jnp.int32, shape, 0)
        c = lax.broadcasted_iota(jnp.int32, shape, 1)
        q_pos = past + jnp.where(r >= ts, r - ts, r)
        return ((c + k_off) >> CHUNK_SHIFT) <= (q_pos >> CHUNK_SHIFT)

    sc = jnp.where(visible(sc.shape, 0), sc, -1e30)
    sn = jnp.where(visible(sn.shape, past), sn, -1e30)
    m = jnp.maximum(jnp.max(sc, axis=-1, keepdims=True), jnp.max(sn, axis=-1, keepdims=True))
    pc = jnp.exp(sc - m)
    pn = jnp.exp(sn - m)
    l = jnp.sum(pc, axis=-1, keepdims=True) + jnp.sum(pn, axis=-1, keepdims=True)
    acc = _dot(pc.astype(BF16), vc_ref[...].astype(BF16)) + _dot(pn.astype(BF16), vn_ref[...])
    o_ref[...] = _attn_finish(acc, l, ts, _lambda_value(lam_ref, lam0), sg_ref[...], lam0).astype(o_ref.dtype)


def _attn_sample(qb, kb, vb, cache_k2, cache_v2, lam4, sg_row, n_heads, lam0):
    B, _, Ts, hw = qb.shape
    D = n_heads * hw
    P = cache_k2.shape[1]
    new = pl.BlockSpec((None, None, Ts, hw), lambda b, h: (b, h, 0, 0))
    old = pl.BlockSpec((None, P, hw), lambda b, h: (b, 0, h))
    return pl.pallas_call(
        functools.partial(_attn_sample_kernel, head_dim=hw // 2, lam0=lam0),
        grid=(B, n_heads),
        in_specs=[new, old, old, new, new, pl.BlockSpec(lam4.shape, lambda b, h: (0, 0)),
                  pl.BlockSpec((1, hw), lambda b, h: (0, 0))],
        out_specs=pl.BlockSpec((None, Ts, hw), lambda b, h: (b, 0, h)),
        out_shape=jax.ShapeDtypeStruct((B, Ts, D), BF16),
        compiler_params=_cparams(("arbitrary", "arbitrary")),
        name="attn_sample",
    )(qb, cache_k2, cache_v2, kb, vb, lam4, sg_row)


def _merge_kernel(x_ref, gaa_ref, gb_ref, on_ref, wao_ref, wo_ref, g2_ref, x1_ref, xn2_ref):
    b = _dot(on_ref[...], wao_ref[...])
    mix = gaa_ref[...] + gb_ref[...] * b
    x1 = x_ref[...] + _dot(mix.astype(BF16), wo_ref[...])
    x1_ref[...] = x1
    xn2_ref[...] = _rmsnorm_rows(x1, g2_ref[...]).astype(BF16)


def _merge(x2, gaa2, gb2, on2, wao_bf, wo_bf, g2, tm):
    N, D = x2.shape
    tile = pl.BlockSpec((tm, D), lambda t: (t, 0))
    wfull = pl.BlockSpec((D, D), lambda t: (0, 0))
    return pl.pallas_call(
        _merge_kernel,
        grid=(N // tm,),
        in_specs=[tile, tile, tile, tile, wfull, wfull, pl.BlockSpec((1, D), lambda t: (0, 0))],
        out_specs=[tile, tile],
        out_shape=[jax.ShapeDtypeStruct((N, D), F32), jax.ShapeDtypeStruct((N, D), BF16)],
        compiler_params=_cparams(("arbitrary",)),
        name="merge",
    )(x2, gaa2, gb2, on2, wao_bf, wo_bf, g2)


def _staircase_pairs(topk):
    return [(a, b) for a in range(topk) for b in range(topk) if (a + 1) * (b + 1) <= topk]


def _peer_select_kernel(xn2_ref, wqt_ref, qg_ref, bd_ref, r1_ref, bs_ref, c0_ref, as_ref,
                        s_ref, work_ref, rank_ref, top_ref, idx_ref, tr_ref, *, nh, nk, topk):
    tm = xn2_ref.shape[0]
    dk = nk
    qp = _dot_nt(wqt_ref[...], xn2_ref[...])
    qg = qg_ref[...]

    for h in range(nh):
        lo = qp[h * dk:(h + 1) * dk, :]
        hi = qp[(nh + h) * dk:(nh + h + 1) * dk, :]
        ss = jnp.sum(lo * lo + hi * hi, axis=0, keepdims=True)
        r = lax.rsqrt(ss * (1.0 / (2 * dk)) + EPS)
        for c, part in ((0, lo), (1, hi)):
            rows = slice((c * nh + h) * dk, (c * nh + h + 1) * dk)
            s_ref[c, pl.ds(h * dk, dk), :] = (part * r) * qg[rows, :]

    for c in range(2):
        qn = s_ref[c].astype(BF16)
        s_ref[c] = _dot(bd_ref[c], qn)

    neg_inf = jnp.float32(-jnp.inf)
    n_chains = 8
    per_chain = nk // n_chains

    for c in range(2):
        work_ref[...] = s_ref[c].reshape(nk, nh, tm)
        if c == 1:
            rank_ref[...] = jnp.full((nk, nh, tm), float(topk), F32)

        def round_body(r, _, c=c):
            chains = []
            for g in range(n_chains):
                k0 = g * per_chain
                m = work_ref[k0]
                idx = jnp.full((nh, tm), k0, jnp.int32)
                for k in range(k0 + 1, k0 + per_chain):
                    w = work_ref[k]
                    gt = w > m
                    m = jnp.where(gt, w, m)
                    idx = jnp.where(gt, k, idx)
                chains.append((m, idx))
            while len(chains) > 1:
                merged = []
                for (ma, ia), (mb, ib) in zip(chains[0::2], chains[1::2]):
                    gt = mb > ma
                    merged.append((jnp.where(gt, mb, ma), jnp.where(gt, ib, ia)))
                chains = merged
            m, idx = chains[0]
            top_ref[c, pl.ds(r, 1)] = m[None]
            if c == 0:
                idx_ref[pl.ds(r, 1)] = idx[None]
            rf = lax.convert_element_type(r, F32)
            for k in range(nk):
                oh = idx == k
                work_ref[k] = jnp.where(oh, neg_inf, work_ref[k])
                if c == 1:
                    rank_ref[k] = jnp.where(oh, rf, rank_ref[k])
            return 0

        for r in range(topk):
            round_body(r, 0)

    A = [top_ref[0, a] for a in range(topk)]
    Bv = [top_ref[1, b] for b in range(topk)]

    pairs = _staircase_pairs(topk)
    cand = [A[a] + Bv[b] for (a, b) in pairs]
    n = len(pairs)
    beaten = [jnp.zeros((nh, tm), F32) for _ in range(n)]
    for p in range(n):
        ap, bp = pairs[p]
        for q in range(p + 1, n):
            aq, bq = pairs[q]
            if ap <= aq and bp <= bq:
                beaten[q] = beaten[q] + 1.0
            else:
                t = jnp.where(cand[p] >= cand[q], 1.0, 0.0)
                beaten[q] = beaten[q] + t
                beaten[p] = beaten[p] + (1.0 - t)
    sel = [jnp.where(beaten[p] < float(topk), 1.0, 0.0) for p in range(n)]

    ea = [jnp.exp(A[a] - A[0]) for a in range(topk)]
    eb = [jnp.exp(Bv[b] - Bv[0]) for b in range(topk)]
    z = jnp.zeros((nh, tm), F32)
    cnt = [jnp.zeros((nh, tm), F32) for _ in range(topk)]
    for p, (a, b) in enumerate(pairs):
        z = z + sel[p] * (ea[a] * eb[b])
        cnt[a] = cnt[a] + sel[p]
    inv_z = 1.0 / z

    def emit(out_ref, val):
        val = val.reshape(nk * nh, tm)
        for lb in range(tm // LANES):
            out_ref[lb] = val[:, lb * LANES:(lb + 1) * LANES]

    def emit_by_head(out_ref, val):
        emit(tr_ref, val)
        for lb in range(tm // LANES):
            for h in range(nh):
                out_ref[lb, h] = tr_ref[lb, pl.ds(h, nk, stride=nh), :].astype(BF16)

    s0 = s_ref[0].reshape(nk, nh, tm)
    s1 = s_ref[1].reshape(nk, nh, tm)
    emit(as_ref, jnp.exp(s0 - A[0][None]) * (0.5 * inv_z)[None])
    emit_by_head(bs_ref, jnp.exp(s1 - Bv[0][None]))
    kio = lax.broadcasted_iota(jnp.int32, (nk, nh, tm), 0)
    c0 = jnp.zeros((nk, nh, tm), F32)
    for a in range(topk):
        c0 = jnp.where(kio == idx_ref[a][None], cnt[a][None], c0)
    emit(c0_ref, c0)
    emit_by_head(r1_ref, rank_ref[...])


def _peer_select(xn2, wqt_bf, qg_rows, bd_bf, nh, nk, tm):
    N, D = xn2.shape
    R = nk * nh
    assert tm % LANES == 0
    nlb = tm // LANES
    by_row = pl.BlockSpec((nlb, R, LANES), lambda t: (t, 0, 0))
    by_head = pl.BlockSpec((nlb, nh, nk, LANES), lambda t: (t, 0, 0, 0))
    return pl.pallas_call(
        functools.partial(_peer_select_kernel, nh=nh, nk=nk, topk=PEER_TOPK),
        grid=(N // tm,),
        in_specs=[pl.BlockSpec((tm, D), lambda t: (t, 0)),
                  pl.BlockSpec(wqt_bf.shape, lambda t: (0, 0)),
                  pl.BlockSpec(qg_rows.shape, lambda t: (0, 0)),
                  pl.BlockSpec(bd_bf.shape, lambda t: (0, 0, 0))],
        out_specs=[by_head, by_head, by_row, by_row],
        out_shape=[jax.ShapeDtypeStruct((N // LANES, nh, nk, LANES), BF16)] * 2
        + [jax.ShapeDtypeStruct((N // LANES, R, LANES), F32)] * 2,
        scratch_shapes=[pltpu.VMEM((2, R, tm), F32), pltpu.VMEM((nk, nh, tm), F32),
                        pltpu.VMEM((nk, nh, tm), F32), pltpu.VMEM((2, PEER_TOPK, nh, tm), F32),
                        pltpu.VMEM((PEER_TOPK, nh, tm), jnp.int32), pltpu.VMEM((nlb, R, LANES), F32)],
        compiler_params=_cparams(("arbitrary",)),
        name="peer_select",
    )(xn2, wqt_bf, qg_rows, bd_bf)


def _gelu_twice(x):
    return x + x * lax.erf(x * (2.0 ** -0.5))


BF16_SUBLANE_ROWS = 16


def _bcast_rows_bf16(row, n):
    one = jnp.broadcast_to(row, (BF16_SUBLANE_ROWS, row.shape[1])).astype(BF16)
    return jnp.tile(one, (n // BF16_SUBLANE_ROWS, 1))


def _peer_main_kernel(xn2_ref, x1_ref, u_ref, vt_ref, r1_ref, bs_ref, c0_ref, as_ref, y_ref,
                      xt_ref, acc_ref, gl_a, gl_b, gt_a, gt_b, *, nh, nk, ne, n_blocks):
    s = pl.program_id(0)
    te = u_ref.shape[0]
    tm = xn2_ref.shape[0]
    keys_per_block = te // nk
    e1 = s % ne
    e2 = (s + ne - 1) % ne
    e3 = (s + 2 * ne - 2) % ne

    @pl.when(s == 0)
    def _():
        gl_b[...] = jnp.zeros_like(gl_b)
        gt_a[...] = jnp.zeros_like(gt_a)
        acc_ref[...] = jnp.zeros_like(acc_ref)

    @pl.when((e1 == 0) & (s < n_blocks))
    def _():
        xt_ref[...] = xn2_ref[...].astype(F32).T.astype(BF16)

    @pl.when((e3 == 0) & (s >= 2))
    def _():
        acc_ref[...] = jnp.zeros_like(acc_ref)

    zero = jnp.zeros((nk, LANES), BF16)
    one = jnp.ones((nk, LANES), BF16)

    keys_per_piece = 1
    n_pieces = keys_per_block // keys_per_piece

    def gate_piece(e, piece, lb):
        rows = []
        for ii in range(piece * keys_per_piece, (piece + 1) * keys_per_piece):
            base = pl.multiple_of((e * keys_per_block + ii) * nh, nh)
            rows.append((c0_ref[lb, pl.ds(base, nh), :], as_ref[lb, pl.ds(base, nh), :]))
        w = [zero] * keys_per_piece
        for h in range(nh):
            slab = pl.ds((lb * nh + h) * nk, nk)
            r1 = r1_ref[slab, :]
            bs = bs_ref[slab, :]
            for k, (crow, arow) in enumerate(rows):
                cb = _bcast_rows_bf16(crow[h:h + 1, :], nk)
                ab = _bcast_rows_bf16(arow[h:h + 1, :], nk)
                below = jnp.minimum(jnp.maximum(cb - r1, zero), one)
                w[k] = w[k] + (below * bs) * ab
        return w

    def step(gl_new, gl_old, gt_new, gt_old):
        D = vt_ref.shape[0]
        nlb = tm // LANES
        n_mxu = te // MXU_PIECE_ROWS
        out_rows = D // n_mxu

        def stage1(ii):
            rows = slice(ii * MXU_PIECE_ROWS, (ii + 1) * MXU_PIECE_ROWS)
            gl_new[rows, :] = _gelu_twice(_dot(u_ref[rows, :], xt_ref[...])).astype(BF16)

        def stage2(piece, lb):
            cols = slice(lb * LANES, (lb + 1) * LANES)
            for k, w in enumerate(gate_piece(e2, piece, lb)):
                ii = piece * keys_per_piece + k
                rows = slice(ii * nk, (ii + 1) * nk)
                gt_new[rows, cols] = gl_old[rows, cols] * w

        def stage3(ii):
            orow = slice(ii * out_rows, (ii + 1) * out_rows)
            acc_ref[orow, :] += _dot(vt_ref[orow, :], gt_old[...])

        mxu_work = []
        for ii in range(n_mxu):
            mxu_work += [functools.partial(stage1, ii), functools.partial(stage3, ii)]
        vpu_work = [functools.partial(stage2, piece, lb) for piece in range(n_pieces) for lb in range(nlb)]
        order = sorted([((i + 0.5) / len(mxu_work), 0, f) for i, f in enumerate(mxu_work)]
                       + [((i + 0.5) / len(vpu_work), 1, f) for i, f in enumerate(vpu_work)], key=lambda t: t[:2])
        for _, _, piece_fn in order:
            piece_fn()

    @pl.when(s % 2 == 0)
    def _():
        step(gl_a, gl_b, gt_b, gt_a)

    @pl.when(s % 2 == 1)
    def _():
        step(gl_b, gl_a, gt_a, gt_b)

    @pl.when((e3 == ne - 1) & (s >= 2))
    def _():
        y_ref[...] = x1_ref[...] + acc_ref[...].T


def _peer_main(xn2, x1, u_bf, vt_bf, r1, bs, c0, a_s, nh, nk, tm):
    N, D = xn2.shape
    ne, _, te = vt_bf.shape
    R = nk * nh
    n_blocks = (N // tm) * ne
    last = n_blocks - 1
    nlb = tm // LANES
    tile1 = lambda s: jnp.minimum(s, last) // ne
    tile2 = lambda s: jnp.clip(s - 1, 0, last) // ne
    tile3 = lambda s: jnp.clip(s - 2, 0, last) // ne
    by_row = pl.BlockSpec((nlb, R, LANES), lambda s: (tile2(s), 0, 0))
    by_head = pl.BlockSpec((nlb * nh * nk, LANES), lambda s: (tile2(s), 0))
    r1 = r1.reshape(-1, LANES)
    bs = bs.reshape(-1, LANES)
    return pl.pallas_call(
        functools.partial(_peer_main_kernel, nh=nh, nk=nk, ne=ne, n_blocks=n_blocks),
        grid=(n_blocks + 2,),
        in_specs=[pl.BlockSpec((tm, D), lambda s: (tile1(s), 0)),
                  pl.BlockSpec((tm, D), lambda s: (tile3(s), 0)),
                  pl.BlockSpec((te, D), lambda s: (s % ne, 0)),
                  pl.BlockSpec((None, D, te), lambda s: ((s + 2 * ne - 2) % ne, 0, 0)),
                  by_head, by_head, by_row, by_row],
        out_specs=pl.BlockSpec((tm, D), lambda s: (tile3(s), 0)),
        out_shape=jax.ShapeDtypeStruct((N, D), F32),
        scratch_shapes=[pltpu.VMEM((D, tm), BF16), pltpu.VMEM((D, tm), F32)]
        + [pltpu.VMEM((te, tm), BF16)] * 4,
        compiler_params=_cparams(("arbitrary",)),
        name="peer_main",
    )(xn2, x1, u_bf, vt_bf, r1, bs, c0, a_s)


def _rope_tables(pos, head_dim, width):
    half = head_dim // 2
    inv = ROPE_THETA ** (-jnp.arange(half, dtype=F32) / half)
    ang = pos.astype(F32)[:, None] * inv[None, :]
    cos = jnp.cos(ang)
    sin = jnp.sin(ang)
    reps = width // head_dim
    cos_t = jnp.tile(jnp.concatenate([cos, cos], axis=1), (1, reps))
    sin_t = jnp.tile(jnp.concatenate([-sin, sin], axis=1), (1, reps))
    return cos_t, sin_t


def _token_tile(n, pref):
    t = min(pref, n)
    while n % t:
        t //= 2
    return t


def _trunk(x, pos, conv_state, cache_k2, cache_v2, p):
    B, T, D = x.shape
    nh, hd = p["n_heads"], p["head_dim"]
    tm = _token_tile(T, 256)
    cos_t, sin_t = _rope_tables(pos, hd, 128)
    gaa, gb, new_conv = _branch_a(x, p["g1"], p["w_in"], p["b_gate"], p["conv_w"], conv_state, p["wco"], tm)
    k, v, qb, kb, vb = _qkv(x, p["g1"], p["w_in"], p["qg"], p["kg"], cos_t, sin_t, p["grp"], tm, hd)
    if cache_k2 is None:
        on = _attn_prompt(qb, kb, vb, p["lam4"], p["sg"], nh, _token_tile(T, 512), p["lam0"])
    else:
        on = _attn_sample(qb, kb, vb, cache_k2, cache_v2, p["lam4"], p["sg"], nh, p["lam0"])
    N = B * T
    tn = _token_tile(N, 256)
    x1, xn2 = _merge(x.reshape(N, D), gaa.reshape(N, D), gb.reshape(N, D), on.reshape(N, D),
                     p["wao"], p["wo"], p["g2"], tn)
    pnh, nk = p["peer_heads"], p["n_keys"]
    tp = _token_tile(N, 512)
    qg_rows = jnp.broadcast_to(p["peer_qg_col"], (p["peer_qg_col"].shape[0], tp))
    r1, bs, c0, a_s = _peer_select(xn2, p["wqt"], qg_rows, p["bd"], pnh, nk, tp)
    y = _peer_main(xn2, x1, p["u"], p["vt"], r1, bs, c0, a_s, pnh, nk, tp)
    return y.reshape(B, T, D), k, v, new_conv


def kernel(x_prompt, x_sample, cache_k, cache_v, state_conv, norm1_g, w_in, b_gate, conv_w, q_norm_g, k_norm_g,
           lam_q1, lam_k1, lam_q2, lam_k2, subln_g, w_conv_out, w_attn_out, w_o, norm2_g, peer_wq, peer_q_g,
           peer_subkeys, peer_u, peer_v):
    depth = w_in.shape[0]
    assert depth == 1
    l = LAYER_IDX
    D = x_prompt.shape[-1]
    n_heads, head_dim, v_dim = cache_k.shape[3], cache_k.shape[5], cache_v.shape[4]
    assert v_dim == 2 * head_dim and n_heads * v_dim == D and w_in.shape[2] == 8 * D
    n_keys, dk = peer_subkeys.shape[2], peer_subkeys.shape[3]
    assert n_keys == dk
    peer_heads = peer_wq.shape[2] // (2 * dk)
    past = cache_k.shape[2]
    assert past % CHUNK == 0 and (1 << CHUNK_SHIFT) == CHUNK

    eye_h = jnp.eye(peer_heads, dtype=F32)
    grp = np.kron(np.eye(256 // head_dim), np.ones((head_dim, head_dim))).astype(np.float32)
    p = dict(
        n_heads=n_heads, head_dim=head_dim, peer_heads=peer_heads, n_keys=n_keys, lam0=_lambda_init(l),
        g1=norm1_g[l][None, :], w_in=w_in[l].astype(BF16), b_gate=b_gate[l], conv_w=conv_w[l],
        wco=w_conv_out[l].astype(BF16), wao=w_attn_out[l].astype(BF16), wo=w_o[l].astype(BF16),
        qg=jnp.tile(q_norm_g[l], D // head_dim)[None, :], kg=jnp.tile(k_norm_g[l], D // head_dim)[None, :],
        grp=jnp.asarray(grp, BF16),
        lam4=jnp.stack([lam_q1[l], lam_k1[l], lam_q2[l], lam_k2[l]]),
        sg=subln_g[l][None, :], g2=norm2_g[l][None, :],
        wqt=peer_wq[l].reshape(D, peer_heads, 2, dk).transpose(2, 1, 3, 0).reshape(2 * peer_heads * dk, D).astype(BF16),
        peer_qg_col=jnp.broadcast_to(peer_q_g[l].reshape(2, 1, dk), (2, peer_heads, dk)).reshape(-1, 1),
        bd=jnp.einsum("ckd,hg->ckhgd", peer_subkeys[l], eye_h).reshape(2, n_keys * peer_heads, peer_heads * dk).astype(BF16),
        u=peer_u[l].astype(BF16),
        vt=peer_v[l].reshape(-1, PEER_EXPERT_BLOCK, D).transpose(0, 2, 1).astype(BF16),
    )

    B, T, _ = x_prompt.shape
    Bs, Ts, _ = x_sample.shape
    pos_p = jnp.arange(T, dtype=jnp.int32)
    pos_s = past + jnp.arange(Ts, dtype=jnp.int32)
    zero_conv = jnp.zeros((B, 2, D), x_prompt.dtype)
    y_p, k_p, v_p, c_p = _trunk(x_prompt, pos_p, zero_conv, None, None, p)
    y_s, k_s, v_s, c_s = _trunk(x_sample, pos_s, state_conv[l], cache_k[l].reshape(Bs, past, D),
                                cache_v[l].reshape(Bs, past, D), p)
    return (y_p, y_s,
            k_p.reshape(1, B, T, n_heads, 2, head_dim), v_p.reshape(1, B, T, n_heads, v_dim), c_p[None],
            k_s.reshape(1, Bs, Ts, n_heads, 2, head_dim), v_s.reshape(1, Bs, Ts, n_heads, v_dim), c_s[None])
```

```python
import functools
import math

import jax
import jax.numpy as jnp
import numpy as np
from jax import lax
from jax.experimental import pallas as pl
from jax.experimental.pallas import tpu as pltpu

EPS = 1e-6
CHUNK = 64
CHUNK_SHIFT = 6
ROPE_THETA = 10000.0
PEER_TOPK = 16
LAYER_IDX = 0

V7X_VMEM_LIMIT = 56 * 1024 * 1024
LANES = 128
PEER_EXPERT_BLOCK = 2048
MXU_PIECE_ROWS = 1024

F32 = jnp.float32
BF16 = jnp.bfloat16


def _lambda_init(layer_idx):
    return 0.8 - 0.6 * math.exp(-0.3 * layer_idx)


def _dot(a, b):
    return jnp.dot(a, b, preferred_element_type=F32)


def _dot_nt(a, b):
    return lax.dot_general(a, b, (((1,), (1,)), ((), ())), preferred_element_type=F32)


def _rmsnorm_rows(x, g):
    r = lax.rsqrt(jnp.mean(x * x, axis=-1, keepdims=True) + EPS)
    return (x * r) * g


def _cparams(sem):
    return pltpu.CompilerParams(dimension_semantics=sem, vmem_limit_bytes=V7X_VMEM_LIMIT)


def _branch_a_kernel(x_ref, g1_ref, wb_ref, wc_ref, wh_ref, wga_ref, wgb_ref, bg_ref, cw_ref, cs_ref,
                     wco_ref, gaa_ref, gb_ref, nc_ref, carry_ref):
    t = pl.program_id(1)

    @pl.when(t == 0)
    def _():
        carry_ref[...] = cs_ref[...]

    x = x_ref[...]
    xn = _rmsnorm_rows(x, g1_ref[...]).astype(BF16)
    hb = _dot(xn, wb_ref[...])
    hc = _dot(xn, wc_ref[...])
    hh = _dot(xn, wh_ref[...])
    u = hc * hh
    tm = u.shape[0]
    row = lax.broadcasted_iota(jnp.int32, u.shape, 0)
    c0 = carry_ref[0:1, :]
    c1 = carry_ref[1:2, :]
    u1 = jnp.where(row == 0, c1, pltpu.roll(u, 1, 0))
    u2 = jnp.where(row == 0, c0, jnp.where(row == 1, c1, pltpu.roll(u, 2, 0)))
    y = cw_ref[0:1, :] * u2 + cw_ref[1:2, :] * u1 + cw_ref[2:3, :] * u
    conv_y = hb * y
    new_c = u[tm - 2:tm, :]
    carry_ref[...] = new_c
    nc_ref[...] = new_c
    a = _dot(conv_y.astype(BF16), wco_ref[...])
    ga = jax.nn.sigmoid(_dot(xn, wga_ref[...]) + bg_ref[0:1, :])
    gb = jax.nn.sigmoid(_dot(xn, wgb_ref[...]) + bg_ref[1:2, :])
    gaa_ref[...] = ga * a
    gb_ref[...] = gb


def _branch_a(x, g1, w_in_bf, b_gate, conv_w, conv_state, wco_bf, tm):
    B, T, D = x.shape
    nt = T // tm
    wspec = lambda j: pl.BlockSpec((D, D), lambda b, t, j=j: (0, j))
    tile = pl.BlockSpec((None, tm, D), lambda b, t: (b, t, 0))
    full2 = lambda r: pl.BlockSpec((r, D), lambda b, t: (0, 0))
    return pl.pallas_call(
        _branch_a_kernel,
        grid=(B, nt),
        in_specs=[tile, full2(1), wspec(0), wspec(1), wspec(2), wspec(6), wspec(7), full2(2), full2(3),
                  pl.BlockSpec((None, 2, D), lambda b, t: (b, 0, 0)), pl.BlockSpec((D, D), lambda b, t: (0, 0))],
        out_specs=[tile, tile, pl.BlockSpec((None, 2, D), lambda b, t: (b, 0, 0))],
        out_shape=[jax.ShapeDtypeStruct((B, T, D), F32), jax.ShapeDtypeStruct((B, T, D), F32),
                   jax.ShapeDtypeStruct((B, 2, D), F32)],
        scratch_shapes=[pltpu.VMEM((2, D), F32)],
        compiler_params=_cparams(("arbitrary", "arbitrary")),
        name="branch_a",
    )(x, g1, w_in_bf, w_in_bf, w_in_bf, w_in_bf, w_in_bf, b_gate, conv_w, conv_state, wco_bf)


def _qkv_kernel(x_ref, g1_ref, wq_ref, wk_ref, wv_ref, qg_ref, kg_ref, cos_ref, sin_ref, grp_ref,
                k_ref, v_ref, qb_ref, kb_ref, vb_ref, *, head_dim):
    x = x_ref[...]
    xn = _rmsnorm_rows(x, g1_ref[...]).astype(BF16)
    q = _dot(xn, wq_ref[...])
    k = _dot(xn, wk_ref[...])
    v = _dot(xn, wv_ref[...])
    D = q.shape[1]
    half = head_dim // 2
    reps = D // cos_ref.shape[1]
    cos = jnp.tile(cos_ref[...], (1, reps))
    sin = jnp.tile(sin_ref[...], (1, reps))
    lane = lax.broadcasted_iota(jnp.int32, q.shape, 1)
    first_half = (lane & half) == 0
    grp = grp_ref[...]
    gw = grp.shape[0]

    def headnorm_rope(z, g):
        z2 = z * z
        hi = z2.astype(BF16)
        lo = (z2 - hi.astype(F32)).astype(BF16)
        parts = []
        for s in range(D // gw):
            sl = slice(s * gw, (s + 1) * gw)
            parts.append(_dot(hi[:, sl], grp) + _dot(lo[:, sl], grp))
        ss = jnp.concatenate(parts, axis=1)
        r = lax.rsqrt(ss * (1.0 / head_dim) + EPS)
        zn = (z * r) * g
        sw = jnp.where(first_half, pltpu.roll(zn, D - half, 1), pltpu.roll(zn, half, 1))
        return zn * cos + sw * sin

    qr = headnorm_rope(q, qg_ref[...])
    kr = headnorm_rope(k, kg_ref[...])
    k_ref[...] = kr
    v_ref[...] = v
    hw = 2 * head_dim
    for dst, val in ((qb_ref, qr * (head_dim ** -0.5)), (kb_ref, kr), (vb_ref, v)):
        val = val.astype(BF16)
        for h in range(D // hw):
            dst[h] = val[:, h * hw:(h + 1) * hw]


def _qkv(x, g1, w_in_bf, qg_row, kg_row, cos_t, sin_t, grp, tm, head_dim):
    B, T, D = x.shape
    nt = T // tm
    hw = 2 * head_dim
    nh = D // hw
    wspec = lambda j: pl.BlockSpec((D, D), lambda b, t, j=j: (0, j))
    tile = pl.BlockSpec((None, tm, D), lambda b, t: (b, t, 0))
    row = pl.BlockSpec((1, D), lambda b, t: (0, 0))
    tab = pl.BlockSpec((tm, cos_t.shape[1]), lambda b, t: (t, 0))
    return pl.pallas_call(
        functools.partial(_qkv_kernel, head_dim=head_dim),
        grid=(B, nt),
        in_specs=[tile, row, wspec(3), wspec(4), wspec(5), row, row, tab, tab,
                  pl.BlockSpec(grp.shape, lambda b, t: (0, 0))],
        out_specs=[tile] * 2 + [pl.BlockSpec((None, nh, tm, hw), lambda b, t: (b, 0, t, 0))] * 3,
        out_shape=[jax.ShapeDtypeStruct((B, T, D), F32)] * 2 + [jax.ShapeDtypeStruct((B, nh, T, hw), BF16)] * 3,
        compiler_params=_cparams(("arbitrary", "arbitrary")),
        name="qkv",
    )(x, g1, w_in_bf, w_in_bf, w_in_bf, qg_row, kg_row, cos_t, sin_t, grp)


def _stack_halves(q, head_dim):
    lane = lax.broadcasted_iota(jnp.int32, q.shape, 1)
    zero = jnp.zeros_like(q)
    return jnp.concatenate([jnp.where(lane < head_dim, q, zero), jnp.where(lane >= head_dim, q, zero)], axis=0)


def _lambda_value(lam_ref, lam0):
    l = lam_ref[...]
    s1 = jnp.sum(l[0:1, :] * l[1:2, :], axis=-1, keepdims=True)
    s2 = jnp.sum(l[2:3, :] * l[3:4, :], axis=-1, keepdims=True)
    return jnp.exp(s1) - jnp.exp(s2) + lam0


def _attn_finish(acc, l, tq, lam, sg, lam0):
    o = acc[:tq] / l[:tq] - lam * (acc[tq:] / l[tq:])
    return _rmsnorm_rows(o, sg) * (1.0 - lam0)


def _attn_prompt_kernel(q_ref, k_ref, v_ref, lam_ref, sg_ref, o_ref, *, tq, head_dim, lam0):
    qi = pl.program_id(2)
    qs = _stack_halves(q_ref[...], head_dim)

    def block(j, carry, masked):
        m, l, acc = carry
        kb = k_ref[j * tq:(j + 1) * tq, :]
        vb = v_ref[j * tq:(j + 1) * tq, :]
        s = _dot_nt(qs, kb)
        if masked:
            r = lax.broadcasted_iota(jnp.int32, s.shape, 0)
            c = lax.broadcasted_iota(jnp.int32, s.shape, 1)
            rr = jnp.where(r >= tq, r - tq, r)
            vis = (c >> CHUNK_SHIFT) <= (rr >> CHUNK_SHIFT)
            s = jnp.where(vis, s, -1e30)
        m_new = jnp.maximum(m, jnp.max(s, axis=-1, keepdims=True))
        alpha = jnp.exp(m - m_new)
        p = jnp.exp(s - m_new)
        l = alpha * l + jnp.sum(p, axis=-1, keepdims=True)
        acc = alpha * acc + _dot(p.astype(BF16), vb)
        return m_new, l, acc

    init = (jnp.full((2 * tq, 1), -1e30, F32), jnp.zeros((2 * tq, 1), F32),
            jnp.zeros((2 * tq, v_ref.shape[1]), F32))
    for n_full in range(k_ref.shape[0] // tq):
        @pl.when(qi == n_full)
        def _(n_full=n_full):
            carry = init
            for j in range(n_full):
                carry = block(j, carry, False)
            _, l, acc = block(n_full, carry, True)
            o_ref[...] = _attn_finish(acc, l, tq, _lambda_value(lam_ref, lam0), sg_ref[...],
                                      lam0).astype(o_ref.dtype)


def _attn_prompt(qb, kb, vb, lam4, sg_row, n_heads, tq, lam0):
    B, n_heads_, T, hw = qb.shape
    assert n_heads_ == n_heads
    D = n_heads * hw
    head_dim = hw // 2
    assert tq % CHUNK == 0 and T % tq == 0
    return pl.pallas_call(
        functools.partial(_attn_prompt_kernel, tq=tq, head_dim=head_dim, lam0=lam0),
        grid=(B, n_heads, T // tq),
        in_specs=[pl.BlockSpec((None, None, tq, hw), lambda b, h, i: (b, h, i, 0)),
                  pl.BlockSpec((None, None, T, hw), lambda b, h, i: (b, h, 0, 0)),
                  pl.BlockSpec((None, None, T, hw), lambda b, h, i: (b, h, 0, 0)),
                  pl.BlockSpec(lam4.shape, lambda b, h, i: (0, 0)),
                  pl.BlockSpec((1, hw), lambda b, h, i: (0, 0))],
        out_specs=pl.BlockSpec((None, tq, hw), lambda b, h, i: (b, i, h)),
        out_shape=jax.ShapeDtypeStruct((B, T, D), BF16),
        compiler_params=_cparams(("arbitrary", "arbitrary", "arbitrary")),
        name="attn_prompt",
    )(qb, kb, vb, lam4, sg_row)


def _attn_sample_kernel(q_ref, kc_ref, vc_ref, kn_ref, vn_ref, lam_ref, sg_ref, o_ref, *, head_dim, lam0):
    ts = q_ref.shape[0]
    past = kc_ref.shape[0]
    qs = _stack_halves(q_ref[...], head_dim)
    sc = _dot_nt(qs, kc_ref[...].astype(BF16))
    sn = _dot_nt(qs, kn_ref[...])

    def visible(shape, k_off):
        r = lax.broadcasted_iota(jnp.int32, shape, 0)
        c = lax.broadcasted_iota(jnp.int32, shape, 1)
        q_pos = past + jnp.where(r >= ts, r - ts, r)
        return ((c + k_off) >> CHUNK_SHIFT) <= (q_pos >> CHUNK_SHIFT)

    sc = jnp.where(visible(sc.shape, 0), sc, -1e30)
    sn = jnp.where(visible(sn.shape, past), sn, -1e30)
    m = jnp.maximum(jnp.max(sc, axis=-1, keepdims=True), jnp.max(sn, axis=-1, keepdims=True))
    pc = jnp.exp(sc - m)
    pn = jnp.exp(sn - m)
    l = jnp.sum(pc, axis=-1, keepdims=True) + jnp.sum(pn, axis=-1, keepdims=True)
    acc = _dot(pc.astype(BF16), vc_ref[...].astype(BF16)) + _dot(pn.astype(BF16), vn_ref[...])
    o_ref[...] = _attn_finish(acc, l, ts, _lambda_value(lam_ref, lam0), sg_ref[...], lam0).astype(o_ref.dtype)


def _attn_sample(qb, kb, vb, cache_k2, cache_v2, lam4, sg_row, n_heads, lam0):
    B, _, Ts, hw = qb.shape
    D = n_heads * hw
    P = cache_k2.shape[1]
    new = pl.BlockSpec((None, None, Ts, hw), lambda b, h: (b, h, 0, 0))
    old = pl.BlockSpec((None, P, hw), lambda b, h: (b, 0, h))
    return pl.pallas_call(
        functools.partial(_attn_sample_kernel, head_dim=hw // 2, lam0=lam0),
        grid=(B, n_heads),
        in_specs=[new, old, old, new, new, pl.BlockSpec(lam4.shape, lambda b, h: (0, 0)),
                  pl.BlockSpec((1, hw), lambda b, h: (0, 0))],
        out_specs=pl.BlockSpec((None, Ts, hw), lambda b, h: (b, 0, h)),
        out_shape=jax.ShapeDtypeStruct((B, Ts, D), BF16),
        compiler_params=_cparams(("arbitrary", "arbitrary")),
        name="attn_sample",
    )(qb, cache_k2, cache_v2, kb, vb, lam4, sg_row)


def _merge_kernel(x_ref, gaa_ref, gb_ref, on_ref, wao_ref, wo_ref, g2_ref, x1_ref, xn2_ref):
    b = _dot(on_ref[...], wao_ref[...])
    mix = gaa_ref[...] + gb_ref[...] * b
    x1 = x_ref[...] + _dot(mix.astype(BF16), wo_ref[...])
    x1_ref[...] = x1
    xn2_ref[...] = _rmsnorm_rows(x1, g2_ref[...]).astype(BF16)


def _merge(x2, gaa2, gb2, on2, wao_bf, wo_bf, g2, tm):
    N, D = x2.shape
    tile = pl.BlockSpec((tm, D), lambda t: (t, 0))
    wfull = pl.BlockSpec((D, D), lambda t: (0, 0))
    return pl.pallas_call(
        _merge_kernel,
        grid=(N // tm,),
        in_specs=[tile, tile, tile, tile, wfull, wfull, pl.BlockSpec((1, D), lambda t: (0, 0))],
        out_specs=[tile, tile],
        out_shape=[jax.ShapeDtypeStruct((N, D), F32), jax.ShapeDtypeStruct((N, D), BF16)],
        compiler_params=_cparams(("arbitrary",)),
        name="merge",
    )(x2, gaa2, gb2, on2, wao_bf, wo_bf, g2)


def _staircase_pairs(topk):
    return [(a, b) for a in range(topk) for b in range(topk) if (a + 1) * (b + 1) <= topk]


def _peer_select_kernel(xn2_ref, wqt_ref, qg_ref, bd_ref, r1_ref, bs_ref, c0_ref, as_ref,
                        s_ref, work_ref, rank_ref, top_ref, idx_ref, tr_ref, *, nh, nk, topk):
    tm = xn2_ref.shape[0]
    dk = nk
    qp = _dot_nt(wqt_ref[...], xn2_ref[...])
    qg = qg_ref[...]

    for h in range(nh):
        lo = qp[h * dk:(h + 1) * dk, :]
        hi = qp[(nh + h) * dk:(nh + h + 1) * dk, :]
        ss = jnp.sum(lo * lo + hi * hi, axis=0, keepdims=True)
        r = lax.rsqrt(ss * (1.0 / (2 * dk)) + EPS)
        for c, part in ((0, lo), (1, hi)):
            rows = slice((c * nh + h) * dk, (c * nh + h + 1) * dk)
            s_ref[c, pl.ds(h * dk, dk), :] = (part * r) * qg[rows, :]

    for c in range(2):
        qn = s_ref[c].astype(BF16)
        s_ref[c] = _dot(bd_ref[c], qn)

    neg_inf = jnp.float32(-jnp.inf)
    n_chains = 8
    per_chain = nk // n_chains

    for c in range(2):
        work_ref[...] = s_ref[c].reshape(nk, nh, tm)
        if c == 1:
            rank_ref[...] = jnp.full((nk, nh, tm), float(topk), F32)

        def round_body(r, _, c=c):
            chains = []
            for g in range(n_chains):
                k0 = g * per_chain
                m = work_ref[k0]
                idx = jnp.full((nh, tm), k0, jnp.int32)
                for k in range(k0 + 1, k0 + per_chain):
                    w = work_ref[k]
                    gt = w > m
                    m = jnp.where(gt, w, m)
                    idx = jnp.where(gt, k, idx)
                chains.append((m, idx))
            while len(chains) > 1:
                merged = []
                for (ma, ia), (mb, ib) in zip(chains[0::2], chains[1::2]):
                    gt = mb > ma
                    merged.append((jnp.where(gt, mb, ma), jnp.where(gt, ib, ia)))
                chains = merged
            m, idx = chains[0]
            top_ref[c, pl.ds(r, 1)] = m[None]
            if c == 0:
                idx_ref[pl.ds(r, 1)] = idx[None]
            rf = lax.convert_element_type(r, F32)
            for k in range(nk):
                oh = idx == k
                work_ref[k] = jnp.where(oh, neg_inf, work_ref[k])
                if c == 1:
                    rank_ref[k] = jnp.where(oh, rf, rank_ref[k])
            return 0

        def four_rounds(i, _):
            for q in range(4):
                round_body(i * 4 + q, 0)
            return 0

        lax.fori_loop(0, topk // 4, four_rounds, 0)

    A = [top_ref[0, a] for a in range(topk)]
    Bv = [top_ref[1, b] for b in range(topk)]

    pairs = _staircase_pairs(topk)
    cand = [A[a] + Bv[b] for (a, b) in pairs]
    n = len(pairs)
    beaten = [jnp.zeros((nh, tm), F32) for _ in range(n)]
    for p in range(n):
        ap, bp = pairs[p]
        for q in range(p + 1, n):
            aq, bq = pairs[q]
            if ap <= aq and bp <= bq:
                beaten[q] = beaten[q] + 1.0
            else:
                t = jnp.where(cand[p] >= cand[q], 1.0, 0.0)
                beaten[q] = beaten[q] + t
                beaten[p] = beaten[p] + (1.0 - t)
    sel = [jnp.where(beaten[p] < float(topk), 1.0, 0.0) for p in range(n)]

    ea = [jnp.exp(A[a] - A[0]) for a in range(topk)]
    eb = [jnp.exp(Bv[b] - Bv[0]) for b in range(topk)]
    z = jnp.zeros((nh, tm), F32)
    cnt = [jnp.zeros((nh, tm), F32) for _ in range(topk)]
    for p, (a, b) in enumerate(pairs):
        z = z + sel[p] * (ea[a] * eb[b])
        cnt[a] = cnt[a] + sel[p]
    inv_z = 1.0 / z

    def emit(out_ref, val):
        val = val.reshape(nk * nh, tm)
        for lb in range(tm // LANES):
            out_ref[lb] = val[:, lb * LANES:(lb + 1) * LANES]

    def emit_by_head(out_ref, val):
        emit(tr_ref, val)
        for lb in range(tm // LANES):
            for h in range(nh):
                out_ref[lb, h] = tr_ref[lb, pl.ds(h, nk, stride=nh), :].astype(BF16)

    s0 = s_ref[0].reshape(nk, nh, tm)
    s1 = s_ref[1].reshape(nk, nh, tm)
    emit(as_ref, jnp.exp(s0 - A[0][None]) * (0.5 * inv_z)[None])
    emit_by_head(bs_ref, jnp.exp(s1 - Bv[0][None]))
    kio = lax.broadcasted_iota(jnp.int32, (nk, nh, tm), 0)
    c0 = jnp.zeros((nk, nh, tm), F32)
    for a in range(topk):
        c0 = jnp.where(kio == idx_ref[a][None], cnt[a][None], c0)
    emit(c0_ref, c0)
    emit_by_head(r1_ref, rank_ref[...])


def _peer_select(xn2, wqt_bf, qg_rows, bd_bf, nh, nk, tm):
    N, D = xn2.shape
    R = nk * nh
    assert tm % LANES == 0
    nlb = tm // LANES
    by_row = pl.BlockSpec((nlb, R, LANES), lambda t: (t, 0, 0))
    by_head = pl.BlockSpec((nlb, nh, nk, LANES), lambda t: (t, 0, 0, 0))
    return pl.pallas_call(
        functools.partial(_peer_select_kernel, nh=nh, nk=nk, topk=PEER_TOPK),
        grid=(N // tm,),
        in_specs=[pl.BlockSpec((tm, D), lambda t: (t, 0)),
                  pl.BlockSpec(wqt_bf.shape, lambda t: (0, 0)),
                  pl.BlockSpec(qg_rows.shape, lambda t: (0, 0)),
                  pl.BlockSpec(bd_bf.shape, lambda t: (0, 0, 0))],
        out_specs=[by_head, by_head, by_row, by_row],
        out_shape=[jax.ShapeDtypeStruct((N // LANES, nh, nk, LANES), BF16)] * 2
        + [jax.ShapeDtypeStruct((N // LANES, R, LANES), F32)] * 2,
        scratch_shapes=[pltpu.VMEM((2, R, tm), F32), pltpu.VMEM((nk, nh, tm), F32),
                        pltpu.VMEM((nk, nh, tm), F32), pltpu.VMEM((2, PEER_TOPK, nh, tm), F32),
                        pltpu.VMEM((PEER_TOPK, nh, tm), jnp.int32), pltpu.VMEM((nlb, R, LANES), F32)],
        compiler_params=_cparams(("arbitrary",)),
        name="peer_select",
    )(xn2, wqt_bf, qg_rows, bd_bf)


def _gelu_twice(x):
    return x + x * lax.erf(x * (2.0 ** -0.5))


BF16_SUBLANE_ROWS = 16


def _bcast_rows_bf16(row, n):
    one = jnp.broadcast_to(row, (BF16_SUBLANE_ROWS, row.shape[1])).astype(BF16)
    return jnp.tile(one, (n // BF16_SUBLANE_ROWS, 1))


def _peer_main_kernel(xn2_ref, x1_ref, u_ref, vt_ref, r1_ref, bs_ref, c0_ref, as_ref, y_ref,
                      xt_ref, acc_ref, gl_a, gl_b, gt_a, gt_b, *, nh, nk, ne, n_blocks):
    s = pl.program_id(0)
    te = u_ref.shape[0]
    tm = xn2_ref.shape[0]
    keys_per_block = te // nk
    e1 = s % ne
    e2 = (s + ne - 1) % ne
    e3 = (s + 2 * ne - 2) % ne

    @pl.when(s == 0)
    def _():
        gl_b[...] = jnp.zeros_like(gl_b)
        gt_a[...] = jnp.zeros_like(gt_a)
        acc_ref[...] = jnp.zeros_like(acc_ref)

    @pl.when((e1 == 0) & (s < n_blocks))
    def _():
        xt_ref[...] = xn2_ref[...].astype(F32).T.astype(BF16)

    @pl.when((e3 == 0) & (s >= 2))
    def _():
        acc_ref[...] = jnp.zeros_like(acc_ref)

    zero = jnp.zeros((nk, LANES), BF16)
    one = jnp.ones((nk, LANES), BF16)

    keys_per_piece = 1
    n_pieces = keys_per_block // keys_per_piece

    def gate_piece(e, piece, lb):
        rows = []
        for ii in range(piece * keys_per_piece, (piece + 1) * keys_per_piece):
            base = pl.multiple_of((e * keys_per_block + ii) * nh, nh)
            rows.append((c0_ref[lb, pl.ds(base, nh), :], as_ref[lb, pl.ds(base, nh), :]))
        w = [zero] * keys_per_piece
        for h in range(nh):
            slab = pl.ds((lb * nh + h) * nk, nk)
            r1 = r1_ref[slab, :]
            bs = bs_ref[slab, :]
            for k, (crow, arow) in enumerate(rows):
                cb = _bcast_rows_bf16(crow[h:h + 1, :], nk)
                ab = _bcast_rows_bf16(arow[h:h + 1, :], nk)
                below = jnp.minimum(jnp.maximum(cb - r1, zero), one)
                w[k] = w[k] + (below * bs) * ab
        return w

    def step(gl_new, gl_old, gt_new, gt_old):
        D = vt_ref.shape[0]
        nlb = tm // LANES
        n_mxu = te // MXU_PIECE_ROWS
        out_rows = D // n_mxu

        def stage1(ii):
            rows = slice(ii * MXU_PIECE_ROWS, (ii + 1) * MXU_PIECE_ROWS)
            gl_new[rows, :] = _gelu_twice(_dot(u_ref[rows, :], xt_ref[...])).astype(BF16)

        def stage2(piece, lb):
            cols = slice(lb * LANES, (lb + 1) * LANES)
            for k, w in enumerate(gate_piece(e2, piece, lb)):
                ii = piece * keys_per_piece + k
                rows = slice(ii * nk, (ii + 1) * nk)
                gt_new[rows, cols] = gl_old[rows, cols] * w

        def stage3(ii):
            orow = slice(ii * out_rows, (ii + 1) * out_rows)
            acc_ref[orow, :] += _dot(vt_ref[orow, :], gt_old[...])

        mxu_work = []
        for ii in range(n_mxu):
            mxu_work += [functools.partial(stage1, ii), functools.partial(stage3, ii)]
        vpu_work = [functools.partial(stage2, piece, lb) for piece in range(n_pieces) for lb in range(nlb)]
        order = sorted([((i + 0.5) / len(mxu_work), 0, f) for i, f in enumerate(mxu_work)]
                       + [((i + 0.5) / len(vpu_work), 1, f) for i, f in enumerate(vpu_work)], key=lambda t: t[:2])
        for _, _, piece_fn in order:
            piece_fn()

    @pl.when(s % 2 == 0)
    def _():
        step(gl_a, gl_b, gt_b, gt_a)

    @pl.when(s % 2 == 1)
    def _():
        step(gl_b, gl_a, gt_a, gt_b)

    @pl.when((e3 == ne - 1) & (s >= 2))
    def _():
        y_ref[...] = x1_ref[...] + acc_ref[...].T


def _peer_main(xn2, x1, u_bf, vt_bf, r1, bs, c0, a_s, nh, nk, tm):
    N, D = xn2.shape
    ne, _, te = vt_bf.shape
    R = nk * nh
    n_blocks = (N // tm) * ne
    last = n_blocks - 1
    nlb = tm // LANES
    tile1 = lambda s: jnp.minimum(s, last) // ne
    tile2 = lambda s: jnp.clip(s - 1, 0, last) // ne
    tile3 = lambda s: jnp.clip(s - 2, 0, last) // ne
    by_row = pl.BlockSpec((nlb, R, LANES), lambda s: (tile2(s), 0, 0))
    by_head = pl.BlockSpec((nlb * nh * nk, LANES), lambda s: (tile2(s), 0))
    r1 = r1.reshape(-1, LANES)
    bs = bs.reshape(-1, LANES)
    return pl.pallas_call(
        functools.partial(_peer_main_kernel, nh=nh, nk=nk, ne=ne, n_blocks=n_blocks),
        grid=(n_blocks + 2,),
        in_specs=[pl.BlockSpec((tm, D), lambda s: (tile1(s), 0)),
                  pl.BlockSpec((tm, D), lambda s: (tile3(s), 0)),
                  pl.BlockSpec((te, D), lambda s: (s % ne, 0)),
                  pl.BlockSpec((None, D, te), lambda s: ((s + 2 * ne - 2) % ne, 0, 0)),
                  by_head, by_head, by_row, by_row],
        out_specs=pl.BlockSpec((tm, D), lambda s: (tile3(s), 0)),
        out_shape=jax.ShapeDtypeStruct((N, D), F32),
        scratch_shapes=[pltpu.VMEM((D, tm), BF16), pltpu.VMEM((D, tm), F32)]
        + [pltpu.VMEM((te, tm), BF16)] * 4,
        compiler_params=_cparams(("arbitrary",)),
        name="peer_main",
    )(xn2, x1, u_bf, vt_bf, r1, bs, c0, a_s)


def _rope_tables(pos, head_dim, width):
    half = head_dim // 2
    inv = ROPE_THETA ** (-jnp.arange(half, dtype=F32) / half)
    ang = pos.astype(F32)[:, None] * inv[None, :]
    cos = jnp.cos(ang)
    sin = jnp.sin(ang)
    reps = width // head_dim
    cos_t = jnp.tile(jnp.concatenate([cos, cos], axis=1), (1, reps))
    sin_t = jnp.tile(jnp.concatenate([-sin, sin], axis=1), (1, reps))
    return cos_t, sin_t


def _token_tile(n, pref):
    t = min(pref, n)
    while n % t:
        t //= 2
    return t


def _trunk(x, pos, conv_state, cache_k2, cache_v2, p):
    B, T, D = x.shape
    nh, hd = p["n_heads"], p["head_dim"]
    tm = _token_tile(T, 256)
    cos_t, sin_t = _rope_tables(pos, hd, 128)
    gaa, gb, new_conv = _branch_a(x, p["g1"], p["w_in"], p["b_gate"], p["conv_w"], conv_state, p["wco"], tm)
    k, v, qb, kb, vb = _qkv(x, p["g1"], p["w_in"], p["qg"], p["kg"], cos_t, sin_t, p["grp"], tm, hd)
    if cache_k2 is None:
        on = _attn_prompt(qb, kb, vb, p["lam4"], p["sg"], nh, _token_tile(T, 512), p["lam0"])
    else:
        on = _attn_sample(qb, kb, vb, cache_k2, cache_v2, p["lam4"], p["sg"], nh, p["lam0"])
    N = B * T
    tn = _token_tile(N, 256)
    x1, xn2 = _merge(x.reshape(N, D), gaa.reshape(N, D), gb.reshape(N, D), on.reshape(N, D),
                     p["wao"], p["wo"], p["g2"], tn)
    pnh, nk = p["peer_heads"], p["n_keys"]
    tp = _token_tile(N, 512)
    qg_rows = jnp.broadcast_to(p["peer_qg_col"], (p["peer_qg_col"].shape[0], tp))
    r1, bs, c0, a_s = _peer_select(xn2, p["wqt"], qg_rows, p["bd"], pnh, nk, tp)
    y = _peer_main(xn2, x1, p["u"], p["vt"], r1, bs, c0, a_s, pnh, nk, tp)
    return y.reshape(B, T, D), k, v, new_conv


def kernel(x_prompt, x_sample, cache_k, cache_v, state_conv, norm1_g, w_in, b_gate, conv_w, q_norm_g, k_norm_g,
           lam_q1, lam_k1, lam_q2, lam_k2, subln_g, w_conv_out, w_attn_out, w_o, norm2_g, peer_wq, peer_q_g,
           peer_subkeys, peer_u, peer_v):
    depth = w_in.shape[0]
    assert depth == 1
    l = LAYER_IDX
    D = x_prompt.shape[-1]
    n_heads, head_dim, v_dim = cache_k.shape[3], cache_k.shape[5], cache_v.shape[4]
    assert v_dim == 2 * head_dim and n_heads * v_dim == D and w_in.shape[2] == 8 * D
    n_keys, dk = peer_subkeys.shape[2], peer_subkeys.shape[3]
    assert n_keys == dk
    peer_heads = peer_wq.shape[2] // (2 * dk)
    past = cache_k.shape[2]
    assert past % CHUNK == 0 and (1 << CHUNK_SHIFT) == CHUNK

    eye_h = jnp.eye(peer_heads, dtype=F32)
    grp = np.kron(np.eye(256 // head_dim), np.ones((head_dim, head_dim))).astype(np.float32)
    p = dict(
        n_heads=n_heads, head_dim=head_dim, peer_heads=peer_heads, n_keys=n_keys, lam0=_lambda_init(l),
        g1=norm1_g[l][None, :], w_in=w_in[l].astype(BF16), b_gate=b_gate[l], conv_w=conv_w[l],
        wco=w_conv_out[l].astype(BF16), wao=w_attn_out[l].astype(BF16), wo=w_o[l].astype(BF16),
        qg=jnp.tile(q_norm_g[l], D // head_dim)[None, :], kg=jnp.tile(k_norm_g[l], D // head_dim)[None, :],
        grp=jnp.asarray(grp, BF16),
        lam4=jnp.stack([lam_q1[l], lam_k1[l], lam_q2[l], lam_k2[l]]),
        sg=subln_g[l][None, :], g2=norm2_g[l][None, :],
        wqt=peer_wq[l].reshape(D, peer_heads, 2, dk).transpose(2, 1, 3, 0).reshape(2 * peer_heads * dk, D).astype(BF16),
        peer_qg_col=jnp.broadcast_to(peer_q_g[l].reshape(2, 1, dk), (2, peer_heads, dk)).reshape(-1, 1),
        bd=jnp.einsum("ckd,hg->ckhgd", peer_subkeys[l], eye_h).reshape(2, n_keys * peer_heads, peer_heads * dk).astype(BF16),
        u=peer_u[l].astype(BF16),
        vt=peer_v[l].reshape(-1, PEER_EXPERT_BLOCK, D).transpose(0, 2, 1).astype(BF16),
    )

    B, T, _ = x_prompt.shape
    Bs, Ts, _ = x_sample.shape
    pos_p = jnp.arange(T, dtype=jnp.int32)
    pos_s = past + jnp.arange(Ts, dtype=jnp.int32)
    zero_conv = jnp.zeros((B, 2, D), x_prompt.dtype)
    y_p, k_p, v_p, c_p = _trunk(x_prompt, pos_p, zero_conv, None, None, p)
    y_s, k_s, v_s, c_s = _trunk(x_sample, pos_s, state_conv[l], cache_k[l].reshape(Bs, past, D),
                                cache_v[l].reshape(Bs, past, D), p)
    return (y_p, y_s,
            k_p.reshape(1, B, T, n_heads, 2, head_dim), v_p.reshape(1, B, T, n_heads, v_dim), c_p[None],
            k_s.reshape(1, Bs, Ts, n_heads, 2, head_dim), v_s.reshape(1, Bs, Ts, n_heads, v_dim), c_s[None])
```
